```python
import jax
import jax.numpy as jnp
from jax import lax
import numpy as np

D_MODEL = 1024
BATCH = 8
SEQ = 2048
DEPTH = 4

GRID_W = 64
CTX_LEN = 256
N_MIXERS = 4
N_LAYERS_NA = (DEPTH + N_MIXERS - 1) // N_MIXERS
N_LAYERS_CONV = (DEPTH + N_MIXERS - 2) // N_MIXERS
N_LAYERS_GLA = (DEPTH + N_MIXERS - 3) // N_MIXERS
N_LAYERS_RWKV = (DEPTH + N_MIXERS - 4) // N_MIXERS
N_SUB = 3
ALPHA = (2.0 * DEPTH) ** 0.25
BETA = (8.0 * DEPTH) ** -0.25
LN_EPS = 1e-5
D_FF = 2816
NA_HEADS = 16
NA_HEAD_DIM = D_MODEL // NA_HEADS
NA_KH = 8
NA_KW = 16
NA_QCB = NA_KW
NA_KCB = 2 * NA_KW
NA_NCB = GRID_W // NA_QCB
CONV_WIDTH = 31
GLA_HEADS = 4
GLA_DK = D_MODEL // 2 // GLA_HEADS
GLA_DV = D_MODEL // GLA_HEADS
GLA_GATE_RANK = 16
GLA_NORMALIZER = 16.0
GLA_CHUNK = 64
ROPE_BASE = 10000.0
RW_HEAD = 64
RW_HEADS = D_MODEL // RW_HEAD
RW_DECAY_RANK = 64
RW_A_RANK = 64
RW_GATE_RANK = 128
RW_GN_EPS = 64e-5

kernel_name = 'hybrid_na_conv_gla_rwkv7_dit_block'


def _layer_norm(x, g, b, eps=LN_EPS):
    xf = x.astype(jnp.float32)
    mu = jnp.mean(xf, axis=-1, keepdims=True)
    var = jnp.mean(jnp.square(xf - mu), axis=-1, keepdims=True)
    return ((xf - mu) * lax.rsqrt(var + eps) * g + b).astype(x.dtype)


def _modulate(h, m, j):
    return h * (1 + m[:, :, j, 1]) + m[:, :, j, 0]


def _post_norm(h, y, m, j, g, b):
    return _layer_norm(ALPHA * h + m[:, :, j, 2] * y, g, b)


def _swiglu(h, w13, w2):
    a, u = jnp.split(h @ w13, 2, axis=-1)
    return (jax.nn.silu(a) * u) @ w2


def _rope_1d(x, pos):
    half = x.shape[-1] // 2
    freqs = ROPE_BASE ** (-jnp.arange(half, dtype=jnp.float32) / half)
    ang = pos.astype(jnp.float32)[:, None] * freqs
    cos, sin = jnp.cos(ang)[:, None, :], jnp.sin(ang)[:, None, :]
    x1, x2 = x[..., :half], x[..., half:]
    return jnp.concatenate([x1 * cos - x2 * sin, x1 * sin + x2 * cos], axis=-1).astype(x.dtype)


def _axial_rope(x):
    t = jnp.arange(x.shape[1])
    half = x.shape[-1] // 2
    return jnp.concatenate([_rope_1d(x[..., :half], t // GRID_W), _rope_1d(x[..., half:], t % GRID_W)], axis=-1)


def _centred_shift(h):
    prev = jnp.pad(h, ((0, 0), (1, 0), (0, 0)))[:, :-1]
    nxt = jnp.pad(h, ((0, 0), (0, 1), (0, 0)))[:, 1:]
    return 0.5 * (prev + nxt)


def _neighbourhood_attention(hx, hc, w_qkv, w_o, rpb, ctx_out):
    B, N, D = hx.shape
    L = hc.shape[1]
    rows = N // GRID_W
    kh = min(NA_KH, rows)
    scale = NA_HEAD_DIM ** -0.5
    qkv = (hx @ w_qkv).reshape(B, N, 3, NA_HEADS, NA_HEAD_DIM)
    q, k, v = qkv[:, :, 0] * scale, qkv[:, :, 1], qkv[:, :, 2]
    qkv_c = (hc @ w_qkv).reshape(B, L, 3, NA_HEADS, NA_HEAD_DIM)
    qc, kc, vc = qkv_c[:, :, 0] * scale, qkv_c[:, :, 1], qkv_c[:, :, 2]

    qg = q.reshape(B, rows, NA_NCB, NA_QCB, NA_HEADS, NA_HEAD_DIM)
    kg = k.reshape(B, rows, GRID_W, NA_HEADS, NA_HEAD_DIM)
    vg = v.reshape(B, rows, GRID_W, NA_HEADS, NA_HEAD_DIM)
    qcol = np.arange(GRID_W).reshape(NA_NCB, NA_QCB)
    cstart = np.clip(qcol - NA_KW // 2, 0, GRID_W - NA_KW)
    band0 = np.clip(np.arange(NA_NCB) * NA_QCB - NA_KW // 2, 0, GRID_W - NA_KCB)
    band = band0[:, None] + np.arange(NA_KCB)
    kcol = band[:, None, :]
    col_ok = (kcol >= cstart[..., None]) & (kcol < cstart[..., None] + NA_KW)
    dc_idx = np.clip(kcol - qcol[..., None] + NA_KW - 1, 0, 2 * NA_KW - 2)
    rpb_c = rpb[:, :, dc_idx]

    def row_block(r):
        r0 = jnp.clip(r - kh // 2, 0, rows - kh)
        k_rows = lax.dynamic_slice_in_dim(kg, r0, kh, axis=1)[:, :, band]
        v_rows = lax.dynamic_slice_in_dim(vg, r0, kh, axis=1)[:, :, band]
        q_r = lax.dynamic_index_in_dim(qg, r, axis=1, keepdims=False)
        bias = lax.dynamic_slice_in_dim(rpb_c, r0 - r + NA_KH - 1, kh, axis=1)
        s_loc = jnp.einsum('bjqhd,bajkhd->bhjqak', q_r, k_rows).astype(jnp.float32)
        s_loc = s_loc + bias.transpose(0, 2, 3, 1, 4)[None].astype(jnp.float32)
        s_loc = jnp.where(col_ok[:, :, None, :], s_loc, -jnp.inf)
        s_ctx = jnp.einsum('bjqhd,blhd->bhjql', q_r, kc).astype(jnp.float32)
        s = jnp.concatenate([s_loc.reshape(B, NA_HEADS, NA_NCB, NA_QCB, kh * NA_KCB), s_ctx], axis=-1)
        p = jax.nn.softmax(s, axis=-1).astype(v.dtype)
        p_loc = p[..., :kh * NA_KCB].reshape(B, NA_HEADS, NA_NCB, NA_QCB, kh, NA_KCB)
        p_ctx = p[..., kh * NA_KCB:]
        return (jnp.einsum('bhjqak,bajkhd->bjqhd', p_loc, v_rows)
                + jnp.einsum('bhjql,blhd->bjqhd', p_ctx, vc))

    o = lax.map(row_block, jnp.arange(rows))
    yx = jnp.moveaxis(o, 0, 1).reshape(B, N, D) @ w_o
    yc = None
    if ctx_out:
        sc = jnp.einsum('blhd,bmhd->bhlm', qc, kc).astype(jnp.float32)
        pc = jax.nn.softmax(sc, axis=-1).astype(vc.dtype)
        yc = jnp.einsum('bhlm,bmhd->blhd', pc, vc).reshape(B, L, D) @ w_o
    return yx, yc


def _conv_module(hx, hc, w1, b1, w_dw, b_dw, ln_g, ln_b, w2, b2, ctx_out):
    def branch(h):
        a, gate = jnp.split(h @ w1 + b1, 2, axis=-1)
        u = a * jax.nn.sigmoid(gate)
        u = lax.conv_general_dilated(u, w_dw[:, None, :], (1,), [(CONV_WIDTH // 2, CONV_WIDTH // 2)],
                                     dimension_numbers=('NWC', 'WIO', 'NWC'),
                                     feature_group_count=u.shape[-1]) + b_dw
        u = jax.nn.silu(_layer_norm(u, ln_g, ln_b))
        return u @ w2 + b2
    return branch(hx), (branch(hc) if ctx_out else None)


def _gla_chunked(q, k, v, log_a, s0):
    B, T, H, _ = q.shape
    dv = v.shape[-1]
    n = T // GLA_CHUNK
    chunks = lambda t: t.astype(jnp.float32).reshape(B, n, GLA_CHUNK, H, t.shape[-1])
    q, k, v, log_a = chunks(q), chunks(k), chunks(v), chunks(log_a)
    b = jnp.cumsum(log_a, axis=2)
    b_last = b[:, :, -1:]
    qd = q * jnp.exp(b)
    kd = k * jnp.exp(-b)
    kt = k * jnp.exp(b_last - b)
    mask = jnp.tril(jnp.ones((GLA_CHUNK, GLA_CHUNK), bool))
    att = jnp.where(mask, jnp.einsum('bnchd,bnshd->bnhcs', qd, kd), 0.0)
    o = jnp.einsum('bnhcs,bnshe->bnche', att, v)
    chunk_state = jnp.einsum('bnchd,bnche->nbhde', kt, v)
    chunk_decay = jnp.moveaxis(jnp.exp(b_last[:, :, 0]), 1, 0)

    def step(s, inp):
        dec, st = inp
        return dec[..., None] * s + st, s

    s_fin, s_in = lax.scan(step, s0, (chunk_decay, chunk_state))
    o = o + jnp.einsum('bnchd,nbhde->bnche', qd, s_in)
    return o.reshape(B, T, H, dv), s_fin


def _gla_bidir(q, k, v, log_a, s_f, s_b):
    fl = lambda t: jnp.flip(t, axis=1)
    o_f, s_f = _gla_chunked(q, k, v, log_a[0], s_f)
    o_b, s_b = _gla_chunked(fl(q), fl(k), fl(v), fl(log_a[1]), s_b)
    return o_f + fl(o_b), s_f, s_b


def _gla(hx, hc, w_in, w_a1, w_a2, b_a, norm_g, w_o, ctx_out):
    nqk = GLA_HEADS * GLA_DK

    def project(h, rotary):
        B, T, _ = h.shape
        q, k, v, g = jnp.split(h @ w_in, [nqk, 2 * nqk, 2 * nqk + D_MODEL], axis=-1)
        q = q.reshape(B, T, GLA_HEADS, GLA_DK) * GLA_DK ** -0.5
        k = k.reshape(B, T, GLA_HEADS, GLA_DK)
        if rotary:
            q, k = _axial_rope(q), _axial_rope(k)
        v = v.reshape(B, T, GLA_HEADS, GLA_DV)
        log_a = [(jax.nn.log_sigmoid(((h @ w_a1[d]) @ w_a2[d] + b_a[d]).astype(jnp.float32))
                  / GLA_NORMALIZER).reshape(B, T, GLA_HEADS, GLA_DK) for d in range(2)]
        return q, k, v, g, log_a

    def finish(o, g, dtype):
        B, T, H, dv = o.shape
        o = o * lax.rsqrt(jnp.mean(o * o, axis=-1, keepdims=True) + LN_EPS) * norm_g
        o = o * jax.nn.silu(g.astype(jnp.float32)).reshape(B, T, H, dv)
        return o.reshape(B, T, H * dv).astype(dtype) @ w_o

    qc, kc, vc, gc, lac = project(hc, False)
    qx, kx, vx, gx, lax_ = project(hx, True)
    s0 = jnp.zeros((hx.shape[0], GLA_HEADS, GLA_DK, GLA_DV), jnp.float32)
    oc, sc_f, sc_b = _gla_bidir(qc, kc, vc, lac, s0, s0)
    ox, _, _ = _gla_bidir(qx, kx, vx, lax_, sc_f, sc_b)
    return finish(ox, gx, hx.dtype), (finish(oc, gc, hc.dtype) if ctx_out else None)


def _rwkv7_inputs(h, mu, w_rkv, w0, w1, w2, a0, a1, a2, g1, g2, k_k, k_a):
    B, T, _ = h.shape
    heads = lambda t: t.astype(jnp.float32).reshape(B, T, RW_HEADS, RW_HEAD)
    xx = _centred_shift(h) - h
    xr, xw, xk, xv, xa, xg = (h + xx * mu[j] for j in range(6))
    r = heads(xr @ w_rkv[0])
    k = xk @ w_rkv[1]
    v = heads(xv @ w_rkv[2])
    g = jax.nn.sigmoid(xg @ g1) @ g2
    kk = heads(k * k_k)
    kk = kk / jnp.maximum(jnp.sqrt(jnp.sum(kk * kk, axis=-1, keepdims=True)), 1e-12)
    dirs = []
    for d in range(2):
        w_log = -jax.nn.softplus(-(w0[d] + jnp.tanh(xw @ w1[d]) @ w2[d])) - 0.5
        decay = jnp.exp(-jnp.exp(heads(w_log)))
        a = jax.nn.sigmoid(a0[d] + (xa @ a1[d]) @ a2[d])
        k_d = heads(k * (1 + (a - 1) * k_a))
        dirs.append((decay, k_d, kk * heads(a)))
    return r, v, kk, g, dirs


def _rwkv7_scan(r, decay, k, v, kk, b, s0):
    def step(s, inp):
        r_t, w_t, k_t, v_t, kk_t, b_t = inp
        sa = jnp.einsum('bhvk,bhk->bhv', s, -kk_t)
        s = s * w_t[:, :, None, :] + sa[..., None] * b_t[:, :, None, :] + v_t[..., None] * k_t[:, :, None, :]
        return s, jnp.einsum('bhvk,bhk->bhv', s, r_t)
    xs = tuple(jnp.moveaxis(t, 1, 0) for t in (r, decay, k, v, kk, b))
    s_fin, y = lax.scan(step, s0, xs)
    return jnp.moveaxis(y, 0, 1), s_fin


def _rwkv7_bidir(r, v, kk, dirs, s_f, s_b):
    (w_f, k_f, b_f), (w_b, k_b, b_b) = dirs
    fl = lambda t: jnp.flip(t, axis=1)
    y_f, s_f = _rwkv7_scan(r, w_f, k_f, v, kk, b_f, s_f)
    y_b, s_b = _rwkv7_scan(fl(r), fl(w_b), fl(k_b), fl(v), fl(kk), fl(b_b), s_b)
    return y_f + fl(y_b), s_f, s_b


def _rwkv7(hx, hc, mu, w_rkv, w0, w1, w2, a0, a1, a2, g1, g2, k_k, k_a, r_k, gn_g, gn_b, w_o, ctx_out):
    p = (mu, w_rkv, w0, w1, w2, a0, a1, a2, g1, g2, k_k, k_a)

    def finish(y, r, v, dirs, g, dtype):
        B, T, H, N = y.shape
        m = jnp.mean(y, axis=-1, keepdims=True)
        var = jnp.mean(jnp.square(y - m), axis=-1, keepdims=True)
        yn = ((y - m) * lax.rsqrt(var + RW_GN_EPS)).reshape(B, T, H * N) * gn_g + gn_b
        bonus = (jnp.sum(r * dirs[0][1] * r_k, axis=-1, keepdims=True)
                 + jnp.sum(r * dirs[1][1] * r_k, axis=-1, keepdims=True)) * v
        return ((yn + bonus.reshape(B, T, H * N)) * g).astype(dtype) @ w_o

    rc, vc, kkc, gc, dc = _rwkv7_inputs(hc, *p)
    rx, vx, kkx, gx, dx = _rwkv7_inputs(hx, *p)
    s0 = jnp.zeros((hx.shape[0], RW_HEADS, RW_HEAD, RW_HEAD), jnp.float32)
    yc, sc_f, sc_b = _rwkv7_bidir(rc, vc, kkc, dc, s0, s0)
    yx, _, _ = _rwkv7_bidir(rx, vx, kkx, dx, sc_f, sc_b)
    return finish(yx, rx, vx, dx, gx, hx.dtype), (finish(yc, rc, vc, dc, gc, hc.dtype) if ctx_out else None)


def setup_inputs(seed: int = 0) -> dict:
    key = jax.random.key(seed)
    ks = iter(jax.random.split(key, 64))
    nrm = lambda shape, s: s * jax.random.normal(next(ks), shape, jnp.float32)
    uni = lambda shape, lo, hi: jax.random.uniform(next(ks), shape, jnp.float32, lo, hi)
    D = D_MODEL
    fan = D ** -0.5
    na, nb, nc, nd = N_LAYERS_NA, N_LAYERS_CONV, N_LAYERS_GLA, N_LAYERS_RWKV
    return {
        'x': nrm((BATCH, SEQ, D), 1.0),
        'c': nrm((BATCH, D), 1.0),
        'ctx': nrm((BATCH, CTX_LEN, D), 1.0),
        'c_ctx': nrm((D,), 1.0),
        'ada_w': nrm((DEPTH, D, N_SUB * 3 * D), fan),
        'ada_b': nrm((DEPTH, N_SUB * 3 * D), 0.02),
        'ln_g': 1.0 + nrm((DEPTH, N_SUB, D), 0.02),
        'ln_b': nrm((DEPTH, N_SUB, D), 0.02),
        'ffn_w13': nrm((DEPTH, 2, D, 2 * D_FF), fan),
        'ffn_w2': nrm((DEPTH, 2, D_FF, D), D_FF ** -0.5 * BETA),
        'na_wqkv': nrm((na, D, 3 * D), fan),
        'na_wo': nrm((na, D, D), fan * BETA),
        'na_rpb': nrm((na, NA_HEADS, 2 * NA_KH - 1, 2 * NA_KW - 1), 0.1),
        'cv_w1': nrm((nb, D, 2 * D), fan),
        'cv_b1': nrm((nb, 2 * D), 0.02),
        'cv_wdw': nrm((nb, CONV_WIDTH, D), CONV_WIDTH ** -0.5),
        'cv_bdw': nrm((nb, D), 0.02),
        'cv_ln_g': 1.0 + nrm((nb, D), 0.02),
        'cv_ln_b': nrm((nb, D), 0.02),
        'cv_w2': nrm((nb, D, D), fan * BETA),
        'cv_b2': nrm((nb, D), 0.02),
        'gla_win': nrm((nc, D, 2 * GLA_HEADS * GLA_DK + 2 * D), fan),
        'gla_wa1': nrm((nc, 2, D, GLA_GATE_RANK), fan),
        'gla_wa2': nrm((nc, 2, GLA_GATE_RANK, GLA_HEADS * GLA_DK), GLA_GATE_RANK ** -0.5),
        'gla_ba': nrm((nc, 2, GLA_HEADS * GLA_DK), 0.02),
        'gla_norm_g': 1.0 + nrm((nc, GLA_DV), 0.02),
        'gla_wo': nrm((nc, D, D), fan * BETA),
        'rw_mu': uni((nd, 6, D), 0.0, 1.0),
        'rw_wrkv': nrm((nd, 3, D, D), fan),
        'rw_w0': uni((nd, 2, D), -6.0, -1.0),
        'rw_w1': nrm((nd, 2, D, RW_DECAY_RANK), fan),
        'rw_w2': nrm((nd, 2, RW_DECAY_RANK, D), 0.1 * RW_DECAY_RANK ** -0.5),
        'rw_a0': nrm((nd, 2, D), 0.1),
        'rw_a1': nrm((nd, 2, D, RW_A_RANK), fan),
        'rw_a2': nrm((nd, 2, RW_A_RANK, D), 0.1 * RW_A_RANK ** -0.5),
        'rw_g1': nrm((nd, D, RW_GATE_RANK), fan),
        'rw_g2': nrm((nd, RW_GATE_RANK, D), RW_GATE_RANK ** -0.5),
        'rw_kk': 0.85 + nrm((nd, D), 0.02),
        'rw_ka': 1.0 + nrm((nd, D), 0.02),
        'rw_rk': nrm((nd, RW_HEADS, RW_HEAD), 0.05),
        'rw_gn_g': 1.0 + nrm((nd, D), 0.02),
        'rw_gn_b': nrm((nd, D), 0.02),
        'rw_wo': nrm((nd, D, D), fan * BETA),
    }


def reference(x, c, ctx, c_ctx, ada_w, ada_b, ln_g, ln_b, ffn_w13, ffn_w2,
              na_wqkv, na_wo, na_rpb,
              cv_w1, cv_b1, cv_wdw, cv_bdw, cv_ln_g, cv_ln_b, cv_w2, cv_b2,
              gla_win, gla_wa1, gla_wa2, gla_ba, gla_norm_g, gla_wo,
              rw_mu, rw_wrkv, rw_w0, rw_w1, rw_w2, rw_a0, rw_a1, rw_a2, rw_g1, rw_g2,
              rw_kk, rw_ka, rw_rk, rw_gn_g, rw_gn_b, rw_wo):
    B = x.shape[0]
    h_x, h_c = x, ctx
    cond = jax.nn.silu(jnp.concatenate([c, c_ctx[None, :]], axis=0))
    for i in range(DEPTH):
        last = i == DEPTH - 1
        mod = (cond @ ada_w[i] + ada_b[i]).reshape(B + 1, 1, N_SUB, 3, D_MODEL)
        mx, mc = mod[:B], mod[B:]
        h_x = _post_norm(h_x, 0.5 * _swiglu(_modulate(h_x, mx, 0), ffn_w13[i, 0], ffn_w2[i, 0]), mx, 0, ln_g[i, 0], ln_b[i, 0])
        h_c = _post_norm(h_c, 0.5 * _swiglu(_modulate(h_c, mc, 0), ffn_w13[i, 0], ffn_w2[i, 0]), mc, 0, ln_g[i, 0], ln_b[i, 0])
        ux, uc = _modulate(h_x, mx, 1), _modulate(h_c, mc, 1)
        kind, j = i % N_MIXERS, i // N_MIXERS
        if kind == 0:
            yx, yc = _neighbourhood_attention(ux, uc, na_wqkv[j], na_wo[j], na_rpb[j], not last)
        elif kind == 1:
            yx, yc = _conv_module(ux, uc, cv_w1[j], cv_b1[j], cv_wdw[j], cv_bdw[j], cv_ln_g[j], cv_ln_b[j],
                                  cv_w2[j], cv_b2[j], not last)
        elif kind == 2:
            yx, yc = _gla(ux, uc, gla_win[j], gla_wa1[j], gla_wa2[j], gla_ba[j], gla_norm_g[j], gla_wo[j], not last)
        else:
            yx, yc = _rwkv7(ux, uc, rw_mu[j], rw_wrkv[j], rw_w0[j], rw_w1[j], rw_w2[j], rw_a0[j], rw_a1[j], rw_a2[j],
                            rw_g1[j], rw_g2[j], rw_kk[j], rw_ka[j], rw_rk[j], rw_gn_g[j], rw_gn_b[j], rw_wo[j], not last)
        h_x = _post_norm(h_x, yx, mx, 1, ln_g[i, 1], ln_b[i, 1])
        h_x = _post_norm(h_x, 0.5 * _swiglu(_modulate(h_x, mx, 2), ffn_w13[i, 1], ffn_w2[i, 1]), mx, 2, ln_g[i, 2], ln_b[i, 2])
        if not last:
            h_c = _post_norm(h_c, yc, mc, 1, ln_g[i, 1], ln_b[i, 1])
            h_c = _post_norm(h_c, 0.5 * _swiglu(_modulate(h_c, mc, 2), ffn_w13[i, 1], ffn_w2[i, 1]), mc, 2, ln_g[i, 2], ln_b[i, 2])
    return h_x
```

```python
import functools

import numpy as np
import jax
import jax.numpy as jnp
from jax import lax
from jax.experimental import pallas as pl
from jax.experimental.pallas import tpu as pltpu

F32 = jnp.float32
BF16 = jnp.bfloat16

D_MODEL = 1024
DEPTH = 4
N_SUB = 3
GRID_W = 64
ALPHA = (2.0 * DEPTH) ** 0.25
LN_EPS = 1e-5
D_FF = 2816
NA_HEADS = 16
NA_HEAD_DIM = D_MODEL // NA_HEADS
NA_KH = 8
NA_KW = 16
CONV_WIDTH = 31
CONV_HALO = 16
GLA_HEADS = 4
GLA_DK = 128
GLA_DV = 256
GLA_GATE_RANK = 16
GLA_NORMALIZER = 16.0
GLA_CHUNK = 64
ROPE_BASE = 10000.0
RW_HEAD = 64
RW_HEADS = D_MODEL // RW_HEAD
RW_PAIRS = RW_HEADS // 2
RW_CHUNK = 64
RW_GN_EPS = 64e-5
LANES = 128
NEG_BIG = -1e30
VMEM_LIMIT = 56 * 1024 * 1024


def _cparams(*sem):
    return pltpu.CompilerParams(dimension_semantics=sem, vmem_limit_bytes=VMEM_LIMIT)


def _dot(a, b):
    return jnp.dot(a.astype(BF16), b.astype(BF16), preferred_element_type=F32)


def _dot_nt(a, b):
    return lax.dot_general(a.astype(BF16), b.astype(BF16), (((1,), (1,)), ((), ())),
                           preferred_element_type=F32)


def _dot_tn(a, b):
    return lax.dot_general(a.astype(BF16), b.astype(BF16), (((0,), (0,)), ((), ())),
                           preferred_element_type=F32)


def _split(x):
    hi = x.astype(BF16)
    lo = (x - hi.astype(F32)).astype(BF16)
    return hi, lo


def _dot_sel(x, sel):
    hi, lo = _split(x)
    return (jnp.dot(hi, sel, preferred_element_type=F32)
            + jnp.dot(lo, sel, preferred_element_type=F32))


def _sel_dot(sel, x):
    hi, lo = _split(x)
    return (jnp.dot(sel, hi, preferred_element_type=F32)
            + jnp.dot(sel, lo, preferred_element_type=F32))


def _sigmoid(x):
    return jax.nn.sigmoid(x)


def _softplus(x):
    return jnp.maximum(x, 0.0) + jnp.log(1.0 + jnp.exp(-jnp.abs(x)))


def _layer_norm(x, g, b, eps=LN_EPS):
    mu = jnp.mean(x, axis=-1, keepdims=True)
    xc = x - mu
    var = jnp.mean(xc * xc, axis=-1, keepdims=True)
    return xc * lax.rsqrt(var + eps) * g + b


def _mod_rows(mod_ref, j):
    return (mod_ref[0, 3 * j:3 * j + 1, :], mod_ref[0, 3 * j + 1:3 * j + 2, :],
            mod_ref[0, 3 * j + 2:3 * j + 3, :])


def _modulate(h, mod_ref, j):
    shift, scale, _ = _mod_rows(mod_ref, j)
    return h * (1.0 + scale) + shift


def _post_norm(h, y, mod_ref, j, g_ref, b_ref):
    gate = mod_ref[0, 3 * j + 2:3 * j + 3, :]
    return _layer_norm(ALPHA * h + gate * y, g_ref[...], b_ref[...])


def _tok_spec(tm, d):
    return pl.BlockSpec((1, tm, d), lambda b, t: (b, t, 0))


def _mod_spec():
    return pl.BlockSpec((1, 3 * N_SUB, D_MODEL), lambda b, t: (b, 0, 0))


def _const_spec(shape):
    nd = len(shape)
    return pl.BlockSpec(shape, lambda b, t: (0,) * nd)


def _tile(s, pref):
    tm = min(pref, s)
    assert s % tm == 0
    return tm


def _ada_kernel(cond_ref, w_ref, b_ref, o_ref):
    c = cond_ref[...]
    o_ref[0] = _dot(c * _sigmoid(c), w_ref[0]) + b_ref[0]


def _ada(cond, ada_w, ada_b):
    r, d = cond.shape
    depth, _, n = ada_w.shape
    tn = 1024
    return pl.pallas_call(
        _ada_kernel,
        grid=(depth, n // tn),
        in_specs=[pl.BlockSpec((r, d), lambda l, j: (0, 0)),
                  pl.BlockSpec((1, d, tn), lambda l, j: (l, 0, j)),
                  pl.BlockSpec((1, 1, tn), lambda l, j: (l, 0, j))],
        out_specs=pl.BlockSpec((1, r, tn), lambda l, j: (l, 0, j)),
        out_shape=jax.ShapeDtypeStruct((depth, r, n), F32),
        compiler_params=_cparams("parallel", "parallel"),
        name="ada_mod",
    )(cond, ada_w, ada_b.reshape(depth, 1, n))


FFN_CHUNK = 256


def _ffn_kernel(h_ref, mod_ref, w13_ref, w2_ref, g_ref, b_ref, o_ref, *, j):
    h = h_ref[0]
    u = _modulate(h, mod_ref, j).astype(BF16)
    acc = jnp.zeros(h.shape, F32)
    for c in range(D_FF // FFN_CHUNK):
        lo = c * FFN_CHUNK
        a = jnp.dot(u, w13_ref[:, lo:lo + FFN_CHUNK], preferred_element_type=F32)
        v = jnp.dot(u, w13_ref[:, D_FF + lo:D_FF + lo + FFN_CHUNK], preferred_element_type=F32)
        z = (a * _sigmoid(a) * v).astype(BF16)
        acc = acc + jnp.dot(z, w2_ref[lo:lo + FFN_CHUNK, :], preferred_element_type=F32)
    o_ref[0] = _post_norm(h, 0.5 * acc, mod_ref, j, g_ref, b_ref)


def _ffn(h, mod, w13, w2, ln_g, ln_b, j):
    b, s, d = h.shape
    tm = _tile(s, 256)
    return pl.pallas_call(
        functools.partial(_ffn_kernel, j=j),
        grid=(b, s // tm),
        in_specs=[_tok_spec(tm, d), _mod_spec(), _const_spec(w13.shape), _const_spec(w2.shape),
                  _const_spec((1, d)), _const_spec((1, d))],
        out_specs=_tok_spec(tm, d),
        out_shape=jax.ShapeDtypeStruct(h.shape, F32),
        compiler_params=_cparams("parallel", "parallel"),
        name="ffn_half",
    )(h, mod, w13, w2, ln_g.reshape(1, d), ln_b.reshape(1, d))


def _out_kernel(*refs, n_in, prologue, has_bias):
    ins = refs[:n_in]
    w_ref = refs[n_in]
    k = n_in + 1
    bias_ref = None
    if has_bias:
        bias_ref = refs[k]
        k += 1
    h_ref, mod_ref, g_ref, b_ref, o_ref = refs[k:k + 5]
    y = jnp.dot(prologue(*ins), w_ref[...], preferred_element_type=F32)
    if has_bias:
        y = y + bias_ref[...]
    o_ref[0] = _post_norm(h_ref[0], y, mod_ref, 1, g_ref, b_ref)


def _out_proj(tok_inputs, const_inputs, prologue, w, bias, h, mod, ln_g, ln_b, name):
    b, s, d = h.shape
    tm = _tile(s, 256)
    n_in = len(tok_inputs) + len(const_inputs)
    in_specs = [_tok_spec(tm, a.shape[-1]) for a in tok_inputs]
    in_specs += [_const_spec(a.shape) for a in const_inputs]
    in_specs.append(_const_spec(w.shape))
    args = list(tok_inputs) + list(const_inputs) + [w]
    if bias is not None:
        in_specs.append(_const_spec((1, d)))
        args.append(bias.reshape(1, d))
    in_specs += [_tok_spec(tm, d), _mod_spec(), _const_spec((1, d)), _const_spec((1, d))]
    args += [h, mod, ln_g.reshape(1, d), ln_b.reshape(1, d)]
    return pl.pallas_call(
        functools.partial(_out_kernel, n_in=n_in, prologue=prologue, has_bias=bias is not None),
        grid=(b, s // tm),
        in_specs=in_specs,
        out_specs=_tok_spec(tm, d),
        out_shape=jax.ShapeDtypeStruct(h.shape, F32),
        compiler_params=_cparams("parallel", "parallel"),
        name=name,
    )(*args)


def _plain_prologue(y_ref):
    return y_ref[0].astype(BF16)


def _na_qkv_kernel(h_ref, mod_ref, w_ref, q_ref, k_ref, v_ref):
    d = D_MODEL
    u = _modulate(h_ref[0], mod_ref, 1).astype(BF16)
    z = jnp.dot(u, w_ref[...], preferred_element_type=F32)
    q_ref[0] = (z[:, :d] * (NA_HEAD_DIM ** -0.5)).astype(BF16)
    k_ref[0] = z[:, d:2 * d].astype(BF16)
    v_ref[0] = z[:, 2 * d:].astype(BF16)


def _na_qkv(h, mod, w):
    b, s, d = h.shape
    tm = _tile(s, 256)
    out = jax.ShapeDtypeStruct(h.shape, BF16)
    return pl.pallas_call(
        _na_qkv_kernel,
        grid=(b, s // tm),
        in_specs=[_tok_spec(tm, d), _mod_spec(), _const_spec(w.shape)],
        out_specs=[_tok_spec(tm, d)] * 3,
        out_shape=[out] * 3,
        compiler_params=_cparams("parallel", "parallel"),
        name="na_qkv",
    )(h, mod, w)


def _na_row_start(r, rows, kh):
    return jnp.clip(r - kh // 2, 0, rows - kh)


def _na_attn_kernel(q_ref, k_ref, v_ref, kc_ref, vc_ref, bias_ref, o_ref, *, rows, kh):
    r = pl.program_id(1)
    start = pl.multiple_of(_na_row_start(r, rows, kh) * GRID_W, GRID_W)
    win = kh * GRID_W
    for hd in range(NA_HEADS):
        sl = slice(hd * NA_HEAD_DIM, (hd + 1) * NA_HEAD_DIM)
        q = q_ref[0, :, sl]
        k = k_ref[0, pl.ds(start, win), sl]
        v = v_ref[0, pl.ds(start, win), sl]
        kc = kc_ref[0, :, sl]
        vc = vc_ref[0, :, sl]
        s_loc = _dot_nt(q, k) + bias_ref[0, hd]
        s_ctx = _dot_nt(q, kc)
        m = jnp.maximum(jnp.max(s_loc, axis=-1, keepdims=True), jnp.max(s_ctx, axis=-1, keepdims=True))
        p_loc = jnp.exp(s_loc - m)
        p_ctx = jnp.exp(s_ctx - m)
        den = jnp.sum(p_loc, axis=-1, keepdims=True) + jnp.sum(p_ctx, axis=-1, keepdims=True)
        o = _dot(p_loc, v) + _dot(p_ctx, vc)
        o_ref[0, :, sl] = (o / den).astype(BF16)


def _na_bias_table(rpb, kh):
    qc = np.arange(GRID_W)[:, None]
    kc = np.arange(GRID_W)[None, :]
    cstart = np.clip(qc - NA_KW // 2, 0, GRID_W - NA_KW)
    ok = (kc >= cstart) & (kc < cstart + NA_KW)
    dc = np.clip(kc - qc + NA_KW - 1, 0, 2 * NA_KW - 2)
    tbl = jnp.where(ok[None, None], rpb[:, :, dc], NEG_BIG)
    n_s = 2 * NA_KH - kh
    out = []
    for s in range(n_s):
        t = tbl[:, s:s + kh]
        out.append(jnp.transpose(t, (0, 2, 1, 3)).reshape(NA_HEADS, GRID_W, kh * GRID_W))
    return jnp.stack(out, axis=0)


def _na_attn(q, k, v, kc, vc, bias_tab):
    b, n, d = q.shape
    l = kc.shape[1]
    rows = n // GRID_W
    kh = min(NA_KH, rows)

    def bias_idx(bi, r):
        return (_na_row_start(r, rows, kh) - r + NA_KH - 1, 0, 0, 0)

    return pl.pallas_call(
        functools.partial(_na_attn_kernel, rows=rows, kh=kh),
        grid=(b, rows),
        in_specs=[pl.BlockSpec((1, GRID_W, d), lambda bi, r: (bi, r, 0)),
                  pl.BlockSpec((1, n, d), lambda bi, r: (bi, 0, 0)),
                  pl.BlockSpec((1, n, d), lambda bi, r: (bi, 0, 0)),
                  pl.BlockSpec((1, l, d), lambda bi, r: (bi, 0, 0)),
                  pl.BlockSpec((1, l, d), lambda bi, r: (bi, 0, 0)),
                  pl.BlockSpec((1, NA_HEADS, GRID_W, kh * GRID_W), bias_idx)],
        out_specs=pl.BlockSpec((1, GRID_W, d), lambda bi, r: (bi, r, 0)),
        out_shape=jax.ShapeDtypeStruct(q.shape, BF16),
        compiler_params=_cparams("parallel", "arbitrary"),
        name="na_attn",
    )(q, k, v, kc, vc, bias_tab)


def _ctx_attn_kernel(q_ref, k_ref, v_ref, o_ref):
    for hd in range(NA_HEADS):
        sl = slice(hd * NA_HEAD_DIM, (hd + 1) * NA_HEAD_DIM)
        s = _dot_nt(q_ref[0, :, sl], k_ref[0, :, sl])
        p = jnp.exp(s - jnp.max(s, axis=-1, keepdims=True))
        den = jnp.sum(p, axis=-1, keepdims=True)
        o_ref[0, :, sl] = (_dot(p, v_ref[0, :, sl]) / den).astype(BF16)


def _ctx_attn(q, k, v):
    b, l, d = q.shape
    spec = pl.BlockSpec((1, l, d), lambda bi: (bi, 0, 0))
    return pl.pallas_call(
        _ctx_attn_kernel,
        grid=(b,),
        in_specs=[spec] * 3,
        out_specs=spec,
        out_shape=jax.ShapeDtypeStruct(q.shape, BF16),
        compiler_params=_cparams("parallel"),
        name="ctx_attn",
    )(q, k, v)


def _mixer_na(hx, hc, mx, mc, w_qkv, w_o, rpb, ln_g, ln_b, ctx_out):
    w_qkv = w_qkv.astype(BF16)
    w_o = w_o.astype(BF16)
    qx, kx, vx = _na_qkv(hx, mx, w_qkv)
    qc, kc, vc = _na_qkv(hc, mc, w_qkv)
    kh = min(NA_KH, hx.shape[1] // GRID_W)
    ox = _na_attn(qx, kx, vx, kc, vc, _na_bias_table(rpb, kh))
    hx = _out_proj([ox], [], _plain_prologue, w_o, None, hx, mx, ln_g, ln_b, "na_out")
    if ctx_out:
        oc = _ctx_attn(qc, kc, vc)
        hc = _out_proj([oc], [], _plain_prologue, w_o, None, hc, mc, ln_g, ln_b, "na_out_ctx")
    return hx, hc


def _conv_in_kernel(h_ref, mod_ref, w_ref, b_ref, o_ref):
    d = D_MODEL
    u = _modulate(h_ref[0], mod_ref, 1).astype(BF16)
    z = jnp.dot(u, w_ref[...], preferred_element_type=F32) + b_ref[...]
    o_ref[0] = z[:, :d] * _sigmoid(z[:, d:])


def _conv_in(h, mod, w1, b1):
    b, s, d = h.shape
    tm = _tile(s, 256)
    return pl.pallas_call(
        _conv_in_kernel,
        grid=(b, s // tm),
        in_specs=[_tok_spec(tm, d), _mod_spec(), _const_spec(w1.shape), _const_spec((1, 2 * d))],
        out_specs=_tok_spec(tm, d),
        out_shape=jax.ShapeDtypeStruct(h.shape, F32),
        compiler_params=_cparams("parallel", "parallel"),
        name="conv_in",
    )(h, mod, w1, b1.reshape(1, 2 * d))


CONV_ROWS = 16


def _conv_dw_kernel(prev_ref, cur_ref, next_ref, w_ref, bdw_ref, g_ref, b_ref, o_ref, win_ref, *, ts, nt):
    t = pl.program_id(1)
    halo = CONV_HALO
    d = cur_ref.shape[-1]
    win_ref[halo:halo + ts, :] = cur_ref[0]
    win_ref[0:halo, :] = jnp.where(t > 0, prev_ref[0], 0.0)
    win_ref[halo + ts:halo + ts + halo, :] = jnp.where(t < nt - 1, next_ref[0], 0.0)
    off = halo - CONV_WIDTH // 2
    for g in range(ts // CONV_ROWS):
        base = g * CONV_ROWS + off
        acc = jnp.broadcast_to(bdw_ref[...], (CONV_ROWS, d))
        for k in range(CONV_WIDTH):
            acc = acc + win_ref[base + k:base + k + CONV_ROWS, :] * w_ref[k:k + 1, :]
        y = _layer_norm(acc, g_ref[...], b_ref[...])
        o_ref[0, g * CONV_ROWS:(g + 1) * CONV_ROWS, :] = (y * _sigmoid(y)).astype(BF16)


def _conv_dw(u, w_dw, b_dw, ln_g, ln_b):
    b, s, d = u.shape
    ts = _tile(s, 128)
    nt = s // ts
    hb = ts // CONV_HALO
    nhb = s // CONV_HALO
    return pl.pallas_call(
        functools.partial(_conv_dw_kernel, ts=ts, nt=nt),
        grid=(b, nt),
        in_specs=[pl.BlockSpec((1, CONV_HALO, d), lambda bi, t: (bi, jnp.maximum(t * hb - 1, 0), 0)),
                  _tok_spec(ts, d),
                  pl.BlockSpec((1, CONV_HALO, d), lambda bi, t: (bi, jnp.minimum((t + 1) * hb, nhb - 1), 0)),
                  _const_spec(w_dw.shape), _const_spec((1, d)), _const_spec((1, d)), _const_spec((1, d))],
        out_specs=_tok_spec(ts, d),
        out_shape=jax.ShapeDtypeStruct(u.shape, BF16),
        scratch_shapes=[pltpu.VMEM((ts + 2 * CONV_HALO, d), F32)],
        compiler_params=_cparams("parallel", "arbitrary"),
        name="conv_dw",
    )(u, u, u, w_dw, b_dw.reshape(1, d), ln_g.reshape(1, d), ln_b.reshape(1, d))


def _mixer_conv(hx, hc, mx, mc, w1, b1, w_dw, b_dw, cln_g, cln_b, w2, b2, ln_g, ln_b, ctx_out):
    w1 = w1.astype(BF16)
    w2 = w2.astype(BF16)

    def branch(h, mod, name):
        u = _conv_in(h, mod, w1, b1)
        y = _conv_dw(u, w_dw, b_dw, cln_g, cln_b)
        return _out_proj([y], [], _plain_prologue, w2, b2, h, mod, ln_g, ln_b, name)

    hx = branch(hx, mx, "conv_out")
    if ctx_out:
        hc = branch(hc, mc, "conv_out_ctx")
    return hx, hc


GLA_NQK = GLA_HEADS * GLA_DK


def _gla_in_kernel(*refs, rotary):
    if rotary:
        h_ref, mod_ref, w_ref, wa2_ref, ba_ref, cos_ref, sin_ref = refs[:7]
        outs = refs[7:]
    else:
        h_ref, mod_ref, w_ref, wa2_ref, ba_ref = refs[:5]
        outs = refs[5:]
    q_ref, k_ref, v_ref, g_ref, laf_ref, lab_ref = outs
    n, d = GLA_NQK, D_MODEL
    u = _modulate(h_ref[0], mod_ref, 1).astype(BF16)
    z = jnp.dot(u, w_ref[...], preferred_element_type=F32)
    q = z[:, :n]
    k = z[:, n:2 * n]
    o = 2 * n
    if rotary:
        cos = cos_ref[...]
        sin = sin_ref[...]
        q = q * cos + z[:, o:o + n] * sin
        k = k * cos + z[:, o + n:o + 2 * n] * sin
        o += 2 * n
    q_ref[0] = q * (GLA_DK ** -0.5)
    k_ref[0] = k
    v_ref[0] = z[:, o:o + d].astype(BF16)
    g_ref[0] = z[:, o + d:o + 2 * d]
    t = z[:, o + 2 * d:o + 2 * d + LANES].astype(BF16)
    for dr, la_ref in enumerate((laf_ref, lab_ref)):
        x = jnp.dot(t, wa2_ref[dr], preferred_element_type=F32) + ba_ref[dr:dr + 1, :]
        la_ref[0] = -_softplus(-x) * (1.0 / GLA_NORMALIZER)


def _gla_rope_tables(n):
    quarter = GLA_DK // 4
    t = np.arange(n)
    freqs = ROPE_BASE ** (-jnp.arange(quarter, dtype=F32) / quarter)
    ang_r = jnp.asarray(t // GRID_W, F32)[:, None] * freqs
    ang_c = jnp.asarray(t % GRID_W, F32)[:, None] * freqs
    cos = jnp.concatenate([jnp.cos(ang_r), jnp.cos(ang_r), jnp.cos(ang_c), jnp.cos(ang_c)], axis=-1)
    sin = jnp.concatenate([-jnp.sin(ang_r), jnp.sin(ang_r), -jnp.sin(ang_c), jnp.sin(ang_c)], axis=-1)
    return jnp.tile(cos, (1, GLA_HEADS)), jnp.tile(sin, (1, GLA_HEADS))


def _gla_swap_perm():
    quarter = GLA_DK // 4
    base = np.concatenate([np.arange(quarter, 2 * quarter), np.arange(0, quarter),
                           np.arange(3 * quarter, 4 * quarter), np.arange(2 * quarter, 3 * quarter)])
    return np.concatenate([h * GLA_DK + base for h in range(GLA_HEADS)])


def _gla_in(h, mod, w_ext, wa2_ext, ba, rope):
    b, s, d = h.shape
    tm = _tile(s, 256)
    rotary = rope is not None
    in_specs = [_tok_spec(tm, d), _mod_spec(), _const_spec(w_ext.shape), _const_spec(wa2_ext.shape),
                _const_spec(ba.shape)]
    args = [h, mod, w_ext, wa2_ext, ba]
    if rotary:
        in_specs += [pl.BlockSpec((tm, GLA_NQK), lambda bi, t: (t, 0))] * 2
        args += list(rope)
    shp = lambda w, dt: jax.ShapeDtypeStruct((b, s, w), dt)
    return pl.pallas_call(
        functools.partial(_gla_in_kernel, rotary=rotary),
        grid=(b, s // tm),
        in_specs=in_specs,
        out_specs=[_tok_spec(tm, GLA_NQK), _tok_spec(tm, GLA_NQK), _tok_spec(tm, d), _tok_spec(tm, d),
                   _tok_spec(tm, GLA_NQK), _tok_spec(tm, GLA_NQK)],
        out_shape=[shp(GLA_NQK, F32), shp(GLA_NQK, F32), shp(d, BF16), shp(d, F32),
                   shp(GLA_NQK, F32), shp(GLA_NQK, F32)],
        compiler_params=_cparams("parallel", "parallel"),
        name="gla_in_rope" if rotary else "gla_in",
    )(*args)


def _tri_masks(c):
    row = lax.broadcasted_iota(jnp.int32, (c, c), 0)
    col = lax.broadcasted_iota(jnp.int32, (c, c), 1)
    return row >= col, row <= col


def _gla_core_kernel(qf_ref, kf_ref, vf_ref, laf_ref, qb_ref, kb_ref, vb_ref, lab_ref, s0f_ref, s0b_ref,
                     of_ref, ob_ref, sf_ref, sb_ref, stf_ref, stb_ref, *, nchunks):
    i = pl.program_id(1)
    c = GLA_CHUNK

    @pl.when(i == 0)
    def _():
        stf_ref[...] = s0f_ref[0]
        stb_ref[...] = s0b_ref[0]

    lower, upper = _tri_masks(c)
    ones = jnp.ones((c, GLA_DK), BF16)
    dirs = ((qf_ref, kf_ref, vf_ref, laf_ref, of_ref, stf_ref, lower, c - 1),
            (qb_ref, kb_ref, vb_ref, lab_ref, ob_ref, stb_ref, upper, 0))
    for q_ref, k_ref, v_ref, la_ref, o_ref, st_ref, mask, last in dirs:
        tri = mask.astype(BF16)
        for hd in range(GLA_HEADS):
            ks = slice(hd * GLA_DK, (hd + 1) * GLA_DK)
            vs = slice(hd * GLA_DV, (hd + 1) * GLA_DV)
            q = q_ref[0, :, ks]
            k = k_ref[0, :, ks]
            v = v_ref[0, :, vs]
            hi, lo = _split(la_ref[0, :, ks])
            bcum = (jnp.dot(tri, hi, preferred_element_type=F32)
                    + jnp.dot(tri, lo, preferred_element_type=F32))
            blast = bcum[last:last + 1, :]
            qd = q * jnp.exp(bcum)
            kd = k * jnp.exp(-bcum)
            kt = k * jnp.exp(blast - bcum)
            att = jnp.where(mask, _dot_nt(qd, kd), 0.0)
            s_in = st_ref[hd]
            o_ref[0, :, vs] = _dot(att, v) + _dot(qd, s_in)
            tot = _dot_tn(hi, ones) + _dot_tn(lo, ones)
            dec = jnp.exp(tot)
            st_ref[hd] = jnp.concatenate([dec, dec], axis=1) * s_in + _dot_tn(kt, v)

    @pl.when(i == nchunks - 1)
    def _():
        sf_ref[0] = stf_ref[...]
        sb_ref[0] = stb_ref[...]


def _gla_core(q, k, v, la_f, la_b, s0f, s0b):
    b, s, _ = q.shape
    c = GLA_CHUNK
    nchunks = s // c
    fwd = lambda w: pl.BlockSpec((1, c, w), lambda bi, i: (bi, i, 0))
    bwd = lambda w: pl.BlockSpec((1, c, w), lambda bi, i: (bi, nchunks - 1 - i, 0))
    st_shape = (GLA_HEADS, GLA_DK, GLA_DV)
    st_spec = pl.BlockSpec((1,) + st_shape, lambda bi, i: (bi, 0, 0, 0))
    return pl.pallas_call(
        functools.partial(_gla_core_kernel, nchunks=nchunks),
        grid=(b, nchunks),
        in_specs=[fwd(GLA_NQK), fwd(GLA_NQK), fwd(D_MODEL), fwd(GLA_NQK),
                  bwd(GLA_NQK), bwd(GLA_NQK), bwd(D_MODEL), bwd(GLA_NQK), st_spec, st_spec],
        out_specs=[fwd(D_MODEL), bwd(D_MODEL), st_spec, st_spec],
        out_shape=[jax.ShapeDtypeStruct((b, s, D_MODEL), F32)] * 2
        + [jax.ShapeDtypeStruct((b,) + st_shape, F32)] * 2,
        scratch_shapes=[pltpu.VMEM(st_shape, F32), pltpu.VMEM(st_shape, F32)],
        compiler_params=_cparams("parallel", "arbitrary"),
        name="gla_core",
    )(q, k, v, la_f, q, k, v, la_b, s0f, s0b)


def _gla_finish_prologue(of_ref, ob_ref, g_ref, ng_ref):
    o = of_ref[0] + ob_ref[0]
    g = g_ref[0]
    ng = ng_ref[...]
    parts = []
    for hd in range(GLA_HEADS):
        vs = slice(hd * GLA_DV, (hd + 1) * GLA_DV)
        oh = o[:, vs]
        gh = g[:, vs]
        oh = oh * lax.rsqrt(jnp.mean(oh * oh, axis=-1, keepdims=True) + LN_EPS) * ng
        parts.append((oh * (gh * _sigmoid(gh))).astype(BF16))
    return jnp.concatenate(parts, axis=1)


def _mixer_gla(hx, hc, mx, mc, w_in, w_a1, w_a2, b_a, norm_g, w_o, ln_g, ln_b, ctx_out):
    b, n, d = hx.shape
    nqk = GLA_NQK
    pad = jnp.zeros((d, LANES - 2 * GLA_GATE_RANK), F32)
    tail = [w_in[:, 2 * nqk:], w_a1[0], w_a1[1], pad]
    w_plain = jnp.concatenate([w_in[:, :2 * nqk]] + tail, axis=1).astype(BF16)
    perm = _gla_swap_perm()
    w_rope = jnp.concatenate([w_in[:, :2 * nqk], w_in[:, :nqk][:, perm], w_in[:, nqk:2 * nqk][:, perm]] + tail,
                             axis=1).astype(BF16)
    wa2_ext = jnp.zeros((2, LANES, nqk), F32)
    wa2_ext = wa2_ext.at[0, :GLA_GATE_RANK].set(w_a2[0])
    wa2_ext = wa2_ext.at[1, GLA_GATE_RANK:2 * GLA_GATE_RANK].set(w_a2[1]).astype(BF16)
    w_o = w_o.astype(BF16)

    qc, kc, vc, gc, lcf, lcb = _gla_in(hc, mc, w_plain, wa2_ext, b_a, None)
    qx, kx, vx, gx, lxf, lxb = _gla_in(hx, mx, w_rope, wa2_ext, b_a, _gla_rope_tables(n))
    s0 = jnp.zeros((b, GLA_HEADS, GLA_DK, GLA_DV), F32)
    ocf, ocb, scf, scb = _gla_core(qc, kc, vc, lcf, lcb, s0, s0)
    oxf, oxb, _, _ = _gla_core(qx, kx, vx, lxf, lxb, scf, scb)
    ng = norm_g.reshape(1, GLA_DV)
    hx = _out_proj([oxf, oxb, gx], [ng], _gla_finish_prologue, w_o, None, hx, mx, ln_g, ln_b, "gla_out")
    if ctx_out:
        hc = _out_proj([ocf, ocb, gc], [ng], _gla_finish_prologue, w_o, None, hc, mc, ln_g, ln_b, "gla_out_ctx")
    return hx, hc


def _head_sum_matrices():
    g = np.zeros((D_MODEL, LANES), np.float32)
    g[np.arange(D_MODEL), np.arange(D_MODEL) // RW_HEAD] = 1.0
    return jnp.asarray(g, BF16), jnp.asarray(g.T.copy(), BF16)


def _rw_prep_kernel(hp_ref, h_ref, hn_ref, mod_ref, mu_ref, wr_ref, wk_ref, wv_ref, g1_ref, g2_ref,
                    w1_ref, w2_ref, a1_ref, a2_ref, w0_ref, a0_ref, kkw_ref, kaw_ref, rkw_ref, gs_ref, gt_ref,
                    r_ref, kk_ref, v_ref, g_ref, bonus_ref,
                    lw0_ref, k0_ref, b0_ref, lw1_ref, k1_ref, b1_ref, *, tm, nt):
    t = pl.program_id(1)
    u = _modulate(h_ref[0], mod_ref, 1)
    up = _modulate(hp_ref[0, 7:8, :], mod_ref, 1)
    un = _modulate(hn_ref[0, 0:1, :], mod_ref, 1)
    up = jnp.where(t > 0, up, 0.0)
    un = jnp.where(t < nt - 1, un, 0.0)
    rowid = lax.broadcasted_iota(jnp.int32, (tm, 1), 0)
    prev = jnp.where(rowid == 0, up, pltpu.roll(u, 1, axis=0))
    nxt = jnp.where(rowid == tm - 1, un, pltpu.roll(u, tm - 1, axis=0))
    xx = 0.5 * (prev + nxt) - u
    mix = lambda j: (u + xx * mu_ref[j:j + 1, :]).astype(BF16)
    gs = gs_ref[...]
    gt = gt_ref[...]

    r = jnp.dot(mix(0), wr_ref[...], preferred_element_type=F32)
    k = jnp.dot(mix(2), wk_ref[...], preferred_element_type=F32)
    v = jnp.dot(mix(3), wv_ref[...], preferred_element_type=F32)
    hw = jnp.tanh(jnp.dot(mix(1), w1_ref[...], preferred_element_type=F32)).astype(BF16)
    ha = jnp.dot(mix(4), a1_ref[...], preferred_element_type=F32).astype(BF16)
    gg = _sigmoid(jnp.dot(mix(5), g1_ref[...], preferred_element_type=F32)).astype(BF16)
    g_ref[0] = jnp.dot(gg, g2_ref[...], preferred_element_type=F32)

    kk = k * kkw_ref[...]
    nrm = jnp.maximum(jnp.sqrt(_dot_sel(kk * kk, gs)), 1e-12)
    kk = kk * _dot_sel(1.0 / nrm, gt)
    ksum = jnp.zeros_like(k)
    for dr, (lw_ref, kd_ref, bd_ref) in enumerate(((lw0_ref, k0_ref, b0_ref), (lw1_ref, k1_ref, b1_ref))):
        xw = w0_ref[dr:dr + 1, :] + jnp.dot(hw, w2_ref[dr], preferred_element_type=F32)
        w_log = -_softplus(-xw) - 0.5
        lw_ref[0] = -jnp.exp(w_log)
        a = _sigmoid(a0_ref[dr:dr + 1, :] + jnp.dot(ha, a2_ref[dr], preferred_element_type=F32))
        kd = k * (1.0 + (a - 1.0) * kaw_ref[...])
        kd_ref[0] = kd
        bd_ref[0] = kk * a
        ksum = ksum + kd
    r_ref[0] = r
    kk_ref[0] = kk
    v_ref[0] = v.astype(BF16)
    bonus_ref[0] = _dot_sel(_dot_sel(r * rkw_ref[...] * ksum, gs), gt) * v


def _rw_prep(h, mod, consts):
    b, s, d = h.shape
    tm = _tile(s, 128)
    nt = s // tm
    hb = tm // 8
    nhb = s // 8
    in_specs = [pl.BlockSpec((1, 8, d), lambda bi, t: (bi, jnp.maximum(t * hb - 1, 0), 0)),
                _tok_spec(tm, d),
                pl.BlockSpec((1, 8, d), lambda bi, t: (bi, jnp.minimum((t + 1) * hb, nhb - 1), 0)),
                _mod_spec()] + [_const_spec(a.shape) for a in consts]
    f32o = jax.ShapeDtypeStruct(h.shape, F32)
    out_shape = [f32o, f32o, jax.ShapeDtypeStruct(h.shape, BF16), f32o, f32o] + [f32o] * 6
    return pl.pallas_call(
        functools.partial(_rw_prep_kernel, tm=tm, nt=nt),
        grid=(b, nt),
        in_specs=in_specs,
        out_specs=[_tok_spec(tm, d)] * 11,
        out_shape=out_shape,
        compiler_params=_cparams("parallel", "arbitrary"),
        name="rw_prep",
    )(h, h, h, mod, *consts)


def _rw_masks():
    c = RW_CHUNK
    row = lax.broadcasted_iota(jnp.int32, (2 * c, 2 * c), 0)
    col = lax.broadcasted_iota(jnp.int32, (2 * c, 2 * c), 1)
    rm, cm = row % c, col % c
    row_n = lax.broadcasted_iota(jnp.int32, (2 * c, c), 0) % c
    col_n = lax.broadcasted_iota(jnp.int32, (2 * c, c), 1)
    same_head = (row < c) == (col < c)
    return rm, cm, row_n, col_n, same_head


def _rw_scan_dir(refs, st_ref, y_ref, tri_mask, last, forward):
    r_ref, kk_ref, v_ref, lw_ref, k_ref, b_ref = refs
    c = RW_CHUNK
    rm, cm, row_n, col_n, same_head = _rw_masks()
    if forward:
        np_mask, strict, incl = rm > cm, row_n > col_n, row_n >= col_n
    else:
        np_mask, strict, incl = rm < cm, row_n < col_n, row_n <= col_n
    tri = tri_mask.astype(BF16)
    row2 = lax.broadcasted_iota(jnp.int32, (2 * c, 2 * c), 0)
    col2 = lax.broadcasted_iota(jnp.int32, (2 * c, 2 * c), 1)
    eye = (row2 == col2).astype(F32)
    blocks = [(row2 // s) == (col2 // s) for s in (2 ** e for e in range(1, int(np.log2(c)) + 1))]
    lane0 = lax.broadcasted_iota(jnp.int32, (1, LANES), 1) < RW_HEAD
    ones = jnp.ones((c, LANES), BF16)

    lw = lw_ref[0]
    hi, lo = _split(lw)
    cum = jnp.dot(tri, hi, preferred_element_type=F32) + jnp.dot(tri, lo, preferred_element_type=F32)
    tot = cum[last:last + 1, :]
    e_neg = jnp.exp(-cum)
    e_tail = jnp.exp(tot - cum)
    a_all = (kk_ref[0] * jnp.exp(cum - lw)).astype(BF16)
    r_all = (r_ref[0] * jnp.exp(cum)).astype(BF16)
    b_raw = b_ref[0]
    k_raw = k_ref[0]
    bt_all = (b_raw * e_neg).astype(BF16)
    kt_all = (k_raw * e_neg).astype(BF16)
    bh_all = (b_raw * e_tail).astype(BF16)
    kh_all = (k_raw * e_tail).astype(BF16)
    v_all = v_ref[0]
    zero = jnp.zeros((c, LANES), BF16)

    def stack2(x):
        return jnp.concatenate([jnp.where(lane0, x, zero), jnp.where(lane0, zero, x)], axis=0)

    for p in range(RW_PAIRS):
        sl = slice(p * LANES, (p + 1) * LANES)
        at, rt, bt, kt, v = a_all[:, sl], r_all[:, sl], bt_all[:, sl], kt_all[:, sl], v_all[:, sl]
        atm, rtm, btm = stack2(at), stack2(rt), stack2(bt)
        n1 = jnp.where(np_mask & same_head, _dot_nt(atm, btm), 0.0)
        a_ak = jnp.where(strict, _dot_nt(atm, kt), 0.0)
        r_b = jnp.where(incl, _dot_nt(rtm, bt), 0.0)
        r_k = jnp.where(incl, _dot_nt(rtm, kt), 0.0)
        zp = st_ref[p]
        sz = _dot(jnp.concatenate([at, rt], axis=0), zp)
        az, rz = sz[:c], sz[c:]
        x = jnp.concatenate([az, az], axis=0) + _dot(a_ak, v)
        inv = eye - jnp.where(blocks[0], n1, 0.0)
        for lvl in range(1, len(blocks)):
            off = jnp.where(blocks[lvl] & ~blocks[lvl - 1], n1, 0.0)
            inv = inv - _dot(inv, _dot(off, inv))
        x = _dot(inv, x)
        u = jnp.where(lane0, -x[:c], -x[c:])
        ys = _dot(r_b, u) + _dot(r_k, v)
        y_ref[0, :, sl] = rz + jnp.where(lane0, ys[:c], ys[c:])
        totc = _dot_tn(hi[:, sl], ones) + _dot_tn(lo[:, sl], ones)
        upd = _dot_tn(jnp.concatenate([bh_all[:, sl], kh_all[:, sl]], axis=0),
                      jnp.concatenate([u.astype(BF16), v], axis=0))
        st_ref[p] = jnp.exp(totc) * zp + jnp.where(same_head, upd, 0.0)


def _rw_scan_kernel(*refs, nchunks):
    fwd_in, bwd_in = refs[0:6], refs[6:12]
    s0f_ref, s0b_ref, yf_ref, yb_ref, sf_ref, sb_ref, stf_ref, stb_ref = refs[12:]
    i = pl.program_id(1)

    @pl.when(i == 0)
    def _():
        stf_ref[...] = s0f_ref[0]
        stb_ref[...] = s0b_ref[0]

    lower, upper = _tri_masks(RW_CHUNK)
    _rw_scan_dir(fwd_in, stf_ref, yf_ref, lower, RW_CHUNK - 1, True)
    _rw_scan_dir(bwd_in, stb_ref, yb_ref, upper, 0, False)

    @pl.when(i == nchunks - 1)
    def _():
        sf_ref[0] = stf_ref[...]
        sb_ref[0] = stb_ref[...]


def _rw_scan(r, kk, v, dirs, s0f, s0b):
    b, s, d = r.shape
    c = RW_CHUNK
    nchunks = s // c
    fwd = pl.BlockSpec((1, c, d), lambda bi, i: (bi, i, 0))
    bwd = pl.BlockSpec((1, c, d), lambda bi, i: (bi, nchunks - 1 - i, 0))
    st_shape = (RW_PAIRS, LANES, LANES)
    st_spec = pl.BlockSpec((1,) + st_shape, lambda bi, i: (bi, 0, 0, 0))
    (lw0, k0, b0), (lw1, k1, b1) = dirs
    return pl.pallas_call(
        functools.partial(_rw_scan_kernel, nchunks=nchunks),
        grid=(b, nchunks),
        in_specs=[fwd] * 6 + [bwd] * 6 + [st_spec, st_spec],
        out_specs=[fwd, bwd, st_spec, st_spec],
        out_shape=[jax.ShapeDtypeStruct((b, s, d), F32)] * 2 + [jax.ShapeDtypeStruct((b,) + st_shape, F32)] * 2,
        scratch_shapes=[pltpu.VMEM(st_shape, F32), pltpu.VMEM(st_shape, F32)],
        compiler_params=_cparams("parallel", "arbitrary"),
        name="rw_scan",
    )(r, kk, v, lw0, k0, b0, r, kk, v, lw1, k1, b1, s0f, s0b)


def _rw_finish_prologue(yf_ref, yb_ref, bonus_ref, g_ref, gs_ref, gt_ref, gng_ref, gnb_ref):
    y = yf_ref[0] + yb_ref[0]
    gs = gs_ref[...]
    gt = gt_ref[...]
    inv_n = 1.0 / RW_HEAD
    mean = _dot_sel(_dot_sel(y, gs) * inv_n, gt)
    yc = y - mean
    var = _dot_sel(yc * yc, gs) * inv_n
    rstd = _dot_sel(lax.rsqrt(var + RW_GN_EPS), gt)
    yn = yc * rstd * gng_ref[...] + gnb_ref[...]
    return ((yn + bonus_ref[0]) * g_ref[0]).astype(BF16)


def _mixer_rwkv(hx, hc, mx, mc, mu, w_rkv, w0, w1, w2, a0, a1, a2, g1, g2, k_k, k_a, r_k, gn_g, gn_b, w_o,
                ln_g, ln_b, ctx_out):
    b, n, d = hx.shape
    gs, gt = _head_sum_matrices()
    rank_w, rank_a = w1.shape[-1], a1.shape[-1]
    assert 2 * rank_w == LANES and 2 * rank_a == LANES

    def ext(m2):
        z = jnp.zeros((2, LANES, d), F32)
        z = z.at[0, :m2.shape[1]].set(m2[0])
        return z.at[1, m2.shape[1]:].set(m2[1]).astype(BF16)

    consts = [mu, w_rkv[0].astype(BF16), w_rkv[1].astype(BF16), w_rkv[2].astype(BF16),
              g1.astype(BF16), g2.astype(BF16),
              jnp.concatenate([w1[0], w1[1]], axis=1).astype(BF16), ext(w2),
              jnp.concatenate([a1[0], a1[1]], axis=1).astype(BF16), ext(a2),
              w0, a0, k_k.reshape(1, d), k_a.reshape(1, d), r_k.reshape(1, d), gs, gt]
    w_o = w_o.astype(BF16)

    def prep(h, mod):
        r, kk, v, g, bonus, lw0, k0, b0, lw1, k1, b1 = _rw_prep(h, mod, consts)
        return r, kk, v, g, bonus, ((lw0, k0, b0), (lw1, k1, b1))

    rc, kkc, vc, gc, bonc, dc = prep(hc, mc)
    rx, kkx, vx, gx, bonx, dx = prep(hx, mx)
    s0 = jnp.zeros((b, RW_PAIRS, LANES, LANES), F32)
    ycf, ycb, scf, scb = _rw_scan(rc, kkc, vc, dc, s0, s0)
    yxf, yxb, _, _ = _rw_scan(rx, kkx, vx, dx, scf, scb)
    fin_consts = [gs, gt, gn_g.reshape(1, d), gn_b.reshape(1, d)]
    hx = _out_proj([yxf, yxb, bonx, gx], fin_consts, _rw_finish_prologue, w_o, None, hx, mx, ln_g, ln_b, "rw_out")
    if ctx_out:
        hc = _out_proj([ycf, ycb, bonc, gc], fin_consts, _rw_finish_prologue, w_o, None, hc, mc, ln_g, ln_b,
                       "rw_out_ctx")
    return hx, hc


def kernel(x, c, ctx, c_ctx, ada_w, ada_b, ln_g, ln_b, ffn_w13, ffn_w2, na_wqkv, na_wo, na_rpb, cv_w1, cv_b1, cv_wdw, cv_bdw, cv_ln_g, cv_ln_b, cv_w2, cv_b2, gla_win, gla_wa1, gla_wa2, gla_ba, gla_norm_g, gla_wo, rw_mu, rw_wrkv, rw_w0, rw_w1, rw_w2, rw_a0, rw_a1, rw_a2, rw_g1, rw_g2, rw_kk, rw_ka, rw_rk, rw_gn_g, rw_gn_b, rw_wo):
    b, n, d = x.shape
    depth = ada_w.shape[0]
    assert d == D_MODEL and depth == DEPTH and n % GRID_W == 0
    rows = -(-(b + 1) // 8) * 8
    cond = jnp.zeros((rows, d), F32).at[:b].set(c).at[b].set(c_ctx)
    mod_all = _ada(cond, ada_w, ada_b)
    w13 = ffn_w13.astype(BF16)
    w2 = ffn_w2.astype(BF16)
    hx, hc = x, ctx
    for i in range(depth):
        last = i == depth - 1
        mx = mod_all[i, :b].reshape(b, 3 * N_SUB, d)
        mc = jnp.broadcast_to(mod_all[i, b].reshape(1, 3 * N_SUB, d), (b, 3 * N_SUB, d))
        hx = _ffn(hx, mx, w13[i, 0], w2[i, 0], ln_g[i, 0], ln_b[i, 0], 0)
        hc = _ffn(hc, mc, w13[i, 0], w2[i, 0], ln_g[i, 0], ln_b[i, 0], 0)
        kind, j = i % 4, i // 4
        lg, lb = ln_g[i, 1], ln_b[i, 1]
        if kind == 0:
            hx, hc = _mixer_na(hx, hc, mx, mc, na_wqkv[j], na_wo[j], na_rpb[j], lg, lb, not last)
        elif kind == 1:
            hx, hc = _mixer_conv(hx, hc, mx, mc, cv_w1[j], cv_b1[j], cv_wdw[j], cv_bdw[j], cv_ln_g[j], cv_ln_b[j],
                                 cv_w2[j], cv_b2[j], lg, lb, not last)
        elif kind == 2:
            hx, hc = _mixer_gla(hx, hc, mx, mc, gla_win[j], gla_wa1[j], gla_wa2[j], gla_ba[j], gla_norm_g[j],
                                gla_wo[j], lg, lb, not last)
        else:
            hx, hc = _mixer_rwkv(hx, hc, mx, mc, rw_mu[j], rw_wrkv[j], rw_w0[j], rw_w1[j], rw_w2[j], rw_a0[j],
                                 rw_a1[j], rw_a2[j], rw_g1[j], rw_g2[j], rw_kk[j], rw_ka[j], rw_rk[j],
                                 rw_gn_g[j], rw_gn_b[j], rw_wo[j], lg, lb, not last)
        hx = _ffn(hx, mx, w13[i, 1], w2[i, 1], ln_g[i, 2], ln_b[i, 2], 2)
        if not last:
            hc = _ffn(hc, mc, w13[i, 1], w2[i, 1], ln_g[i, 2], ln_b[i, 2], 2)
    return hx
```

```python
import functools

import numpy as np
import jax
import jax.numpy as jnp
from jax import lax
from jax.experimental import pallas as pl
from jax.experimental.pallas import tpu as pltpu

F32 = jnp.float32
BF16 = jnp.bfloat16

D_MODEL = 1024
DEPTH = 4
N_SUB = 3
GRID_W = 64
ALPHA = (2.0 * DEPTH) ** 0.25
LN_EPS = 1e-5
D_FF = 2816
NA_HEADS = 16
NA_HEAD_DIM = D_MODEL // NA_HEADS
NA_KH = 8
NA_KW = 16
CONV_WIDTH = 31
CONV_HALO = 16
GLA_HEADS = 4
GLA_DK = 128
GLA_DV = 256
GLA_GATE_RANK = 16
GLA_NORMALIZER = 16.0
GLA_CHUNK = 64
ROPE_BASE = 10000.0
RW_HEAD = 64
RW_HEADS = D_MODEL // RW_HEAD
RW_PAIRS = RW_HEADS // 2
RW_CHUNK = 64
RW_GN_EPS = 64e-5
LANES = 128
SUBLANES = 8
NEG_BIG = -1e30
VMEM_LIMIT = 56 * 1024 * 1024


def _cparams(*sem):
    return pltpu.CompilerParams(dimension_semantics=sem, vmem_limit_bytes=VMEM_LIMIT)


def _dot(a, b):
    return jnp.dot(a.astype(BF16), b.astype(BF16), preferred_element_type=F32)


def _dot_nt(a, b):
    return lax.dot_general(a.astype(BF16), b.astype(BF16), (((1,), (1,)), ((), ())),
                           preferred_element_type=F32)


def _dot_tn(a, b):
    return lax.dot_general(a.astype(BF16), b.astype(BF16), (((0,), (0,)), ((), ())),
                           preferred_element_type=F32)


def _split(x):
    hi = x.astype(BF16)
    lo = (x - hi.astype(F32)).astype(BF16)
    return hi, lo


def _dot_sel(x, sel):
    hi, lo = _split(x)
    return (jnp.dot(hi, sel, preferred_element_type=F32)
            + jnp.dot(lo, sel, preferred_element_type=F32))


def _sel_dot(sel, x):
    hi, lo = _split(x)
    return (jnp.dot(sel, hi, preferred_element_type=F32)
            + jnp.dot(sel, lo, preferred_element_type=F32))


def _sigmoid(x):
    return jax.nn.sigmoid(x)


def _softplus(x):
    return jnp.maximum(x, 0.0) + jnp.log(1.0 + jnp.exp(-jnp.abs(x)))


def _layer_norm(x, g, b, eps=LN_EPS):
    mu = jnp.mean(x, axis=-1, keepdims=True)
    xc = x - mu
    var = jnp.mean(xc * xc, axis=-1, keepdims=True)
    return xc * lax.rsqrt(var + eps) * g + b


def _mod_rows(mod_ref, j):
    return (mod_ref[0, 3 * j:3 * j + 1, :], mod_ref[0, 3 * j + 1:3 * j + 2, :],
            mod_ref[0, 3 * j + 2:3 * j + 3, :])


def _modulate(h, mod_ref, j):
    shift, scale, _ = _mod_rows(mod_ref, j)
    return h * (1.0 + scale) + shift


def _post_norm(h, y, mod_ref, j, g_ref, b_ref):
    gate = mod_ref[0, 3 * j + 2:3 * j + 3, :]
    return _layer_norm(ALPHA * h + gate * y, g_ref[...], b_ref[...])


def _tok_spec(tm, d):
    return pl.BlockSpec((1, tm, d), lambda b, t: (b, t, 0))


def _mod_spec():
    return pl.BlockSpec((1, 3 * N_SUB, D_MODEL), lambda b, t: (b, 0, 0))


def _const_spec(shape):
    nd = len(shape)
    return pl.BlockSpec(shape, lambda b, t: (0,) * nd, pipeline_mode=pl.Buffered(1))


def _tile(s, pref):
    tm = min(pref, s)
    assert s % tm == 0
    return tm


def _ada_kernel(cond_ref, w_ref, b_ref, o_ref):
    c = cond_ref[...]
    o_ref[0] = _dot(c * _sigmoid(c), w_ref[0]) + b_ref[0]


def _ada(cond, ada_w, ada_b):
    r, d = cond.shape
    depth, _, n = ada_w.shape
    tn = 1024
    return pl.pallas_call(
        _ada_kernel,
        grid=(depth, n // tn),
        in_specs=[pl.BlockSpec((r, d), lambda l, j: (0, 0)),
                  pl.BlockSpec((1, d, tn), lambda l, j: (l, 0, j)),
                  pl.BlockSpec((1, 1, tn), lambda l, j: (l, 0, j))],
        out_specs=pl.BlockSpec((1, r, tn), lambda l, j: (l, 0, j)),
        out_shape=jax.ShapeDtypeStruct((depth, r, n), F32),
        compiler_params=_cparams("parallel", "parallel"),
        name="ada_mod",
    )(cond, ada_w, ada_b.reshape(depth, 1, n))


FFN_CHUNK = 256


def _ffn_kernel(h_ref, mod_ref, w13_ref, w2_ref, g_ref, b_ref, o_ref, *, j):
    h = h_ref[0]
    u = _modulate(h, mod_ref, j).astype(BF16)
    acc = jnp.zeros(h.shape, F32)
    for c in range(D_FF // FFN_CHUNK):
        lo = c * FFN_CHUNK
        a = jnp.dot(u, w13_ref[:, lo:lo + FFN_CHUNK], preferred_element_type=F32)
        v = jnp.dot(u, w13_ref[:, D_FF + lo:D_FF + lo + FFN_CHUNK], preferred_element_type=F32)
        z = (a * _sigmoid(a) * v).astype(BF16)
        acc = acc + jnp.dot(z, w2_ref[lo:lo + FFN_CHUNK, :], preferred_element_type=F32)
    o_ref[0] = _post_norm(h, 0.5 * acc, mod_ref, j, g_ref, b_ref)


def _ffn(h, mod, w13, w2, ln_g, ln_b, j):
    b, s, d = h.shape
    tm = _tile(s, 512)
    return pl.pallas_call(
        functools.partial(_ffn_kernel, j=j),
        grid=(b, s // tm),
        in_specs=[_tok_spec(tm, d), _mod_spec(), _const_spec(w13.shape), _const_spec(w2.shape),
                  _const_spec((1, d)), _const_spec((1, d))],
        out_specs=_tok_spec(tm, d),
        out_shape=jax.ShapeDtypeStruct(h.shape, F32),
        compiler_params=_cparams("parallel", "parallel"),
        name="ffn_half",
    )(h, mod, w13, w2, ln_g.reshape(1, d), ln_b.reshape(1, d))


def _out_kernel(*refs, n_in, prologue, has_bias):
    ins = refs[:n_in]
    w_ref = refs[n_in]
    k = n_in + 1
    bias_ref = None
    if has_bias:
        bias_ref = refs[k]
        k += 1
    h_ref, mod_ref, g_ref, b_ref, o_ref = refs[k:k + 5]
    y = jnp.dot(prologue(*ins), w_ref[...], preferred_element_type=F32)
    if has_bias:
        y = y + bias_ref[...]
    o_ref[0] = _post_norm(h_ref[0], y, mod_ref, 1, g_ref, b_ref)


def _out_proj(tok_inputs, const_inputs, prologue, w, bias, h, mod, ln_g, ln_b, name):
    b, s, d = h.shape
    tm = _tile(s, 256)
    n_in = len(tok_inputs) + len(const_inputs)
    in_specs = [_tok_spec(tm, a.shape[-1]) for a in tok_inputs]
    in_specs += [_const_spec(a.shape) for a in const_inputs]
    in_specs.append(_const_spec(w.shape))
    args = list(tok_inputs) + list(const_inputs) + [w]
    if bias is not None:
        in_specs.append(_const_spec((1, d)))
        args.append(bias.reshape(1, d))
    in_specs += [_tok_spec(tm, d), _mod_spec(), _const_spec((1, d)), _const_spec((1, d))]
    args += [h, mod, ln_g.reshape(1, d), ln_b.reshape(1, d)]
    return pl.pallas_call(
        functools.partial(_out_kernel, n_in=n_in, prologue=prologue, has_bias=bias is not None),
        grid=(b, s // tm),
        in_specs=in_specs,
        out_specs=_tok_spec(tm, d),
        out_shape=jax.ShapeDtypeStruct(h.shape, F32),
        compiler_params=_cparams("parallel", "parallel"),
        name=name,
    )(*args)


def _plain_prologue(y_ref):
    return y_ref[0].astype(BF16)


def _na_qkv_kernel(h_ref, mod_ref, w_ref, q_ref, k_ref, v_ref):
    d = D_MODEL
    u = _modulate(h_ref[0], mod_ref, 1).astype(BF16)
    z = jnp.dot(u, w_ref[...], preferred_element_type=F32)
    q_ref[0] = (z[:, :d] * (NA_HEAD_DIM ** -0.5)).astype(BF16)
    k_ref[0] = z[:, d:2 * d].astype(BF16)
    v_ref[0] = z[:, 2 * d:].astype(BF16)


def _na_qkv(h, mod, w):
    b, s, d = h.shape
    tm = _tile(s, 256)
    out = jax.ShapeDtypeStruct(h.shape, BF16)
    return pl.pallas_call(
        _na_qkv_kernel,
        grid=(b, s // tm),
        in_specs=[_tok_spec(tm, d), _mod_spec(), _const_spec(w.shape)],
        out_specs=[_tok_spec(tm, d)] * 3,
        out_shape=[out] * 3,
        compiler_params=_cparams("parallel", "parallel"),
        name="na_qkv",
    )(h, mod, w)


def _na_row_start(r, rows, kh):
    return jnp.clip(r - kh // 2, 0, rows - kh)


def _na_attn_kernel(q_ref, k_ref, v_ref, kc_ref, vc_ref, bias_ref, o_ref, *, rows, kh):
    r = pl.program_id(1)
    start = pl.multiple_of(_na_row_start(r, rows, kh) * GRID_W, GRID_W)
    win = kh * GRID_W
    heads = range(NA_HEADS)
    sls = [slice(hd * NA_HEAD_DIM, (hd + 1) * NA_HEAD_DIM) for hd in heads]
    q = [q_ref[0, :, sl] for sl in sls]
    s_loc = [_dot_nt(q[hd], k_ref[0, pl.ds(start, win), sls[hd]]) + bias_ref[0, hd] for hd in heads]
    s_ctx = [_dot_nt(q[hd], kc_ref[0, :, sls[hd]]) for hd in heads]
    m = [jnp.maximum(jnp.max(s_loc[hd], axis=-1, keepdims=True), jnp.max(s_ctx[hd], axis=-1, keepdims=True))
         for hd in heads]
    p_loc = [jnp.exp(s_loc[hd] - m[hd]) for hd in heads]
    p_ctx = [jnp.exp(s_ctx[hd] - m[hd]) for hd in heads]
    den = [jnp.sum(p_loc[hd], axis=-1, keepdims=True) + jnp.sum(p_ctx[hd], axis=-1, keepdims=True) for hd in heads]
    o = [_dot(p_loc[hd], v_ref[0, pl.ds(start, win), sls[hd]]) + _dot(p_ctx[hd], vc_ref[0, :, sls[hd]])
         for hd in heads]
    for hd in heads:
        o_ref[0, :, sls[hd]] = (o[hd] / den[hd]).astype(BF16)


def _na_bias_table(rpb, kh):
    qc = np.arange(GRID_W)[:, None]
    kc = np.arange(GRID_W)[None, :]
    cstart = np.clip(qc - NA_KW // 2, 0, GRID_W - NA_KW)
    ok = (kc >= cstart) & (kc < cstart + NA_KW)
    dc = np.clip(kc - qc + NA_KW - 1, 0, 2 * NA_KW - 2)
    tbl = jnp.where(ok[None, None], rpb[:, :, dc], NEG_BIG)
    n_s = 2 * NA_KH - kh
    out = []
    for s in range(n_s):
        t = tbl[:, s:s + kh]
        out.append(jnp.transpose(t, (0, 2, 1, 3)).reshape(NA_HEADS, GRID_W, kh * GRID_W))
    return jnp.stack(out, axis=0)


def _na_attn(q, k, v, kc, vc, bias_tab):
    b, n, d = q.shape
    l = kc.shape[1]
    rows = n // GRID_W
    kh = min(NA_KH, rows)

    def bias_idx(bi, r):
        return (_na_row_start(r, rows, kh) - r + NA_KH - 1, 0, 0, 0)

    return pl.pallas_call(
        functools.partial(_na_attn_kernel, rows=rows, kh=kh),
        grid=(b, rows),
        in_specs=[pl.BlockSpec((1, GRID_W, d), lambda bi, r: (bi, r, 0)),
                  pl.BlockSpec((1, n, d), lambda bi, r: (bi, 0, 0)),
                  pl.BlockSpec((1, n, d), lambda bi, r: (bi, 0, 0)),
                  pl.BlockSpec((1, l, d), lambda bi, r: (bi, 0, 0)),
                  pl.BlockSpec((1, l, d), lambda bi, r: (bi, 0, 0)),
                  pl.BlockSpec((1, NA_HEADS, GRID_W, kh * GRID_W), bias_idx)],
        out_specs=pl.BlockSpec((1, GRID_W, d), lambda bi, r: (bi, r, 0)),
        out_shape=jax.ShapeDtypeStruct(q.shape, BF16),
        compiler_params=_cparams("parallel", "arbitrary"),
        name="na_attn",
    )(q, k, v, kc, vc, bias_tab)


def _ctx_attn_kernel(q_ref, k_ref, v_ref, o_ref):
    for hd in range(NA_HEADS):
        sl = slice(hd * NA_HEAD_DIM, (hd + 1) * NA_HEAD_DIM)
        s = _dot_nt(q_ref[0, :, sl], k_ref[0, :, sl])
        p = jnp.exp(s - jnp.max(s, axis=-1, keepdims=True))
        den = jnp.sum(p, axis=-1, keepdims=True)
        o_ref[0, :, sl] = (_dot(p, v_ref[0, :, sl]) / den).astype(BF16)


def _ctx_attn(q, k, v):
    b, l, d = q.shape
    spec = pl.BlockSpec((1, l, d), lambda bi: (bi, 0, 0))
    return pl.pallas_call(
        _ctx_attn_kernel,
        grid=(b,),
        in_specs=[spec] * 3,
        out_specs=spec,
        out_shape=jax.ShapeDtypeStruct(q.shape, BF16),
        compiler_params=_cparams("parallel"),
        name="ctx_attn",
    )(q, k, v)


def _mixer_na(hx, hc, mx, mc, w_qkv, w_o, rpb, ln_g, ln_b, ctx_out):
    w_qkv = w_qkv.astype(BF16)
    w_o = w_o.astype(BF16)
    qx, kx, vx = _na_qkv(hx, mx, w_qkv)
    qc, kc, vc = _na_qkv(hc, mc, w_qkv)
    kh = min(NA_KH, hx.shape[1] // GRID_W)
    ox = _na_attn(qx, kx, vx, kc, vc, _na_bias_table(rpb, kh))
    hx = _out_proj([ox], [], _plain_prologue, w_o, None, hx, mx, ln_g, ln_b, "na_out")
    if ctx_out:
        oc = _ctx_attn(qc, kc, vc)
        hc = _out_proj([oc], [], _plain_prologue, w_o, None, hc, mc, ln_g, ln_b, "na_out_ctx")
    return hx, hc


def _conv_in_kernel(h_ref, mod_ref, w_ref, b_ref, o_ref):
    d = D_MODEL
    u = _modulate(h_ref[0], mod_ref, 1).astype(BF16)
    z = jnp.dot(u, w_ref[...], preferred_element_type=F32) + b_ref[...]
    o_ref[0] = z[:, :d] * _sigmoid(z[:, d:])


def _conv_in(h, mod, w1, b1):
    b, s, d = h.shape
    tm = _tile(s, 256)
    return pl.pallas_call(
        _conv_in_kernel,
        grid=(b, s // tm),
        in_specs=[_tok_spec(tm, d), _mod_spec(), _const_spec(w1.shape), _const_spec((1, 2 * d))],
        out_specs=_tok_spec(tm, d),
        out_shape=jax.ShapeDtypeStruct(h.shape, F32),
        compiler_params=_cparams("parallel", "parallel"),
        name="conv_in",
    )(h, mod, w1, b1.reshape(1, 2 * d))


CONV_ROWS = 16


def _conv_dw_kernel(prev_ref, cur_ref, next_ref, w_ref, bdw_ref, g_ref, b_ref, o_ref, win_ref, *, ts, nt):
    t = pl.program_id(1)
    halo = CONV_HALO
    d = cur_ref.shape[-1]
    rows = ts + 2 * halo
    win_ref[0, halo:halo + ts, :] = cur_ref[0]
    win_ref[0, 0:halo, :] = jnp.where(t > 0, prev_ref[0], 0.0)
    win_ref[0, halo + ts:rows, :] = jnp.where(t < nt - 1, next_ref[0], 0.0)
    sub = lax.broadcasted_iota(jnp.int32, (rows // SUBLANES, SUBLANES, LANES), 1)
    for cc in range(d // LANES):
        cols = slice(cc * LANES, (cc + 1) * LANES)
        w = win_ref[0, :, cols].reshape(rows // SUBLANES, SUBLANES, LANES)
        for i in range(1, SUBLANES):
            rot = pltpu.roll(w, SUBLANES - i, axis=1)
            nxt = jnp.concatenate([rot[1:], rot[:1]], axis=0)
            win_ref[i, :, cols] = jnp.where(sub < SUBLANES - i, rot, nxt).reshape(rows, LANES)
    off = halo - CONV_WIDTH // 2
    slabs = CONV_ROWS // SUBLANES
    for g in range(ts // CONV_ROWS):
        acc = jnp.broadcast_to(bdw_ref[...], (slabs, SUBLANES, d))
        for k in range(CONV_WIDTH):
            i, base = (off + k) % SUBLANES, g * CONV_ROWS + (off + k) // SUBLANES * SUBLANES
            acc = acc + win_ref[i, base:base + CONV_ROWS, :].reshape(slabs, SUBLANES, d) * w_ref[k][None]
        y = _layer_norm(acc.reshape(CONV_ROWS, d), g_ref[...], b_ref[...])
        o_ref[0, g * CONV_ROWS:(g + 1) * CONV_ROWS, :] = (y * _sigmoid(y)).astype(BF16)


def _conv_dw(u, w_dw, b_dw, ln_g, ln_b):
    b, s, d = u.shape
    ts = _tile(s, 128)
    nt = s // ts
    hb = ts // CONV_HALO
    nhb = s // CONV_HALO
    w_rep = jnp.broadcast_to(w_dw[:, None, :], (CONV_WIDTH, SUBLANES, d))
    return pl.pallas_call(
        functools.partial(_conv_dw_kernel, ts=ts, nt=nt),
        grid=(b, nt),
        in_specs=[pl.BlockSpec((1, CONV_HALO, d), lambda bi, t: (bi, jnp.maximum(t * hb - 1, 0), 0)),
                  _tok_spec(ts, d),
                  pl.BlockSpec((1, CONV_HALO, d), lambda bi, t: (bi, jnp.minimum((t + 1) * hb, nhb - 1), 0)),
                  _const_spec(w_rep.shape), _const_spec((1, d)), _const_spec((1, d)), _const_spec((1, d))],
        out_specs=_tok_spec(ts, d),
        out_shape=jax.ShapeDtypeStruct(u.shape, BF16),
        scratch_shapes=[pltpu.VMEM((SUBLANES, ts + 2 * CONV_HALO, d), F32)],
        compiler_params=_cparams("parallel", "arbitrary"),
        name="conv_dw",
    )(u, u, u, w_rep, b_dw.reshape(1, d), ln_g.reshape(1, d), ln_b.reshape(1, d))


def _mixer_conv(hx, hc, mx, mc, w1, b1, w_dw, b_dw, cln_g, cln_b, w2, b2, ln_g, ln_b, ctx_out):
    w1 = w1.astype(BF16)
    w2 = w2.astype(BF16)

    def branch(h, mod, name):
        u = _conv_in(h, mod, w1, b1)
        y = _conv_dw(u, w_dw, b_dw, cln_g, cln_b)
        return _out_proj([y], [], _plain_prologue, w2, b2, h, mod, ln_g, ln_b, name)

    hx = branch(hx, mx, "conv_out")
    if ctx_out:
        hc = branch(hc, mc, "conv_out_ctx")
    return hx, hc


GLA_NQK = GLA_HEADS * GLA_DK


def _gla_in_kernel(*refs, rotary):
    if rotary:
        h_ref, mod_ref, w_ref, wa2_ref, ba_ref, cos_ref, sin_ref = refs[:7]
        outs = refs[7:]
    else:
        h_ref, mod_ref, w_ref, wa2_ref, ba_ref = refs[:5]
        outs = refs[5:]
    q_ref, k_ref, v_ref, g_ref, laf_ref, lab_ref = outs
    n, d = GLA_NQK, D_MODEL
    u = _modulate(h_ref[0], mod_ref, 1).astype(BF16)
    z = jnp.dot(u, w_ref[...], preferred_element_type=F32)
    q = z[:, :n]
    k = z[:, n:2 * n]
    o = 2 * n
    if rotary:
        cos = cos_ref[...]
        sin = sin_ref[...]
        q = q * cos + z[:, o:o + n] * sin
        k = k * cos + z[:, o + n:o + 2 * n] * sin
        o += 2 * n
    q_ref[0] = q * (GLA_DK ** -0.5)
    k_ref[0] = k
    v_ref[0] = z[:, o:o + d].astype(BF16)
    g_ref[0] = z[:, o + d:o + 2 * d]
    t = z[:, o + 2 * d:o + 2 * d + LANES].astype(BF16)
    for dr, la_ref in enumerate((laf_ref, lab_ref)):
        x = jnp.dot(t, wa2_ref[dr], preferred_element_type=F32) + ba_ref[dr:dr + 1, :]
        la_ref[0] = -_softplus(-x) * (1.0 / GLA_NORMALIZER)


def _gla_rope_tables(n):
    quarter = GLA_DK // 4
    t = np.arange(n)
    freqs = ROPE_BASE ** (-jnp.arange(quarter, dtype=F32) / quarter)
    ang_r = jnp.asarray(t // GRID_W, F32)[:, None] * freqs
    ang_c = jnp.asarray(t % GRID_W, F32)[:, None] * freqs
    cos = jnp.concatenate([jnp.cos(ang_r), jnp.cos(ang_r), jnp.cos(ang_c), jnp.cos(ang_c)], axis=-1)
    sin = jnp.concatenate([-jnp.sin(ang_r), jnp.sin(ang_r), -jnp.sin(ang_c), jnp.sin(ang_c)], axis=-1)
    return jnp.tile(cos, (1, GLA_HEADS)), jnp.tile(sin, (1, GLA_HEADS))


def _gla_swap_perm():
    quarter = GLA_DK // 4
    base = np.concatenate([np.arange(quarter, 2 * quarter), np.arange(0, quarter),
                           np.arange(3 * quarter, 4 * quarter), np.arange(2 * quarter, 3 * quarter)])
    return np.concatenate([h * GLA_DK + base for h in range(GLA_HEADS)])


def _gla_in(h, mod, w_ext, wa2_ext, ba, rope):
    b, s, d = h.shape
    tm = _tile(s, 256)
    rotary = rope is not None
    in_specs = [_tok_spec(tm, d), _mod_spec(), _const_spec(w_ext.shape), _const_spec(wa2_ext.shape),
                _const_spec(ba.shape)]
    args = [h, mod, w_ext, wa2_ext, ba]
    if rotary:
        in_specs += [pl.BlockSpec((tm, GLA_NQK), lambda bi, t: (t, 0))] * 2
        args += list(rope)
    shp = lambda w, dt: jax.ShapeDtypeStruct((b, s, w), dt)
    return pl.pallas_call(
        functools.partial(_gla_in_kernel, rotary=rotary),
        grid=(b, s // tm),
        in_specs=in_specs,
        out_specs=[_tok_spec(tm, GLA_NQK), _tok_spec(tm, GLA_NQK), _tok_spec(tm, d), _tok_spec(tm, d),
                   _tok_spec(tm, GLA_NQK), _tok_spec(tm, GLA_NQK)],
        out_shape=[shp(GLA_NQK, F32), shp(GLA_NQK, F32), shp(d, BF16), shp(d, F32),
                   shp(GLA_NQK, F32), shp(GLA_NQK, F32)],
        compiler_params=_cparams("parallel", "parallel"),
        name="gla_in_rope" if rotary else "gla_in",
    )(*args)


def _tri_masks(c):
    row = lax.broadcasted_iota(jnp.int32, (c, c), 0)
    col = lax.broadcasted_iota(jnp.int32, (c, c), 1)
    return row >= col, row <= col


def _gla_core_kernel(qf_ref, kf_ref, vf_ref, laf_ref, qb_ref, kb_ref, vb_ref, lab_ref, s0f_ref, s0b_ref,
                     of_ref, ob_ref, sf_ref, sb_ref, stf_ref, stb_ref, *, nchunks):
    i = pl.program_id(1)
    c = GLA_CHUNK

    @pl.when(i == 0)
    def _():
        stf_ref[...] = s0f_ref[0]
        stb_ref[...] = s0b_ref[0]

    lower, upper = _tri_masks(c)
    ones = jnp.ones((c, GLA_DK), BF16)
    dirs = ((qf_ref, kf_ref, vf_ref, laf_ref, of_ref, stf_ref, lower, c - 1),
            (qb_ref, kb_ref, vb_ref, lab_ref, ob_ref, stb_ref, upper, 0))
    insts = []
    for q_ref, k_ref, v_ref, la_ref, o_ref, st_ref, mask, last in dirs:
        tri = mask.astype(BF16)
        hi, lo = _split(la_ref[0])
        bcum = jnp.dot(tri, hi, preferred_element_type=F32) + jnp.dot(tri, lo, preferred_element_type=F32)
        blast = bcum[last:last + 1, :]
        qd = (q_ref[0] * jnp.exp(bcum)).astype(BF16)
        k = k_ref[0]
        kd = (k * jnp.exp(-bcum)).astype(BF16)
        kt = (k * jnp.exp(blast - bcum)).astype(BF16)
        for hd in range(GLA_HEADS):
            ks = slice(hd * GLA_DK, (hd + 1) * GLA_DK)
            vs = slice(hd * GLA_DV, (hd + 1) * GLA_DV)
            insts.append(dict(qd=qd[:, ks], kd=kd[:, ks], kt=kt[:, ks], hi=hi[:, ks], lo=lo[:, ks],
                              v=v_ref[0, :, vs], mask=mask, o_ref=o_ref, st_ref=st_ref, hd=hd, vs=vs))
    att = [jnp.where(t["mask"], _dot_nt(t["qd"], t["kd"]), 0.0) for t in insts]
    s_in = [t["st_ref"][t["hd"]] for t in insts]
    out = [_dot(a, t["v"]) + _dot(t["qd"], s) for a, t, s in zip(att, insts, s_in)]
    for t, o in zip(insts, out):
        t["o_ref"][0, :, t["vs"]] = o
    tot = [_dot_tn(t["hi"], ones) + _dot_tn(t["lo"], ones) for t in insts]
    upd = [_dot_tn(t["kt"], t["v"]) for t in insts]
    for t, tt, s, u in zip(insts, tot, s_in, upd):
        dec = jnp.exp(tt)
        t["st_ref"][t["hd"]] = jnp.concatenate([dec, dec], axis=1) * s + u

    @pl.when(i == nchunks - 1)
    def _():
        sf_ref[0] = stf_ref[...]
        sb_ref[0] = stb_ref[...]


def _gla_core(q, k, v, la_f, la_b, s0f, s0b):
    b, s, _ = q.shape
    c = GLA_CHUNK
    nchunks = s // c
    fwd = lambda w: pl.BlockSpec((1, c, w), lambda bi, i: (bi, i, 0))
    bwd = lambda w: pl.BlockSpec((1, c, w), lambda bi, i: (bi, nchunks - 1 - i, 0))
    st_shape = (GLA_HEADS, GLA_DK, GLA_DV)
    st_spec = pl.BlockSpec((1,) + st_shape, lambda bi, i: (bi, 0, 0, 0))
    return pl.pallas_call(
        functools.partial(_gla_core_kernel, nchunks=nchunks),
        grid=(b, nchunks),
        in_specs=[fwd(GLA_NQK), fwd(GLA_NQK), fwd(D_MODEL), fwd(GLA_NQK),
                  bwd(GLA_NQK), bwd(GLA_NQK), bwd(D_MODEL), bwd(GLA_NQK), st_spec, st_spec],
        out_specs=[fwd(D_MODEL), bwd(D_MODEL), st_spec, st_spec],
        out_shape=[jax.ShapeDtypeStruct((b, s, D_MODEL), F32)] * 2
        + [jax.ShapeDtypeStruct((b,) + st_shape, F32)] * 2,
        scratch_shapes=[pltpu.VMEM(st_shape, F32), pltpu.VMEM(st_shape, F32)],
        compiler_params=_cparams("parallel", "arbitrary"),
        name="gla_core",
    )(q, k, v, la_f, q, k, v, la_b, s0f, s0b)


def _gla_finish_prologue(of_ref, ob_ref, g_ref, ng_ref):
    o = of_ref[0] + ob_ref[0]
    g = g_ref[0]
    ng = ng_ref[...]
    parts = []
    for hd in range(GLA_HEADS):
        vs = slice(hd * GLA_DV, (hd + 1) * GLA_DV)
        oh = o[:, vs]
        gh = g[:, vs]
        oh = oh * lax.rsqrt(jnp.mean(oh * oh, axis=-1, keepdims=True) + LN_EPS) * ng
        parts.append((oh * (gh * _sigmoid(gh))).astype(BF16))
    return jnp.concatenate(parts, axis=1)


def _mixer_gla(hx, hc, mx, mc, w_in, w_a1, w_a2, b_a, norm_g, w_o, ln_g, ln_b, ctx_out):
    b, n, d = hx.shape
    nqk = GLA_NQK
    pad = jnp.zeros((d, LANES - 2 * GLA_GATE_RANK), F32)
    tail = [w_in[:, 2 * nqk:], w_a1[0], w_a1[1], pad]
    w_plain = jnp.concatenate([w_in[:, :2 * nqk]] + tail, axis=1).astype(BF16)
    perm = _gla_swap_perm()
    w_rope = jnp.concatenate([w_in[:, :2 * nqk], w_in[:, :nqk][:, perm], w_in[:, nqk:2 * nqk][:, perm]] + tail,
                             axis=1).astype(BF16)
    wa2_ext = jnp.zeros((2, LANES, nqk), F32)
    wa2_ext = wa2_ext.at[0, :GLA_GATE_RANK].set(w_a2[0])
    wa2_ext = wa2_ext.at[1, GLA_GATE_RANK:2 * GLA_GATE_RANK].set(w_a2[1]).astype(BF16)
    w_o = w_o.astype(BF16)

    qc, kc, vc, gc, lcf, lcb = _gla_in(hc, mc, w_plain, wa2_ext, b_a, None)
    qx, kx, vx, gx, lxf, lxb = _gla_in(hx, mx, w_rope, wa2_ext, b_a, _gla_rope_tables(n))
    s0 = jnp.zeros((b, GLA_HEADS, GLA_DK, GLA_DV), F32)
    ocf, ocb, scf, scb = _gla_core(qc, kc, vc, lcf, lcb, s0, s0)
    oxf, oxb, _, _ = _gla_core(qx, kx, vx, lxf, lxb, scf, scb)
    ng = norm_g.reshape(1, GLA_DV)
    hx = _out_proj([oxf, oxb, gx], [ng], _gla_finish_prologue, w_o, None, hx, mx, ln_g, ln_b, "gla_out")
    if ctx_out:
        hc = _out_proj([ocf, ocb, gc], [ng], _gla_finish_prologue, w_o, None, hc, mc, ln_g, ln_b, "gla_out_ctx")
    return hx, hc


def _head_sum_matrices():
    g = np.zeros((D_MODEL, LANES), np.float32)
    g[np.arange(D_MODEL), np.arange(D_MODEL) // RW_HEAD] = 1.0
    return jnp.asarray(g, BF16), jnp.asarray(g.T.copy(), BF16)


def _rw_prep_kernel(hp_ref, h_ref, hn_ref, mod_ref, mu_ref, wr_ref, wk_ref, wv_ref, g1_ref, g2_ref,
                    w1_ref, w2_ref, a1_ref, a2_ref, w0_ref, a0_ref, kkw_ref, kaw_ref, rkw_ref, gs_ref, gt_ref,
                    r_ref, kk_ref, v_ref, g_ref, bonus_ref,
                    lw0_ref, k0_ref, b0_ref, lw1_ref, k1_ref, b1_ref, *, tm, nt):
    t = pl.program_id(1)
    u = _modulate(h_ref[0], mod_ref, 1)
    up = _modulate(hp_ref[0, 7:8, :], mod_ref, 1)
    un = _modulate(hn_ref[0, 0:1, :], mod_ref, 1)
    up = jnp.where(t > 0, up, 0.0)
    un = jnp.where(t < nt - 1, un, 0.0)
    rowid = lax.broadcasted_iota(jnp.int32, (tm, 1), 0)
    prev = jnp.where(rowid == 0, up, pltpu.roll(u, 1, axis=0))
    nxt = jnp.where(rowid == tm - 1, un, pltpu.roll(u, tm - 1, axis=0))
    xx = 0.5 * (prev + nxt) - u
    mix = lambda j: (u + xx * mu_ref[j:j + 1, :]).astype(BF16)
    gs = gs_ref[...]
    gt = gt_ref[...]

    r = jnp.dot(mix(0), wr_ref[...], preferred_element_type=F32)
    k = jnp.dot(mix(2), wk_ref[...], preferred_element_type=F32)
    v = jnp.dot(mix(3), wv_ref[...], preferred_element_type=F32)
    hw = jnp.tanh(jnp.dot(mix(1), w1_ref[...], preferred_element_type=F32)).astype(BF16)
    ha = jnp.dot(mix(4), a1_ref[...], preferred_element_type=F32).astype(BF16)
    gg = _sigmoid(jnp.dot(mix(5), g1_ref[...], preferred_element_type=F32)).astype(BF16)
    g_ref[0] = jnp.dot(gg, g2_ref[...], preferred_element_type=F32)

    kk = k * kkw_ref[...]
    nrm = jnp.maximum(jnp.sqrt(_dot_sel(kk * kk, gs)), 1e-12)
    kk = kk * _dot_sel(1.0 / nrm, gt)
    ksum = jnp.zeros_like(k)
    for dr, (lw_ref, kd_ref, bd_ref) in enumerate(((lw0_ref, k0_ref, b0_ref), (lw1_ref, k1_ref, b1_ref))):
        xw = w0_ref[dr:dr + 1, :] + jnp.dot(hw, w2_ref[dr], preferred_element_type=F32)
        w_log = -_softplus(-xw) - 0.5
        lw_ref[0] = -jnp.exp(w_log)
        a = _sigmoid(a0_ref[dr:dr + 1, :] + jnp.dot(ha, a2_ref[dr], preferred_element_type=F32))
        kd = k * (1.0 + (a - 1.0) * kaw_ref[...])
        kd_ref[0] = kd
        bd_ref[0] = kk * a
        ksum = ksum + kd
    r_ref[0] = r
    kk_ref[0] = kk
    v_ref[0] = v.astype(BF16)
    bonus_ref[0] = _dot_sel(_dot_sel(r * rkw_ref[...] * ksum, gs), gt) * v


def _rw_prep(h, mod, consts):
    b, s, d = h.shape
    tm = _tile(s, 128)
    nt = s // tm
    hb = tm // 8
    nhb = s // 8
    in_specs = [pl.BlockSpec((1, 8, d), lambda bi, t: (bi, jnp.maximum(t * hb - 1, 0), 0)),
                _tok_spec(tm, d),
                pl.BlockSpec((1, 8, d), lambda bi, t: (bi, jnp.minimum((t + 1) * hb, nhb - 1), 0)),
                _mod_spec()] + [_const_spec(a.shape) for a in consts]
    f32o = jax.ShapeDtypeStruct(h.shape, F32)
    out_shape = [f32o, f32o, jax.ShapeDtypeStruct(h.shape, BF16), f32o, f32o] + [f32o] * 6
    return pl.pallas_call(
        functools.partial(_rw_prep_kernel, tm=tm, nt=nt),
        grid=(b, nt),
        in_specs=in_specs,
        out_specs=[_tok_spec(tm, d)] * 11,
        out_shape=out_shape,
        compiler_params=_cparams("parallel", "arbitrary"),
        name="rw_prep",
    )(h, h, h, mod, *consts)


def _rw_masks():
    c = RW_CHUNK
    row = lax.broadcasted_iota(jnp.int32, (2 * c, 2 * c), 0)
    col = lax.broadcasted_iota(jnp.int32, (2 * c, 2 * c), 1)
    rm, cm = row % c, col % c
    row_n = lax.broadcasted_iota(jnp.int32, (2 * c, c), 0) % c
    col_n = lax.broadcasted_iota(jnp.int32, (2 * c, c), 1)
    same_head = (row < c) == (col < c)
    return rm, cm, row_n, col_n, same_head


def _rw_prelude(refs, forward):
    r_ref, kk_ref, v_ref, lw_ref, k_ref, b_ref = refs
    c = RW_CHUNK
    rm, cm, row_n, col_n, same_head = _rw_masks()
    lower, upper = _tri_masks(c)
    if forward:
        np_mask, strict, incl, tri, last = rm > cm, row_n > col_n, row_n >= col_n, lower, c - 1
    else:
        np_mask, strict, incl, tri, last = rm < cm, row_n < col_n, row_n <= col_n, upper, 0
    tri = tri.astype(BF16)
    lw = lw_ref[0]
    hi, lo = _split(lw)
    cum = jnp.dot(tri, hi, preferred_element_type=F32) + jnp.dot(tri, lo, preferred_element_type=F32)
    tot = cum[last:last + 1, :]
    e_neg = jnp.exp(-cum)
    e_tail = jnp.exp(tot - cum)
    b_raw = b_ref[0]
    k_raw = k_ref[0]
    return dict(
        hi=hi, lo=lo, np_mask=np_mask & same_head, strict=strict, incl=incl,
        a=(kk_ref[0] * jnp.exp(cum - lw)).astype(BF16), r=(r_ref[0] * jnp.exp(cum)).astype(BF16),
        bt=(b_raw * e_neg).astype(BF16), kt=(k_raw * e_neg).astype(BF16),
        bh=(b_raw * e_tail).astype(BF16), kh=(k_raw * e_tail).astype(BF16), v=v_ref[0])


def _rw_scan_chunk(pre_f, pre_b, stf_ref, stb_ref, yf_ref, yb_ref):
    c = RW_CHUNK
    row2 = lax.broadcasted_iota(jnp.int32, (2 * c, 2 * c), 0)
    col2 = lax.broadcasted_iota(jnp.int32, (2 * c, 2 * c), 1)
    eye = (row2 == col2).astype(F32)
    same_head = (row2 < c) == (col2 < c)
    blocks = [(row2 // s) == (col2 // s) for s in (2 ** e for e in range(1, int(np.log2(c)) + 1))]
    lane0 = lax.broadcasted_iota(jnp.int32, (1, LANES), 1) < RW_HEAD
    ones = jnp.ones((c, LANES), BF16)
    zero = jnp.zeros((c, LANES), BF16)

    def stack2(x):
        return jnp.concatenate([jnp.where(lane0, x, zero), jnp.where(lane0, zero, x)], axis=0)

    insts = [(pre, st_ref, y_ref, slice(p * LANES, (p + 1) * LANES), p)
             for pre, st_ref, y_ref in ((pre_f, stf_ref, yf_ref), (pre_b, stb_ref, yb_ref))
             for p in range(RW_PAIRS)]
    n = len(insts)
    at = [pre["a"][:, sl] for pre, _, _, sl, _ in insts]
    rt = [pre["r"][:, sl] for pre, _, _, sl, _ in insts]
    bt = [pre["bt"][:, sl] for pre, _, _, sl, _ in insts]
    kt = [pre["kt"][:, sl] for pre, _, _, sl, _ in insts]
    v = [pre["v"][:, sl] for pre, _, _, sl, _ in insts]
    atm = [stack2(x) for x in at]
    rtm = [stack2(x) for x in rt]
    btm = [stack2(x) for x in bt]
    n1 = [jnp.where(insts[i][0]["np_mask"], _dot_nt(atm[i], btm[i]), 0.0) for i in range(n)]
    a_ak = [jnp.where(insts[i][0]["strict"], _dot_nt(atm[i], kt[i]), 0.0) for i in range(n)]
    zp = [st_ref[p] for _, st_ref, _, _, p in insts]
    sz = [_dot(jnp.concatenate([at[i], rt[i]], axis=0), zp[i]) for i in range(n)]
    x = [jnp.concatenate([sz[i][:c], sz[i][:c]], axis=0) + _dot(a_ak[i], v[i]) for i in range(n)]
    inv = [eye - jnp.where(blocks[0], n1[i], 0.0) for i in range(n)]
    for lvl in range(1, len(blocks)):
        sel = blocks[lvl] & ~blocks[lvl - 1]
        tmp = [_dot(jnp.where(sel, n1[i], 0.0), inv[i]) for i in range(n)]
        inv = [inv[i] - _dot(inv[i], tmp[i]) for i in range(n)]
    x = [_dot(inv[i], x[i]) for i in range(n)]
    u = [jnp.where(lane0, -x[i][:c], -x[i][c:]) for i in range(n)]
    r_b = [jnp.where(insts[i][0]["incl"], _dot_nt(rtm[i], bt[i]), 0.0) for i in range(n)]
    r_k = [jnp.where(insts[i][0]["incl"], _dot_nt(rtm[i], kt[i]), 0.0) for i in range(n)]
    ys = [_dot(r_b[i], u[i]) + _dot(r_k[i], v[i]) for i in range(n)]
    for i, (_, _, y_ref, sl, _) in enumerate(insts):
        y_ref[0, :, sl] = sz[i][c:] + jnp.where(lane0, ys[i][:c], ys[i][c:])
    totc = [_dot_tn(pre["hi"][:, sl], ones) + _dot_tn(pre["lo"][:, sl], ones) for pre, _, _, sl, _ in insts]
    upd = [_dot_tn(jnp.concatenate([insts[i][0]["bh"][:, insts[i][3]], insts[i][0]["kh"][:, insts[i][3]]], axis=0),
                   jnp.concatenate([u[i].astype(BF16), v[i]], axis=0)) for i in range(n)]
    for i, (_, st_ref, _, _, p) in enumerate(insts):
        st_ref[p] = jnp.exp(totc[i]) * zp[i] + jnp.where(same_head, upd[i], 0.0)


def _rw_scan_kernel(*refs, nchunks):
    fwd_in, bwd_in = refs[0:6], refs[6:12]
    s0f_ref, s0b_ref, yf_ref, yb_ref, sf_ref, sb_ref, stf_ref, stb_ref = refs[12:]
    i = pl.program_id(1)

    @pl.when(i == 0)
    def _():
        stf_ref[...] = s0f_ref[0]
        stb_ref[...] = s0b_ref[0]

    _rw_scan_chunk(_rw_prelude(fwd_in, True), _rw_prelude(bwd_in, False), stf_ref, stb_ref, yf_ref, yb_ref)

    @pl.when(i == nchunks - 1)
    def _():
        sf_ref[0] = stf_ref[...]
        sb_ref[0] = stb_ref[...]


def _rw_scan(r, kk, v, dirs, s0f, s0b):
    b, s, d = r.shape
    c = RW_CHUNK
    nchunks = s // c
    fwd = pl.BlockSpec((1, c, d), lambda bi, i: (bi, i, 0))
    bwd = pl.BlockSpec((1, c, d), lambda bi, i: (bi, nchunks - 1 - i, 0))
    st_shape = (RW_PAIRS, LANES, LANES)
    st_spec = pl.BlockSpec((1,) + st_shape, lambda bi, i: (bi, 0, 0, 0))
    (lw0, k0, b0), (lw1, k1, b1) = dirs
    return pl.pallas_call(
        functools.partial(_rw_scan_kernel, nchunks=nchunks),
        grid=(b, nchunks),
        in_specs=[fwd] * 6 + [bwd] * 6 + [st_spec, st_spec],
        out_specs=[fwd, bwd, st_spec, st_spec],
        out_shape=[jax.ShapeDtypeStruct((b, s, d), F32)] * 2 + [jax.ShapeDtypeStruct((b,) + st_shape, F32)] * 2,
        scratch_shapes=[pltpu.VMEM(st_shape, F32), pltpu.VMEM(st_shape, F32)],
        compiler_params=_cparams("parallel", "arbitrary"),
        name="rw_scan",
    )(r, kk, v, lw0, k0, b0, r, kk, v, lw1, k1, b1, s0f, s0b)


def _rw_finish_prologue(yf_ref, yb_ref, bonus_ref, g_ref, gs_ref, gt_ref, gng_ref, gnb_ref):
    y = yf_ref[0] + yb_ref[0]
    gs = gs_ref[...]
    gt = gt_ref[...]
    inv_n = 1.0 / RW_HEAD
    mean = _dot_sel(_dot_sel(y, gs) * inv_n, gt)
    yc = y - mean
    var = _dot_sel(yc * yc, gs) * inv_n
    rstd = _dot_sel(lax.rsqrt(var + RW_GN_EPS), gt)
    yn = yc * rstd * gng_ref[...] + gnb_ref[...]
    return ((yn + bonus_ref[0]) * g_ref[0]).astype(BF16)


def _mixer_rwkv(hx, hc, mx, mc, mu, w_rkv, w0, w1, w2, a0, a1, a2, g1, g2, k_k, k_a, r_k, gn_g, gn_b, w_o,
                ln_g, ln_b, ctx_out):
    b, n, d = hx.shape
    gs, gt = _head_sum_matrices()
    rank_w, rank_a = w1.shape[-1], a1.shape[-1]
    assert 2 * rank_w == LANES and 2 * rank_a == LANES

    def ext(m2):
        z = jnp.zeros((2, LANES, d), F32)
        z = z.at[0, :m2.shape[1]].set(m2[0])
        return z.at[1, m2.shape[1]:].set(m2[1]).astype(BF16)

    consts = [mu, w_rkv[0].astype(BF16), w_rkv[1].astype(BF16), w_rkv[2].astype(BF16),
              g1.astype(BF16), g2.astype(BF16),
              jnp.concatenate([w1[0], w1[1]], axis=1).astype(BF16), ext(w2),
              jnp.concatenate([a1[0], a1[1]], axis=1).astype(BF16), ext(a2),
              w0, a0, k_k.reshape(1, d), k_a.reshape(1, d), r_k.reshape(1, d), gs, gt]
    w_o = w_o.astype(BF16)

    def prep(h, mod):
        r, kk, v, g, bonus, lw0, k0, b0, lw1, k1, b1 = _rw_prep(h, mod, consts)
        return r, kk, v, g, bonus, ((lw0, k0, b0), (lw1, k1, b1))

    rc, kkc, vc, gc, bonc, dc = prep(hc, mc)
    rx, kkx, vx, gx, bonx, dx = prep(hx, mx)
    s0 = jnp.zeros((b, RW_PAIRS, LANES, LANES), F32)
    ycf, ycb, scf, scb = _rw_scan(rc, kkc, vc, dc, s0, s0)
    yxf, yxb, _, _ = _rw_scan(rx, kkx, vx, dx, scf, scb)
    fin_consts = [gs, gt, gn_g.reshape(1, d), gn_b.reshape(1, d)]
    hx = _out_proj([yxf, yxb, bonx, gx], fin_consts, _rw_finish_prologue, w_o, None, hx, mx, ln_g, ln_b, "rw_out")
    if ctx_out:
        hc = _out_proj([ycf, ycb, bonc, gc], fin_consts, _rw_finish_prologue, w_o, None, hc, mc, ln_g, ln_b,
                       "rw_out_ctx")
    return hx, hc


def kernel(x, c, ctx, c_ctx, ada_w, ada_b, ln_g, ln_b, ffn_w13, ffn_w2, na_wqkv, na_wo, na_rpb, cv_w1, cv_b1, cv_wdw, cv_bdw, cv_ln_g, cv_ln_b, cv_w2, cv_b2, gla_win, gla_wa1, gla_wa2, gla_ba, gla_norm_g, gla_wo, rw_mu, rw_wrkv, rw_w0, rw_w1, rw_w2, rw_a0, rw_a1, rw_a2, rw_g1, rw_g2, rw_kk, rw_ka, rw_rk, rw_gn_g, rw_gn_b, rw_wo):
    b, n, d = x.shape
    depth = ada_w.shape[0]
    assert d == D_MODEL and depth == DEPTH and n % GRID_W == 0
    rows = -(-(b + 1) // 8) * 8
    cond = jnp.zeros((rows, d), F32).at[:b].set(c).at[b].set(c_ctx)
    mod_all = _ada(cond, ada_w, ada_b)
    w13 = ffn_w13.astype(BF16)
    w2 = ffn_w2.astype(BF16)
    hx, hc = x, ctx
    for i in range(depth):
        last = i == depth - 1
        mx = mod_all[i, :b].reshape(b, 3 * N_SUB, d)
        mc = jnp.broadcast_to(mod_all[i, b].reshape(1, 3 * N_SUB, d), (b, 3 * N_SUB, d))
        hx = _ffn(hx, mx, w13[i, 0], w2[i, 0], ln_g[i, 0], ln_b[i, 0], 0)
        hc = _ffn(hc, mc, w13[i, 0], w2[i, 0], ln_g[i, 0], ln_b[i, 0], 0)
        kind, j = i % 4, i // 4
        lg, lb = ln_g[i, 1], ln_b[i, 1]
        if kind == 0:
            hx, hc = _mixer_na(hx, hc, mx, mc, na_wqkv[j], na_wo[j], na_rpb[j], lg, lb, not last)
        elif kind == 1:
            hx, hc = _mixer_conv(hx, hc, mx, mc, cv_w1[j], cv_b1[j], cv_wdw[j], cv_bdw[j], cv_ln_g[j], cv_ln_b[j],
                                 cv_w2[j], cv_b2[j], lg, lb, not last)
        elif kind == 2:
            hx, hc = _mixer_gla(hx, hc, mx, mc, gla_win[j], gla_wa1[j], gla_wa2[j], gla_ba[j], gla_norm_g[j],
                                gla_wo[j], lg, lb, not last)
        else:
            hx, hc = _mixer_rwkv(hx, hc, mx, mc, rw_mu[j], rw_wrkv[j], rw_w0[j], rw_w1[j], rw_w2[j], rw_a0[j],
                                 rw_a1[j], rw_a2[j], rw_g1[j], rw_g2[j], rw_kk[j], rw_ka[j], rw_rk[j],
                                 rw_gn_g[j], rw_gn_b[j], rw_wo[j], lg, lb, not last)
        hx = _ffn(hx, mx, w13[i, 1], w2[i, 1], ln_g[i, 2], ln_b[i, 2], 2)
        if not last:
            hc = _ffn(hc, mc, w13[i, 1], w2[i, 1], ln_g[i, 2], ln_b[i, 2], 2)
    return hx
```

```python
import functools

import numpy as np
import jax
import jax.numpy as jnp
from jax import lax
from jax.experimental import pallas as pl
from jax.experimental.pallas import tpu as pltpu

F32 = jnp.float32
BF16 = jnp.bfloat16

D_MODEL = 1024
DEPTH = 4
N_SUB = 3
GRID_W = 64
ALPHA = (2.0 * DEPTH) ** 0.25
LN_EPS = 1e-5
D_FF = 2816
NA_HEADS = 16
NA_HEAD_DIM = D_MODEL // NA_HEADS
NA_KH = 8
NA_KW = 16
CONV_WIDTH = 31
CONV_HALO = 16
GLA_HEADS = 4
GLA_DK = 128
GLA_DV = 256
GLA_GATE_RANK = 16
GLA_NORMALIZER = 16.0
GLA_CHUNK = 64
ROPE_BASE = 10000.0
RW_HEAD = 64
RW_HEADS = D_MODEL // RW_HEAD
RW_PAIRS = RW_HEADS // 2
RW_CHUNK = 64
RW_GN_EPS = 64e-5
LANES = 128
SUBLANES = 8
NEG_BIG = -1e30
VMEM_LIMIT = 56 * 1024 * 1024


def _cparams(*sem):
    return pltpu.CompilerParams(dimension_semantics=sem, vmem_limit_bytes=VMEM_LIMIT)


def _dot(a, b):
    return jnp.dot(a.astype(BF16), b.astype(BF16), preferred_element_type=F32)


def _dot_nt(a, b):
    return lax.dot_general(a.astype(BF16), b.astype(BF16), (((1,), (1,)), ((), ())),
                           preferred_element_type=F32)


def _dot_tn(a, b):
    return lax.dot_general(a.astype(BF16), b.astype(BF16), (((0,), (0,)), ((), ())),
                           preferred_element_type=F32)


def _split(x):
    hi = x.astype(BF16)
    lo = (x - hi.astype(F32)).astype(BF16)
    return hi, lo


def _dot_sel(x, sel):
    hi, lo = _split(x)
    return (jnp.dot(hi, sel, preferred_element_type=F32)
            + jnp.dot(lo, sel, preferred_element_type=F32))


def _sel_dot(sel, x):
    hi, lo = _split(x)
    return (jnp.dot(sel, hi, preferred_element_type=F32)
            + jnp.dot(sel, lo, preferred_element_type=F32))


def _sigmoid(x):
    return jax.nn.sigmoid(x)


def _softplus(x):
    return jnp.maximum(x, 0.0) + jnp.log(1.0 + jnp.exp(-jnp.abs(x)))


def _layer_norm(x, g, b, eps=LN_EPS):
    mu = jnp.mean(x, axis=-1, keepdims=True)
    xc = x - mu
    var = jnp.mean(xc * xc, axis=-1, keepdims=True)
    return xc * lax.rsqrt(var + eps) * g + b


def _mod_rows(mod_ref, j):
    return (mod_ref[0, 3 * j:3 * j + 1, :], mod_ref[0, 3 * j + 1:3 * j + 2, :],
            mod_ref[0, 3 * j + 2:3 * j + 3, :])


def _modulate(h, mod_ref, j):
    shift, scale, _ = _mod_rows(mod_ref, j)
    return h * (1.0 + scale) + shift


def _post_norm(h, y, mod_ref, j, g_ref, b_ref):
    gate = mod_ref[0, 3 * j + 2:3 * j + 3, :]
    return _layer_norm(ALPHA * h + gate * y, g_ref[...], b_ref[...])


def _tok_spec(tm, d):
    return pl.BlockSpec((1, tm, d), lambda b, t: (b, t, 0))


def _mod_spec():
    return pl.BlockSpec((1, 3 * N_SUB, D_MODEL), lambda b, t: (b, 0, 0))


def _const_spec(shape):
    nd = len(shape)
    return pl.BlockSpec(shape, lambda b, t: (0,) * nd, pipeline_mode=pl.Buffered(1))


def _tile(s, pref):
    tm = min(pref, s)
    assert s % tm == 0
    return tm


def _ada_kernel(cond_ref, w_ref, b_ref, o_ref):
    c = cond_ref[...]
    o_ref[0] = _dot(c * _sigmoid(c), w_ref[0]) + b_ref[0]


def _ada(cond, ada_w, ada_b):
    r, d = cond.shape
    depth, _, n = ada_w.shape
    tn = 1024
    return pl.pallas_call(
        _ada_kernel,
        grid=(depth, n // tn),
        in_specs=[pl.BlockSpec((r, d), lambda l, j: (0, 0)),
                  pl.BlockSpec((1, d, tn), lambda l, j: (l, 0, j)),
                  pl.BlockSpec((1, 1, tn), lambda l, j: (l, 0, j))],
        out_specs=pl.BlockSpec((1, r, tn), lambda l, j: (l, 0, j)),
        out_shape=jax.ShapeDtypeStruct((depth, r, n), F32),
        compiler_params=_cparams("parallel", "parallel"),
        name="ada_mod",
    )(cond, ada_w, ada_b.reshape(depth, 1, n))


FFN_CHUNK = 256


def _ffn_kernel(h_ref, mod_ref, w13_ref, w2_ref, g_ref, b_ref, o_ref, *, j):
    h = h_ref[0]
    u = _modulate(h, mod_ref, j).astype(BF16)
    acc = jnp.zeros(h.shape, F32)
    for c in range(D_FF // FFN_CHUNK):
        lo = c * FFN_CHUNK
        a = jnp.dot(u, w13_ref[:, lo:lo + FFN_CHUNK], preferred_element_type=F32)
        v = jnp.dot(u, w13_ref[:, D_FF + lo:D_FF + lo + FFN_CHUNK], preferred_element_type=F32)
        z = (a * _sigmoid(a) * v).astype(BF16)
        acc = acc + jnp.dot(z, w2_ref[lo:lo + FFN_CHUNK, :], preferred_element_type=F32)
    o_ref[0] = _post_norm(h, 0.5 * acc, mod_ref, j, g_ref, b_ref)


def _ffn(h, mod, w13, w2, ln_g, ln_b, j):
    b, s, d = h.shape
    tm = _tile(s, 512)
    return pl.pallas_call(
        functools.partial(_ffn_kernel, j=j),
        grid=(b, s // tm),
        in_specs=[_tok_spec(tm, d), _mod_spec(), _const_spec(w13.shape), _const_spec(w2.shape),
                  _const_spec((1, d)), _const_spec((1, d))],
        out_specs=_tok_spec(tm, d),
        out_shape=jax.ShapeDtypeStruct(h.shape, F32),
        compiler_params=_cparams("parallel", "parallel"),
        name="ffn_half",
    )(h, mod, w13, w2, ln_g.reshape(1, d), ln_b.reshape(1, d))


def _out_kernel(*refs, n_in, prologue, has_bias):
    ins = refs[:n_in]
    w_ref = refs[n_in]
    k = n_in + 1
    bias_ref = None
    if has_bias:
        bias_ref = refs[k]
        k += 1
    h_ref, mod_ref, g_ref, b_ref, o_ref = refs[k:k + 5]
    y = jnp.dot(prologue(*ins), w_ref[...], preferred_element_type=F32)
    if has_bias:
        y = y + bias_ref[...]
    o_ref[0] = _post_norm(h_ref[0], y, mod_ref, 1, g_ref, b_ref)


def _out_proj(tok_inputs, const_inputs, prologue, w, bias, h, mod, ln_g, ln_b, name):
    b, s, d = h.shape
    tm = _tile(s, 256)
    n_in = len(tok_inputs) + len(const_inputs)
    in_specs = [_tok_spec(tm, a.shape[-1]) for a in tok_inputs]
    in_specs += [_const_spec(a.shape) for a in const_inputs]
    in_specs.append(_const_spec(w.shape))
    args = list(tok_inputs) + list(const_inputs) + [w]
    if bias is not None:
        in_specs.append(_const_spec((1, d)))
        args.append(bias.reshape(1, d))
    in_specs += [_tok_spec(tm, d), _mod_spec(), _const_spec((1, d)), _const_spec((1, d))]
    args += [h, mod, ln_g.reshape(1, d), ln_b.reshape(1, d)]
    return pl.pallas_call(
        functools.partial(_out_kernel, n_in=n_in, prologue=prologue, has_bias=bias is not None),
        grid=(b, s // tm),
        in_specs=in_specs,
        out_specs=_tok_spec(tm, d),
        out_shape=jax.ShapeDtypeStruct(h.shape, F32),
        compiler_params=_cparams("parallel", "parallel"),
        name=name,
    )(*args)


def _plain_prologue(y_ref):
    return y_ref[0].astype(BF16)


def _na_qkv_kernel(h_ref, mod_ref, w_ref, q_ref, k_ref, v_ref):
    d = D_MODEL
    u = _modulate(h_ref[0], mod_ref, 1).astype(BF16)
    z = jnp.dot(u, w_ref[...], preferred_element_type=F32)
    q_ref[0] = (z[:, :d] * (NA_HEAD_DIM ** -0.5)).astype(BF16)
    k_ref[0] = z[:, d:2 * d].astype(BF16)
    v_ref[0] = z[:, 2 * d:].astype(BF16)


def _na_qkv(h, mod, w):
    b, s, d = h.shape
    tm = _tile(s, 256)
    out = jax.ShapeDtypeStruct(h.shape, BF16)
    return pl.pallas_call(
        _na_qkv_kernel,
        grid=(b, s // tm),
        in_specs=[_tok_spec(tm, d), _mod_spec(), _const_spec(w.shape)],
        out_specs=[_tok_spec(tm, d)] * 3,
        out_shape=[out] * 3,
        compiler_params=_cparams("parallel", "parallel"),
        name="na_qkv",
    )(h, mod, w)


def _na_row_start(r, rows, kh):
    return jnp.clip(r - kh // 2, 0, rows - kh)


def _na_attn_kernel(q_ref, k_ref, v_ref, kc_ref, vc_ref, bias_ref, o_ref, *, rows, kh):
    r = pl.program_id(1)
    start = pl.multiple_of(_na_row_start(r, rows, kh) * GRID_W, GRID_W)
    win = kh * GRID_W
    heads = range(NA_HEADS)
    sls = [slice(hd * NA_HEAD_DIM, (hd + 1) * NA_HEAD_DIM) for hd in heads]
    q = [q_ref[0, :, sl] for sl in sls]
    s_loc = [_dot_nt(q[hd], k_ref[0, pl.ds(start, win), sls[hd]]) + bias_ref[0, hd] for hd in heads]
    s_ctx = [_dot_nt(q[hd], kc_ref[0, :, sls[hd]]) for hd in heads]
    m = [jnp.maximum(jnp.max(s_loc[hd], axis=-1, keepdims=True), jnp.max(s_ctx[hd], axis=-1, keepdims=True))
         for hd in heads]
    p_loc = [jnp.exp(s_loc[hd] - m[hd]) for hd in heads]
    p_ctx = [jnp.exp(s_ctx[hd] - m[hd]) for hd in heads]
    den = [jnp.sum(p_loc[hd], axis=-1, keepdims=True) + jnp.sum(p_ctx[hd], axis=-1, keepdims=True) for hd in heads]
    o = [_dot(p_loc[hd], v_ref[0, pl.ds(start, win), sls[hd]]) + _dot(p_ctx[hd], vc_ref[0, :, sls[hd]])
         for hd in heads]
    for hd in heads:
        o_ref[0, :, sls[hd]] = (o[hd] / den[hd]).astype(BF16)


def _na_bias_table(rpb, kh):
    qc = np.arange(GRID_W)[:, None]
    kc = np.arange(GRID_W)[None, :]
    cstart = np.clip(qc - NA_KW // 2, 0, GRID_W - NA_KW)
    ok = (kc >= cstart) & (kc < cstart + NA_KW)
    dc = np.clip(kc - qc + NA_KW - 1, 0, 2 * NA_KW - 2)
    tbl = jnp.where(ok[None, None], rpb[:, :, dc], NEG_BIG)
    n_s = 2 * NA_KH - kh
    out = []
    for s in range(n_s):
        t = tbl[:, s:s + kh]
        out.append(jnp.transpose(t, (0, 2, 1, 3)).reshape(NA_HEADS, GRID_W, kh * GRID_W))
    return jnp.stack(out, axis=0)


def _na_attn(q, k, v, kc, vc, bias_tab):
    b, n, d = q.shape
    l = kc.shape[1]
    rows = n // GRID_W
    kh = min(NA_KH, rows)

    def bias_idx(bi, r):
        return (_na_row_start(r, rows, kh) - r + NA_KH - 1, 0, 0, 0)

    return pl.pallas_call(
        functools.partial(_na_attn_kernel, rows=rows, kh=kh),
        grid=(b, rows),
        in_specs=[pl.BlockSpec((1, GRID_W, d), lambda bi, r: (bi, r, 0)),
                  pl.BlockSpec((1, n, d), lambda bi, r: (bi, 0, 0)),
                  pl.BlockSpec((1, n, d), lambda bi, r: (bi, 0, 0)),
                  pl.BlockSpec((1, l, d), lambda bi, r: (bi, 0, 0)),
                  pl.BlockSpec((1, l, d), lambda bi, r: (bi, 0, 0)),
                  pl.BlockSpec((1, NA_HEADS, GRID_W, kh * GRID_W), bias_idx)],
        out_specs=pl.BlockSpec((1, GRID_W, d), lambda bi, r: (bi, r, 0)),
        out_shape=jax.ShapeDtypeStruct(q.shape, BF16),
        compiler_params=_cparams("parallel", "arbitrary"),
        name="na_attn",
    )(q, k, v, kc, vc, bias_tab)


def _ctx_attn_kernel(q_ref, k_ref, v_ref, o_ref):
    for hd in range(NA_HEADS):
        sl = slice(hd * NA_HEAD_DIM, (hd + 1) * NA_HEAD_DIM)
        s = _dot_nt(q_ref[0, :, sl], k_ref[0, :, sl])
        p = jnp.exp(s - jnp.max(s, axis=-1, keepdims=True))
        den = jnp.sum(p, axis=-1, keepdims=True)
        o_ref[0, :, sl] = (_dot(p, v_ref[0, :, sl]) / den).astype(BF16)


def _ctx_attn(q, k, v):
    b, l, d = q.shape
    spec = pl.BlockSpec((1, l, d), lambda bi: (bi, 0, 0))
    return pl.pallas_call(
        _ctx_attn_kernel,
        grid=(b,),
        in_specs=[spec] * 3,
        out_specs=spec,
        out_shape=jax.ShapeDtypeStruct(q.shape, BF16),
        compiler_params=_cparams("parallel"),
        name="ctx_attn",
    )(q, k, v)


def _mixer_na(hx, hc, mx, mc, w_qkv, w_o, rpb, ln_g, ln_b, ctx_out):
    w_qkv = w_qkv.astype(BF16)
    w_o = w_o.astype(BF16)
    qx, kx, vx = _na_qkv(hx, mx, w_qkv)
    qc, kc, vc = _na_qkv(hc, mc, w_qkv)
    kh = min(NA_KH, hx.shape[1] // GRID_W)
    ox = _na_attn(qx, kx, vx, kc, vc, _na_bias_table(rpb, kh))
    hx = _out_proj([ox], [], _plain_prologue, w_o, None, hx, mx, ln_g, ln_b, "na_out")
    if ctx_out:
        oc = _ctx_attn(qc, kc, vc)
        hc = _out_proj([oc], [], _plain_prologue, w_o, None, hc, mc, ln_g, ln_b, "na_out_ctx")
    return hx, hc


def _conv_in_kernel(h_ref, mod_ref, w_ref, b_ref, o_ref):
    d = D_MODEL
    u = _modulate(h_ref[0], mod_ref, 1).astype(BF16)
    z = jnp.dot(u, w_ref[...], preferred_element_type=F32) + b_ref[...]
    o_ref[0] = z[:, :d] * _sigmoid(z[:, d:])


def _conv_in(h, mod, w1, b1):
    b, s, d = h.shape
    tm = _tile(s, 256)
    return pl.pallas_call(
        _conv_in_kernel,
        grid=(b, s // tm),
        in_specs=[_tok_spec(tm, d), _mod_spec(), _const_spec(w1.shape), _const_spec((1, 2 * d))],
        out_specs=_tok_spec(tm, d),
        out_shape=jax.ShapeDtypeStruct(h.shape, F32),
        compiler_params=_cparams("parallel", "parallel"),
        name="conv_in",
    )(h, mod, w1, b1.reshape(1, 2 * d))


CONV_ROWS = 16


def _conv_dw_kernel(prev_ref, cur_ref, next_ref, w_ref, bdw_ref, g_ref, b_ref, o_ref, win_ref, *, ts, nt):
    t = pl.program_id(1)
    halo = CONV_HALO
    d = cur_ref.shape[-1]
    rows = ts + 2 * halo
    win_ref[0, halo:halo + ts, :] = cur_ref[0]
    win_ref[0, 0:halo, :] = jnp.where(t > 0, prev_ref[0], 0.0)
    win_ref[0, halo + ts:rows, :] = jnp.where(t < nt - 1, next_ref[0], 0.0)
    sub = lax.broadcasted_iota(jnp.int32, (rows // SUBLANES, SUBLANES, LANES), 1)
    for cc in range(d // LANES):
        cols = slice(cc * LANES, (cc + 1) * LANES)
        w = win_ref[0, :, cols].reshape(rows // SUBLANES, SUBLANES, LANES)
        for i in range(1, SUBLANES):
            rot = pltpu.roll(w, SUBLANES - i, axis=1)
            nxt = jnp.concatenate([rot[1:], rot[:1]], axis=0)
            win_ref[i, :, cols] = jnp.where(sub < SUBLANES - i, rot, nxt).reshape(rows, LANES)
    off = halo - CONV_WIDTH // 2
    slabs = CONV_ROWS // SUBLANES
    for g in range(ts // CONV_ROWS):
        acc = jnp.broadcast_to(bdw_ref[...], (slabs, SUBLANES, d))
        for k in range(CONV_WIDTH):
            i, base = (off + k) % SUBLANES, g * CONV_ROWS + (off + k) // SUBLANES * SUBLANES
            acc = acc + win_ref[i, base:base + CONV_ROWS, :].reshape(slabs, SUBLANES, d) * w_ref[k][None]
        y = _layer_norm(acc.reshape(CONV_ROWS, d), g_ref[...], b_ref[...])
        o_ref[0, g * CONV_ROWS:(g + 1) * CONV_ROWS, :] = (y * _sigmoid(y)).astype(BF16)


def _conv_dw(u, w_dw, b_dw, ln_g, ln_b):
    b, s, d = u.shape
    ts = _tile(s, 128)
    nt = s // ts
    hb = ts // CONV_HALO
    nhb = s // CONV_HALO
    w_rep = jnp.broadcast_to(w_dw[:, None, :], (CONV_WIDTH, SUBLANES, d))
    return pl.pallas_call(
        functools.partial(_conv_dw_kernel, ts=ts, nt=nt),
        grid=(b, nt),
        in_specs=[pl.BlockSpec((1, CONV_HALO, d), lambda bi, t: (bi, jnp.maximum(t * hb - 1, 0), 0)),
                  _tok_spec(ts, d),
                  pl.BlockSpec((1, CONV_HALO, d), lambda bi, t: (bi, jnp.minimum((t + 1) * hb, nhb - 1), 0)),
                  _const_spec(w_rep.shape), _const_spec((1, d)), _const_spec((1, d)), _const_spec((1, d))],
        out_specs=_tok_spec(ts, d),
        out_shape=jax.ShapeDtypeStruct(u.shape, BF16),
        scratch_shapes=[pltpu.VMEM((SUBLANES, ts + 2 * CONV_HALO, d), F32)],
        compiler_params=_cparams("parallel", "arbitrary"),
        name="conv_dw",
    )(u, u, u, w_rep, b_dw.reshape(1, d), ln_g.reshape(1, d), ln_b.reshape(1, d))


def _mixer_conv(hx, hc, mx, mc, w1, b1, w_dw, b_dw, cln_g, cln_b, w2, b2, ln_g, ln_b, ctx_out):
    w1 = w1.astype(BF16)
    w2 = w2.astype(BF16)

    def branch(h, mod, name):
        u = _conv_in(h, mod, w1, b1)
        y = _conv_dw(u, w_dw, b_dw, cln_g, cln_b)
        return _out_proj([y], [], _plain_prologue, w2, b2, h, mod, ln_g, ln_b, name)

    hx = branch(hx, mx, "conv_out")
    if ctx_out:
        hc = branch(hc, mc, "conv_out_ctx")
    return hx, hc


GLA_NQK = GLA_HEADS * GLA_DK


def _gla_in_kernel(*refs, rotary):
    if rotary:
        h_ref, mod_ref, w_ref, wa2_ref, ba_ref, cos_ref, sin_ref = refs[:7]
        outs = refs[7:]
    else:
        h_ref, mod_ref, w_ref, wa2_ref, ba_ref = refs[:5]
        outs = refs[5:]
    q_ref, k_ref, v_ref, g_ref, laf_ref, lab_ref = outs
    n, d = GLA_NQK, D_MODEL
    u = _modulate(h_ref[0], mod_ref, 1).astype(BF16)
    z = jnp.dot(u, w_ref[...], preferred_element_type=F32)
    q = z[:, :n]
    k = z[:, n:2 * n]
    o = 2 * n
    if rotary:
        cos = cos_ref[...]
        sin = sin_ref[...]
        q = q * cos + z[:, o:o + n] * sin
        k = k * cos + z[:, o + n:o + 2 * n] * sin
        o += 2 * n
    q_ref[0] = q * (GLA_DK ** -0.5)
    k_ref[0] = k
    v_ref[0] = z[:, o:o + d].astype(BF16)
    g_ref[0] = z[:, o + d:o + 2 * d]
    t = z[:, o + 2 * d:o + 2 * d + LANES].astype(BF16)
    for dr, la_ref in enumerate((laf_ref, lab_ref)):
        x = jnp.dot(t, wa2_ref[dr], preferred_element_type=F32) + ba_ref[dr:dr + 1, :]
        la_ref[0] = -_softplus(-x) * (1.0 / GLA_NORMALIZER)


def _gla_rope_tables(n):
    quarter = GLA_DK // 4
    t = np.arange(n)
    freqs = ROPE_BASE ** (-jnp.arange(quarter, dtype=F32) / quarter)
    ang_r = jnp.asarray(t // GRID_W, F32)[:, None] * freqs
    ang_c = jnp.asarray(t % GRID_W, F32)[:, None] * freqs
    cos = jnp.concatenate([jnp.cos(ang_r), jnp.cos(ang_r), jnp.cos(ang_c), jnp.cos(ang_c)], axis=-1)
    sin = jnp.concatenate([-jnp.sin(ang_r), jnp.sin(ang_r), -jnp.sin(ang_c), jnp.sin(ang_c)], axis=-1)
    return jnp.tile(cos, (1, GLA_HEADS)), jnp.tile(sin, (1, GLA_HEADS))


def _gla_swap_perm():
    quarter = GLA_DK // 4
    base = np.concatenate([np.arange(quarter, 2 * quarter), np.arange(0, quarter),
                           np.arange(3 * quarter, 4 * quarter), np.arange(2 * quarter, 3 * quarter)])
    return np.concatenate([h * GLA_DK + base for h in range(GLA_HEADS)])


def _gla_in(h, mod, w_ext, wa2_ext, ba, rope):
    b, s, d = h.shape
    tm = _tile(s, 256)
    rotary = rope is not None
    in_specs = [_tok_spec(tm, d), _mod_spec(), _const_spec(w_ext.shape), _const_spec(wa2_ext.shape),
                _const_spec(ba.shape)]
    args = [h, mod, w_ext, wa2_ext, ba]
    if rotary:
        in_specs += [pl.BlockSpec((tm, GLA_NQK), lambda bi, t: (t, 0))] * 2
        args += list(rope)
    shp = lambda w, dt: jax.ShapeDtypeStruct((b, s, w), dt)
    return pl.pallas_call(
        functools.partial(_gla_in_kernel, rotary=rotary),
        grid=(b, s // tm),
        in_specs=in_specs,
        out_specs=[_tok_spec(tm, GLA_NQK), _tok_spec(tm, GLA_NQK), _tok_spec(tm, d), _tok_spec(tm, d),
                   _tok_spec(tm, GLA_NQK), _tok_spec(tm, GLA_NQK)],
        out_shape=[shp(GLA_NQK, F32), shp(GLA_NQK, F32), shp(d, BF16), shp(d, F32),
                   shp(GLA_NQK, F32), shp(GLA_NQK, F32)],
        compiler_params=_cparams("parallel", "parallel"),
        name="gla_in_rope" if rotary else "gla_in",
    )(*args)


def _tri_masks(c):
    row = lax.broadcasted_iota(jnp.int32, (c, c), 0)
    col = lax.broadcasted_iota(jnp.int32, (c, c), 1)
    return row >= col, row <= col


def _gla_core_kernel(qf_ref, kf_ref, vf_ref, laf_ref, qb_ref, kb_ref, vb_ref, lab_ref, s0f_ref, s0b_ref,
                     of_ref, ob_ref, sf_ref, sb_ref, stf_ref, stb_ref, *, nchunks):
    i = pl.program_id(1)
    c = GLA_CHUNK

    @pl.when(i == 0)
    def _():
        stf_ref[...] = s0f_ref[0]
        stb_ref[...] = s0b_ref[0]

    lower, upper = _tri_masks(c)
    dirs = ((qf_ref, kf_ref, vf_ref, laf_ref, of_ref, stf_ref, lower, c - 1),
            (qb_ref, kb_ref, vb_ref, lab_ref, ob_ref, stb_ref, upper, 0))
    insts = []
    for q_ref, k_ref, v_ref, la_ref, o_ref, st_ref, mask, last in dirs:
        tri = mask.astype(BF16)
        hi, lo = _split(la_ref[0])
        bcum = jnp.dot(tri, hi, preferred_element_type=F32) + jnp.dot(tri, lo, preferred_element_type=F32)
        blast = bcum[last:last + 1, :]
        qd = (q_ref[0] * jnp.exp(bcum)).astype(BF16)
        k = k_ref[0]
        kd = (k * jnp.exp(-bcum)).astype(BF16)
        kt = (k * jnp.exp(blast - bcum)).astype(BF16)
        for hd in range(GLA_HEADS):
            ks = slice(hd * GLA_DK, (hd + 1) * GLA_DK)
            vs = slice(hd * GLA_DV, (hd + 1) * GLA_DV)
            insts.append(dict(qd=qd[:, ks], kd=kd[:, ks], kt=kt[:, ks], dec=jnp.exp(blast[:, ks]),
                              v=v_ref[0, :, vs], mask=mask, o_ref=o_ref, st_ref=st_ref, hd=hd, vs=vs))
    att = [jnp.where(t["mask"], _dot_nt(t["qd"], t["kd"]), 0.0) for t in insts]
    s_in = [t["st_ref"][t["hd"]] for t in insts]
    out = [_dot(a, t["v"]) + _dot_nt(t["qd"], s) for a, t, s in zip(att, insts, s_in)]
    for t, o in zip(insts, out):
        t["o_ref"][0, :, t["vs"]] = o
    upd = [_dot_tn(t["v"], t["kt"]) for t in insts]
    for t, s, u in zip(insts, s_in, upd):
        t["st_ref"][t["hd"]] = t["dec"] * s + u

    @pl.when(i == nchunks - 1)
    def _():
        sf_ref[0] = stf_ref[...]
        sb_ref[0] = stb_ref[...]


def _gla_core(q, k, v, la_f, la_b, s0f, s0b):
    b, s, _ = q.shape
    c = GLA_CHUNK
    nchunks = s // c
    fwd = lambda w: pl.BlockSpec((1, c, w), lambda bi, i: (bi, i, 0))
    bwd = lambda w: pl.BlockSpec((1, c, w), lambda bi, i: (bi, nchunks - 1 - i, 0))
    st_shape = (GLA_HEADS, GLA_DV, GLA_DK)
    st_spec = pl.BlockSpec((1,) + st_shape, lambda bi, i: (bi, 0, 0, 0))
    return pl.pallas_call(
        functools.partial(_gla_core_kernel, nchunks=nchunks),
        grid=(b, nchunks),
        in_specs=[fwd(GLA_NQK), fwd(GLA_NQK), fwd(D_MODEL), fwd(GLA_NQK),
                  bwd(GLA_NQK), bwd(GLA_NQK), bwd(D_MODEL), bwd(GLA_NQK), st_spec, st_spec],
        out_specs=[fwd(D_MODEL), bwd(D_MODEL), st_spec, st_spec],
        out_shape=[jax.ShapeDtypeStruct((b, s, D_MODEL), F32)] * 2
        + [jax.ShapeDtypeStruct((b,) + st_shape, F32)] * 2,
        scratch_shapes=[pltpu.VMEM(st_shape, F32), pltpu.VMEM(st_shape, F32)],
        compiler_params=_cparams("parallel", "arbitrary"),
        name="gla_core",
    )(q, k, v, la_f, q, k, v, la_b, s0f, s0b)


def _gla_finish_prologue(of_ref, ob_ref, g_ref, ng_ref):
    o = of_ref[0] + ob_ref[0]
    g = g_ref[0]
    ng = ng_ref[...]
    parts = []
    for hd in range(GLA_HEADS):
        vs = slice(hd * GLA_DV, (hd + 1) * GLA_DV)
        oh = o[:, vs]
        gh = g[:, vs]
        oh = oh * lax.rsqrt(jnp.mean(oh * oh, axis=-1, keepdims=True) + LN_EPS) * ng
        parts.append((oh * (gh * _sigmoid(gh))).astype(BF16))
    return jnp.concatenate(parts, axis=1)


def _mixer_gla(hx, hc, mx, mc, w_in, w_a1, w_a2, b_a, norm_g, w_o, ln_g, ln_b, ctx_out):
    b, n, d = hx.shape
    nqk = GLA_NQK
    pad = jnp.zeros((d, LANES - 2 * GLA_GATE_RANK), F32)
    tail = [w_in[:, 2 * nqk:], w_a1[0], w_a1[1], pad]
    w_plain = jnp.concatenate([w_in[:, :2 * nqk]] + tail, axis=1).astype(BF16)
    perm = _gla_swap_perm()
    w_rope = jnp.concatenate([w_in[:, :2 * nqk], w_in[:, :nqk][:, perm], w_in[:, nqk:2 * nqk][:, perm]] + tail,
                             axis=1).astype(BF16)
    wa2_ext = jnp.zeros((2, LANES, nqk), F32)
    wa2_ext = wa2_ext.at[0, :GLA_GATE_RANK].set(w_a2[0])
    wa2_ext = wa2_ext.at[1, GLA_GATE_RANK:2 * GLA_GATE_RANK].set(w_a2[1]).astype(BF16)
    w_o = w_o.astype(BF16)

    qc, kc, vc, gc, lcf, lcb = _gla_in(hc, mc, w_plain, wa2_ext, b_a, None)
    qx, kx, vx, gx, lxf, lxb = _gla_in(hx, mx, w_rope, wa2_ext, b_a, _gla_rope_tables(n))
    s0 = jnp.zeros((b, GLA_HEADS, GLA_DV, GLA_DK), F32)
    ocf, ocb, scf, scb = _gla_core(qc, kc, vc, lcf, lcb, s0, s0)
    oxf, oxb, _, _ = _gla_core(qx, kx, vx, lxf, lxb, scf, scb)
    ng = norm_g.reshape(1, GLA_DV)
    hx = _out_proj([oxf, oxb, gx], [ng], _gla_finish_prologue, w_o, None, hx, mx, ln_g, ln_b, "gla_out")
    if ctx_out:
        hc = _out_proj([ocf, ocb, gc], [ng], _gla_finish_prologue, w_o, None, hc, mc, ln_g, ln_b, "gla_out_ctx")
    return hx, hc


def _head_sum_matrices():
    g = np.zeros((D_MODEL, LANES), np.float32)
    g[np.arange(D_MODEL), np.arange(D_MODEL) // RW_HEAD] = 1.0
    return jnp.asarray(g, BF16), jnp.asarray(g.T.copy(), BF16)


def _rw_prep_kernel(hp_ref, h_ref, hn_ref, mod_ref, mu_ref, wr_ref, wk_ref, wv_ref, g1_ref, g2_ref,
                    w1_ref, w2_ref, a1_ref, a2_ref, w0_ref, a0_ref, kkw_ref, kaw_ref, rkw_ref, gs_ref, gt_ref,
                    r_ref, kk_ref, v_ref, g_ref, bonus_ref,
                    lw0_ref, k0_ref, b0_ref, lw1_ref, k1_ref, b1_ref, *, tm, nt):
    t = pl.program_id(1)
    u = _modulate(h_ref[0], mod_ref, 1)
    up = _modulate(hp_ref[0, 7:8, :], mod_ref, 1)
    un = _modulate(hn_ref[0, 0:1, :], mod_ref, 1)
    up = jnp.where(t > 0, up, 0.0)
    un = jnp.where(t < nt - 1, un, 0.0)
    rowid = lax.broadcasted_iota(jnp.int32, (tm, 1), 0)
    prev = jnp.where(rowid == 0, up, pltpu.roll(u, 1, axis=0))
    nxt = jnp.where(rowid == tm - 1, un, pltpu.roll(u, tm - 1, axis=0))
    xx = 0.5 * (prev + nxt) - u
    mix = lambda j: (u + xx * mu_ref[j:j + 1, :]).astype(BF16)
    gs = gs_ref[...]
    gt = gt_ref[...]

    r = jnp.dot(mix(0), wr_ref[...], preferred_element_type=F32)
    k = jnp.dot(mix(2), wk_ref[...], preferred_element_type=F32)
    v = jnp.dot(mix(3), wv_ref[...], preferred_element_type=F32)
    hw = jnp.tanh(jnp.dot(mix(1), w1_ref[...], preferred_element_type=F32)).astype(BF16)
    ha = jnp.dot(mix(4), a1_ref[...], preferred_element_type=F32).astype(BF16)
    gg = _sigmoid(jnp.dot(mix(5), g1_ref[...], preferred_element_type=F32)).astype(BF16)
    g_ref[0] = jnp.dot(gg, g2_ref[...], preferred_element_type=F32)

    kk = k * kkw_ref[...]
    nrm = jnp.maximum(jnp.sqrt(_dot_sel(kk * kk, gs)), 1e-12)
    kk = kk * _dot_sel(1.0 / nrm, gt)
    ksum = jnp.zeros_like(k)
    for dr, (lw_ref, kd_ref, bd_ref) in enumerate(((lw0_ref, k0_ref, b0_ref), (lw1_ref, k1_ref, b1_ref))):
        xw = w0_ref[dr:dr + 1, :] + jnp.dot(hw, w2_ref[dr], preferred_element_type=F32)
        w_log = -_softplus(-xw) - 0.5
        lw_ref[0] = -jnp.exp(w_log)
        a = _sigmoid(a0_ref[dr:dr + 1, :] + jnp.dot(ha, a2_ref[dr], preferred_element_type=F32))
        kd = k * (1.0 + (a - 1.0) * kaw_ref[...])
        kd_ref[0] = kd
        bd_ref[0] = kk * a
        ksum = ksum + kd
    r_ref[0] = r
    kk_ref[0] = kk
    v_ref[0] = v.astype(BF16)
    bonus_ref[0] = _dot_sel(_dot_sel(r * rkw_ref[...] * ksum, gs), gt) * v


def _rw_prep(h, mod, consts):
    b, s, d = h.shape
    tm = _tile(s, 256)
    nt = s // tm
    hb = tm // 8
    nhb = s // 8
    in_specs = [pl.BlockSpec((1, 8, d), lambda bi, t: (bi, jnp.maximum(t * hb - 1, 0), 0)),
                _tok_spec(tm, d),
                pl.BlockSpec((1, 8, d), lambda bi, t: (bi, jnp.minimum((t + 1) * hb, nhb - 1), 0)),
                _mod_spec()] + [_const_spec(a.shape) for a in consts]
    f32o = jax.ShapeDtypeStruct(h.shape, F32)
    out_shape = [f32o, f32o, jax.ShapeDtypeStruct(h.shape, BF16), f32o, f32o] + [f32o] * 6
    return pl.pallas_call(
        functools.partial(_rw_prep_kernel, tm=tm, nt=nt),
        grid=(b, nt),
        in_specs=in_specs,
        out_specs=[_tok_spec(tm, d)] * 11,
        out_shape=out_shape,
        compiler_params=_cparams("parallel", "arbitrary"),
        name="rw_prep",
    )(h, h, h, mod, *consts)


def _rw_prelude(refs, forward):
    r_ref, kk_ref, v_ref, lw_ref, k_ref, b_ref = refs
    c = RW_CHUNK
    row = lax.broadcasted_iota(jnp.int32, (c, 4 * c), 0)
    col = lax.broadcasted_iota(jnp.int32, (c, 4 * c), 1) % c
    lower, upper = _tri_masks(c)
    if forward:
        strict, incl, tri, last = row > col, row >= col, lower, c - 1
    else:
        strict, incl, tri, last = row < col, row <= col, upper, 0
    tri = tri.astype(BF16)
    lw = lw_ref[0]
    hi, lo = _split(lw)
    cum = jnp.dot(tri, hi, preferred_element_type=F32) + jnp.dot(tri, lo, preferred_element_type=F32)
    tot = cum[last:last + 1, :]
    e_neg = jnp.exp(-cum)
    e_tail = jnp.exp(tot - cum)
    b_raw = b_ref[0]
    k_raw = k_ref[0]
    return dict(
        strict=strict, incl=incl, dec=jnp.exp(tot),
        a=(kk_ref[0] * jnp.exp(cum - lw)).astype(BF16), r=(r_ref[0] * jnp.exp(cum)).astype(BF16),
        bt=(b_raw * e_neg).astype(BF16), kt=(k_raw * e_neg).astype(BF16),
        bh=(b_raw * e_tail).astype(BF16), kh=(k_raw * e_tail).astype(BF16), v=v_ref[0])


def _rw_scan_chunk(pre_f, pre_b, stf_ref, stb_ref, yf_ref, yb_ref):
    c = RW_CHUNK
    rowp = lax.broadcasted_iota(jnp.int32, (c, 2 * c), 0)
    colp = lax.broadcasted_iota(jnp.int32, (c, 2 * c), 1) % c
    eye = (rowp == colp).astype(F32)
    blocks = [(rowp // s) == (colp // s) for s in (2 ** e for e in range(1, int(np.log2(c)) + 1))]
    row2 = lax.broadcasted_iota(jnp.int32, (LANES, LANES), 0)
    col2 = lax.broadcasted_iota(jnp.int32, (LANES, LANES), 1)
    same_head = (row2 < RW_HEAD) == (col2 < RW_HEAD)
    lane0 = lax.broadcasted_iota(jnp.int32, (1, LANES), 1) < RW_HEAD
    zero = jnp.zeros((c, LANES), BF16)

    def bdiag(x):
        x = x.astype(BF16)
        return jnp.concatenate([jnp.where(lane0, x, zero), jnp.where(lane0, zero, x)], axis=0)

    insts = [(pre, st_ref, y_ref, slice(p * LANES, (p + 1) * LANES), p)
             for pre, st_ref, y_ref in ((pre_f, stf_ref, yf_ref), (pre_b, stb_ref, yb_ref))
             for p in range(RW_PAIRS)]
    n = len(insts)
    cut = lambda name: [pre[name][:, sl] for pre, _, _, sl, _ in insts]
    at, rt, bt, kt, bh, kh, v = (cut(nm) for nm in ("a", "r", "bt", "kt", "bh", "kh", "v"))
    ar = [jnp.concatenate([at[i], rt[i]], axis=0) for i in range(n)]
    gram = [_dot_nt(ar[i], jnp.concatenate([bdiag(bt[i]), bdiag(kt[i])], axis=0)) for i in range(n)]
    n1 = [jnp.where(insts[i][0]["strict"][:, :2 * c], gram[i][:c, :2 * c], 0.0) for i in range(n)]
    a_ak = [jnp.where(insts[i][0]["strict"][:, :2 * c], gram[i][:c, 2 * c:], 0.0) for i in range(n)]
    r_bk = [jnp.where(insts[i][0]["incl"], gram[i][c:], 0.0) for i in range(n)]
    zt = [st_ref[p] for _, st_ref, _, _, p in insts]
    sz = [_dot_nt(ar[i], zt[i]) for i in range(n)]
    w = [sz[i][:c] + _dot(a_ak[i], bdiag(v[i])) for i in range(n)]
    inv = [eye - jnp.where(blocks[0], n1[i], 0.0) for i in range(n)]
    for lvl in range(1, len(blocks)):
        sel = blocks[lvl] & ~blocks[lvl - 1]
        tmp = [_dot(jnp.where(sel, n1[i], 0.0), bdiag(inv[i])) for i in range(n)]
        inv = [inv[i] - _dot(inv[i], bdiag(tmp[i])) for i in range(n)]
    u = [-_dot(inv[i], bdiag(w[i])) for i in range(n)]
    ys = [_dot(r_bk[i], jnp.concatenate([bdiag(u[i]), bdiag(v[i])], axis=0)) for i in range(n)]
    for i, (_, _, y_ref, sl, _) in enumerate(insts):
        y_ref[0, :, sl] = sz[i][c:] + ys[i]
    upd = [_dot_tn(jnp.concatenate([u[i].astype(BF16), v[i]], axis=0), jnp.concatenate([bh[i], kh[i]], axis=0))
           for i in range(n)]
    for i, (pre, st_ref, _, sl, p) in enumerate(insts):
        st_ref[p] = pre["dec"][:, sl] * zt[i] + jnp.where(same_head, upd[i], 0.0)


def _rw_scan_kernel(*refs, nchunks):
    fwd_in, bwd_in = refs[0:6], refs[6:12]
    s0f_ref, s0b_ref, yf_ref, yb_ref, sf_ref, sb_ref, stf_ref, stb_ref = refs[12:]
    i = pl.program_id(1)

    @pl.when(i == 0)
    def _():
        stf_ref[...] = s0f_ref[0]
        stb_ref[...] = s0b_ref[0]

    _rw_scan_chunk(_rw_prelude(fwd_in, True), _rw_prelude(bwd_in, False), stf_ref, stb_ref, yf_ref, yb_ref)

    @pl.when(i == nchunks - 1)
    def _():
        sf_ref[0] = stf_ref[...]
        sb_ref[0] = stb_ref[...]


def _rw_scan(r, kk, v, dirs, s0f, s0b):
    b, s, d = r.shape
    c = RW_CHUNK
    nchunks = s // c
    fwd = pl.BlockSpec((1, c, d), lambda bi, i: (bi, i, 0))
    bwd = pl.BlockSpec((1, c, d), lambda bi, i: (bi, nchunks - 1 - i, 0))
    st_shape = (RW_PAIRS, LANES, LANES)
    st_spec = pl.BlockSpec((1,) + st_shape, lambda bi, i: (bi, 0, 0, 0))
    (lw0, k0, b0), (lw1, k1, b1) = dirs
    return pl.pallas_call(
        functools.partial(_rw_scan_kernel, nchunks=nchunks),
        grid=(b, nchunks),
        in_specs=[fwd] * 6 + [bwd] * 6 + [st_spec, st_spec],
        out_specs=[fwd, bwd, st_spec, st_spec],
        out_shape=[jax.ShapeDtypeStruct((b, s, d), F32)] * 2 + [jax.ShapeDtypeStruct((b,) + st_shape, F32)] * 2,
        scratch_shapes=[pltpu.VMEM(st_shape, F32), pltpu.VMEM(st_shape, F32)],
        compiler_params=_cparams("parallel", "arbitrary"),
        name="rw_scan",
    )(r, kk, v, lw0, k0, b0, r, kk, v, lw1, k1, b1, s0f, s0b)


def _rw_finish_prologue(yf_ref, yb_ref, bonus_ref, g_ref, gs_ref, gt_ref, gng_ref, gnb_ref):
    y = yf_ref[0] + yb_ref[0]
    gs = gs_ref[...]
    gt = gt_ref[...]
    inv_n = 1.0 / RW_HEAD
    mean = _dot_sel(_dot_sel(y, gs) * inv_n, gt)
    yc = y - mean
    var = _dot_sel(yc * yc, gs) * inv_n
    rstd = _dot_sel(lax.rsqrt(var + RW_GN_EPS), gt)
    yn = yc * rstd * gng_ref[...] + gnb_ref[...]
    return ((yn + bonus_ref[0]) * g_ref[0]).astype(BF16)


def _mixer_rwkv(hx, hc, mx, mc, mu, w_rkv, w0, w1, w2, a0, a1, a2, g1, g2, k_k, k_a, r_k, gn_g, gn_b, w_o,
                ln_g, ln_b, ctx_out):
    b, n, d = hx.shape
    gs, gt = _head_sum_matrices()
    rank_w, rank_a = w1.shape[-1], a1.shape[-1]
    assert 2 * rank_w == LANES and 2 * rank_a == LANES

    def ext(m2):
        z = jnp.zeros((2, LANES, d), F32)
        z = z.at[0, :m2.shape[1]].set(m2[0])
        return z.at[1, m2.shape[1]:].set(m2[1]).astype(BF16)

    consts = [mu, w_rkv[0].astype(BF16), w_rkv[1].astype(BF16), w_rkv[2].astype(BF16),
              g1.astype(BF16), g2.astype(BF16),
              jnp.concatenate([w1[0], w1[1]], axis=1).astype(BF16), ext(w2),
              jnp.concatenate([a1[0], a1[1]], axis=1).astype(BF16), ext(a2),
              w0, a0, k_k.reshape(1, d), k_a.reshape(1, d), r_k.reshape(1, d), gs, gt]
    w_o = w_o.astype(BF16)

    def prep(h, mod):
        r, kk, v, g, bonus, lw0, k0, b0, lw1, k1, b1 = _rw_prep(h, mod, consts)
        return r, kk, v, g, bonus, ((lw0, k0, b0), (lw1, k1, b1))

    rc, kkc, vc, gc, bonc, dc = prep(hc, mc)
    rx, kkx, vx, gx, bonx, dx = prep(hx, mx)
    s0 = jnp.zeros((b, RW_PAIRS, LANES, LANES), F32)
    ycf, ycb, scf, scb = _rw_scan(rc, kkc, vc, dc, s0, s0)
    yxf, yxb, _, _ = _rw_scan(rx, kkx, vx, dx, scf, scb)
    fin_consts = [gs, gt, gn_g.reshape(1, d), gn_b.reshape(1, d)]
    hx = _out_proj([yxf, yxb, bonx, gx], fin_consts, _rw_finish_prologue, w_o, None, hx, mx, ln_g, ln_b, "rw_out")
    if ctx_out:
        hc = _out_proj([ycf, ycb, bonc, gc], fin_consts, _rw_finish_prologue, w_o, None, hc, mc, ln_g, ln_b,
                       "rw_out_ctx")
    return hx, hc


def kernel(x, c, ctx, c_ctx, ada_w, ada_b, ln_g, ln_b, ffn_w13, ffn_w2, na_wqkv, na_wo, na_rpb, cv_w1, cv_b1, cv_wdw, cv_bdw, cv_ln_g, cv_ln_b, cv_w2, cv_b2, gla_win, gla_wa1, gla_wa2, gla_ba, gla_norm_g, gla_wo, rw_mu, rw_wrkv, rw_w0, rw_w1, rw_w2, rw_a0, rw_a1, rw_a2, rw_g1, rw_g2, rw_kk, rw_ka, rw_rk, rw_gn_g, rw_gn_b, rw_wo):
    b, n, d = x.shape
    depth = ada_w.shape[0]
    assert d == D_MODEL and depth == DEPTH and n % GRID_W == 0
    rows = -(-(b + 1) // 8) * 8
    cond = jnp.zeros((rows, d), F32).at[:b].set(c).at[b].set(c_ctx)
    mod_all = _ada(cond, ada_w, ada_b)
    w13 = ffn_w13.astype(BF16)
    w2 = ffn_w2.astype(BF16)
    hx, hc = x, ctx
    for i in range(depth):
        last = i == depth - 1
        mx = mod_all[i, :b].reshape(b, 3 * N_SUB, d)
        mc = jnp.broadcast_to(mod_all[i, b].reshape(1, 3 * N_SUB, d), (b, 3 * N_SUB, d))
        hx = _ffn(hx, mx, w13[i, 0], w2[i, 0], ln_g[i, 0], ln_b[i, 0], 0)
        hc = _ffn(hc, mc, w13[i, 0], w2[i, 0], ln_g[i, 0], ln_b[i, 0], 0)
        kind, j = i % 4, i // 4
        lg, lb = ln_g[i, 1], ln_b[i, 1]
        if kind == 0:
            hx, hc = _mixer_na(hx, hc, mx, mc, na_wqkv[j], na_wo[j], na_rpb[j], lg, lb, not last)
        elif kind == 1:
            hx, hc = _mixer_conv(hx, hc, mx, mc, cv_w1[j], cv_b1[j], cv_wdw[j], cv_bdw[j], cv_ln_g[j], cv_ln_b[j],
                                 cv_w2[j], cv_b2[j], lg, lb, not last)
        elif kind == 2:
            hx, hc = _mixer_gla(hx, hc, mx, mc, gla_win[j], gla_wa1[j], gla_wa2[j], gla_ba[j], gla_norm_g[j],
                                gla_wo[j], lg, lb, not last)
        else:
            hx, hc = _mixer_rwkv(hx, hc, mx, mc, rw_mu[j], rw_wrkv[j], rw_w0[j], rw_w1[j], rw_w2[j], rw_a0[j],
                                 rw_a1[j], rw_a2[j], rw_g1[j], rw_g2[j], rw_kk[j], rw_ka[j], rw_rk[j],
                                 rw_gn_g[j], rw_gn_b[j], rw_wo[j], lg, lb, not last)
        hx = _ffn(hx, mx, w13[i, 1], w2[i, 1], ln_g[i, 2], ln_b[i, 2], 2)
        if not last:
            hc = _ffn(hc, mc, w13[i, 1], w2[i, 1], ln_g[i, 2], ln_b[i, 2], 2)
    return hx
```

```python
import functools

import numpy as np
import jax
import jax.numpy as jnp
from jax import lax
from jax.experimental import pallas as pl
from jax.experimental.pallas import tpu as pltpu

F32 = jnp.float32
BF16 = jnp.bfloat16

D_MODEL = 1024
DEPTH = 4
N_SUB = 3
GRID_W = 64
ALPHA = (2.0 * DEPTH) ** 0.25
LN_EPS = 1e-5
D_FF = 2816
NA_HEADS = 16
NA_HEAD_DIM = D_MODEL // NA_HEADS
NA_KH = 8
NA_KW = 16
CONV_WIDTH = 31
CONV_HALO = 16
GLA_HEADS = 4
GLA_DK = 128
GLA_DV = 256
GLA_GATE_RANK = 16
GLA_NORMALIZER = 16.0
GLA_CHUNK = 64
ROPE_BASE = 10000.0
RW_HEAD = 64
RW_HEADS = D_MODEL // RW_HEAD
RW_PAIRS = RW_HEADS // 2
RW_CHUNK = 64
RW_GN_EPS = 64e-5
LANES = 128
SUBLANES = 8
NEG_BIG = -1e30
VMEM_LIMIT = 56 * 1024 * 1024


def _cparams(*sem):
    return pltpu.CompilerParams(dimension_semantics=sem, vmem_limit_bytes=VMEM_LIMIT)


def _dot(a, b):
    return jnp.dot(a.astype(BF16), b.astype(BF16), preferred_element_type=F32)


def _dot_nt(a, b):
    return lax.dot_general(a.astype(BF16), b.astype(BF16), (((1,), (1,)), ((), ())),
                           preferred_element_type=F32)


def _dot_tn(a, b):
    return lax.dot_general(a.astype(BF16), b.astype(BF16), (((0,), (0,)), ((), ())),
                           preferred_element_type=F32)


def _split(x):
    hi = x.astype(BF16)
    lo = (x - hi.astype(F32)).astype(BF16)
    return hi, lo


def _dot_sel(x, sel):
    hi, lo = _split(x)
    return (jnp.dot(hi, sel, preferred_element_type=F32)
            + jnp.dot(lo, sel, preferred_element_type=F32))


def _sel_dot(sel, x):
    hi, lo = _split(x)
    return (jnp.dot(sel, hi, preferred_element_type=F32)
            + jnp.dot(sel, lo, preferred_element_type=F32))


def _sigmoid(x):
    return jax.nn.sigmoid(x)


def _softplus(x):
    return jnp.maximum(x, 0.0) + jnp.log(1.0 + jnp.exp(-jnp.abs(x)))


def _layer_norm(x, g, b, eps=LN_EPS):
    mu = jnp.mean(x, axis=-1, keepdims=True)
    xc = x - mu
    var = jnp.mean(xc * xc, axis=-1, keepdims=True)
    return xc * lax.rsqrt(var + eps) * g + b


def _mod_rows(mod_ref, j):
    return (mod_ref[0, 3 * j:3 * j + 1, :], mod_ref[0, 3 * j + 1:3 * j + 2, :],
            mod_ref[0, 3 * j + 2:3 * j + 3, :])


def _modulate(h, mod_ref, j):
    shift, scale, _ = _mod_rows(mod_ref, j)
    return h * (1.0 + scale) + shift


def _post_norm(h, y, mod_ref, j, g_ref, b_ref):
    gate = mod_ref[0, 3 * j + 2:3 * j + 3, :]
    return _layer_norm(ALPHA * h + gate * y, g_ref[...], b_ref[...])


def _tok_spec(tm, d):
    return pl.BlockSpec((1, tm, d), lambda b, t: (b, t, 0))


def _mod_spec():
    return pl.BlockSpec((1, 3 * N_SUB, D_MODEL), lambda b, t: (b, 0, 0))


def _const_spec(shape):
    nd = len(shape)
    return pl.BlockSpec(shape, lambda b, t: (0,) * nd, pipeline_mode=pl.Buffered(1))


def _tile(s, pref):
    tm = min(pref, s)
    assert s % tm == 0
    return tm


def _ada_kernel(cond_ref, w_ref, b_ref, o_ref):
    c = cond_ref[...]
    o_ref[0] = _dot(c * _sigmoid(c), w_ref[0]) + b_ref[0]


def _ada(cond, ada_w, ada_b):
    r, d = cond.shape
    depth, _, n = ada_w.shape
    tn = 1024
    return pl.pallas_call(
        _ada_kernel,
        grid=(depth, n // tn),
        in_specs=[pl.BlockSpec((r, d), lambda l, j: (0, 0)),
                  pl.BlockSpec((1, d, tn), lambda l, j: (l, 0, j)),
                  pl.BlockSpec((1, 1, tn), lambda l, j: (l, 0, j))],
        out_specs=pl.BlockSpec((1, r, tn), lambda l, j: (l, 0, j)),
        out_shape=jax.ShapeDtypeStruct((depth, r, n), F32),
        compiler_params=_cparams("parallel", "parallel"),
        name="ada_mod",
    )(cond, ada_w, ada_b.reshape(depth, 1, n))


FFN_CHUNK = 256


def _ffn_kernel(h_ref, mod_ref, w13_ref, w2_ref, g_ref, b_ref, o_ref, *, j):
    h = h_ref[0]
    u = _modulate(h, mod_ref, j).astype(BF16)
    acc = jnp.zeros(h.shape, F32)
    for lo in range(0, D_FF, FFN_CHUNK):
        hi = min(lo + FFN_CHUNK, D_FF)
        a = jnp.dot(u, w13_ref[:, lo:hi], preferred_element_type=F32)
        v = jnp.dot(u, w13_ref[:, D_FF + lo:D_FF + hi], preferred_element_type=F32)
        z = (a * _sigmoid(a) * v).astype(BF16)
        acc = acc + jnp.dot(z, w2_ref[lo:hi, :], preferred_element_type=F32)
    o_ref[0] = _post_norm(h, 0.5 * acc, mod_ref, j, g_ref, b_ref)


def _ffn(h, mod, w13_all, w2_all, layer, half, ln_g, ln_b, j):
    b, s, d = h.shape
    tm = _tile(s, 512)
    pick = lambda shape: pl.BlockSpec((None, None) + shape[2:], lambda bi, t: (layer, half, 0, 0),
                                      pipeline_mode=pl.Buffered(1))
    return pl.pallas_call(
        functools.partial(_ffn_kernel, j=j),
        grid=(b, s // tm),
        in_specs=[_tok_spec(tm, d), _mod_spec(), pick(w13_all.shape), pick(w2_all.shape),
                  _const_spec((1, d)), _const_spec((1, d))],
        out_specs=_tok_spec(tm, d),
        out_shape=jax.ShapeDtypeStruct(h.shape, F32),
        compiler_params=_cparams("parallel", "parallel"),
        name="ffn_half",
    )(h, mod, w13_all, w2_all, ln_g.reshape(1, d), ln_b.reshape(1, d))


def _out_kernel(*refs, n_in, prologue, has_bias):
    ins = refs[:n_in]
    w_ref = refs[n_in]
    k = n_in + 1
    bias_ref = None
    if has_bias:
        bias_ref = refs[k]
        k += 1
    h_ref, mod_ref, g_ref, b_ref, o_ref = refs[k:k + 5]
    y = jnp.dot(prologue(*ins), w_ref[...], preferred_element_type=F32)
    if has_bias:
        y = y + bias_ref[...]
    o_ref[0] = _post_norm(h_ref[0], y, mod_ref, 1, g_ref, b_ref)


def _out_proj(tok_inputs, const_inputs, prologue, w, bias, h, mod, ln_g, ln_b, name):
    b, s, d = h.shape
    tm = _tile(s, 256)
    n_in = len(tok_inputs) + len(const_inputs)
    in_specs = [_tok_spec(tm, a.shape[-1]) for a in tok_inputs]
    in_specs += [_const_spec(a.shape) for a in const_inputs]
    in_specs.append(_const_spec(w.shape))
    args = list(tok_inputs) + list(const_inputs) + [w]
    if bias is not None:
        in_specs.append(_const_spec((1, d)))
        args.append(bias.reshape(1, d))
    in_specs += [_tok_spec(tm, d), _mod_spec(), _const_spec((1, d)), _const_spec((1, d))]
    args += [h, mod, ln_g.reshape(1, d), ln_b.reshape(1, d)]
    return pl.pallas_call(
        functools.partial(_out_kernel, n_in=n_in, prologue=prologue, has_bias=bias is not None),
        grid=(b, s // tm),
        in_specs=in_specs,
        out_specs=_tok_spec(tm, d),
        out_shape=jax.ShapeDtypeStruct(h.shape, F32),
        compiler_params=_cparams("parallel", "parallel"),
        name=name,
    )(*args)


def _plain_prologue(y_ref):
    return y_ref[0].astype(BF16)


def _na_qkv_kernel(h_ref, mod_ref, w_ref, q_ref, k_ref, v_ref):
    d = D_MODEL
    u = _modulate(h_ref[0], mod_ref, 1).astype(BF16)
    z = jnp.dot(u, w_ref[...], preferred_element_type=F32)
    q_ref[0] = (z[:, :d] * (NA_HEAD_DIM ** -0.5)).astype(BF16)
    k_ref[0] = z[:, d:2 * d].astype(BF16)
    v_ref[0] = z[:, 2 * d:].astype(BF16)


def _na_qkv(h, mod, w):
    b, s, d = h.shape
    tm = _tile(s, 256)
    out = jax.ShapeDtypeStruct(h.shape, BF16)
    return pl.pallas_call(
        _na_qkv_kernel,
        grid=(b, s // tm),
        in_specs=[_tok_spec(tm, d), _mod_spec(), _const_spec(w.shape)],
        out_specs=[_tok_spec(tm, d)] * 3,
        out_shape=[out] * 3,
        compiler_params=_cparams("parallel", "parallel"),
        name="na_qkv",
    )(h, mod, w)


def _na_row_start(r, rows, kh):
    return jnp.clip(r - kh // 2, 0, rows - kh)


def _na_attn_kernel(q_ref, k_ref, v_ref, kc_ref, vc_ref, bias_ref, o_ref, *, rows, kh):
    r = pl.program_id(1)
    start = pl.multiple_of(_na_row_start(r, rows, kh) * GRID_W, GRID_W)
    win = kh * GRID_W
    lane0 = lax.broadcasted_iota(jnp.int32, (1, LANES), 1) < NA_HEAD_DIM
    zero = jnp.zeros((GRID_W, LANES), BF16)
    pairs = range(NA_HEADS // 2)
    sls = [slice(p * LANES, (p + 1) * LANES) for p in pairs]
    qbd = []
    for sl in sls:
        q = q_ref[0, :, sl]
        qbd.append(jnp.concatenate([jnp.where(lane0, q, zero), jnp.where(lane0, zero, q)], axis=0))
    s_loc = [_dot_nt(k_ref[0, pl.ds(start, win), sls[p]], qbd[p]) + bias_ref[0, p] for p in pairs]
    s_ctx = [_dot_nt(kc_ref[0, :, sls[p]], qbd[p]) for p in pairs]
    m = [jnp.maximum(jnp.max(s_loc[p], axis=0, keepdims=True), jnp.max(s_ctx[p], axis=0, keepdims=True))
         for p in pairs]
    p_loc = [jnp.exp(s_loc[p] - m[p]) for p in pairs]
    p_ctx = [jnp.exp(s_ctx[p] - m[p]) for p in pairs]
    inv = [1.0 / (jnp.sum(p_loc[p], axis=0, keepdims=True) + jnp.sum(p_ctx[p], axis=0, keepdims=True))
           for p in pairs]
    o = [_dot_tn(p_loc[p] * inv[p], v_ref[0, pl.ds(start, win), sls[p]])
         + _dot_tn(p_ctx[p] * inv[p], vc_ref[0, :, sls[p]]) for p in pairs]
    for p in pairs:
        o_ref[0, :, sls[p]] = jnp.where(lane0, o[p][:GRID_W], o[p][GRID_W:]).astype(BF16)


def _na_bias_table(rpb, kh):
    w = GRID_W
    kc = np.arange(w)[:, None]
    qc = np.arange(w)[None, :]
    cstart = np.clip(qc - NA_KW // 2, 0, w - NA_KW)
    ok = (kc >= cstart) & (kc < cstart + NA_KW)
    dc = np.clip(kc - qc + NA_KW - 1, 0, 2 * NA_KW - 2)
    n_s = 2 * NA_KH - kh
    h_idx = (2 * np.arange(NA_HEADS // 2)[:, None] + np.arange(2)[None, :]).reshape(1, NA_HEADS // 2, 1, 1, 2, 1)
    r_idx = (np.arange(n_s)[:, None] + np.arange(kh)[None, :]).reshape(n_s, 1, kh, 1, 1, 1)
    c_idx = dc.reshape(1, 1, 1, w, 1, w)
    tbl = jnp.where(ok.reshape(1, 1, 1, w, 1, w), rpb[h_idx, r_idx, c_idx], NEG_BIG)
    return tbl.reshape(n_s, NA_HEADS // 2, kh * w, 2 * w)


def _na_attn(q, k, v, kc, vc, bias_tab):
    b, n, d = q.shape
    l = kc.shape[1]
    rows = n // GRID_W
    kh = min(NA_KH, rows)

    def bias_idx(bi, r):
        return (_na_row_start(r, rows, kh) - r + NA_KH - 1, 0, 0, 0)

    return pl.pallas_call(
        functools.partial(_na_attn_kernel, rows=rows, kh=kh),
        grid=(b, rows),
        in_specs=[pl.BlockSpec((1, GRID_W, d), lambda bi, r: (bi, r, 0)),
                  pl.BlockSpec((1, n, d), lambda bi, r: (bi, 0, 0)),
                  pl.BlockSpec((1, n, d), lambda bi, r: (bi, 0, 0)),
                  pl.BlockSpec((1, l, d), lambda bi, r: (bi, 0, 0)),
                  pl.BlockSpec((1, l, d), lambda bi, r: (bi, 0, 0)),
                  pl.BlockSpec((1, NA_HEADS // 2, kh * GRID_W, 2 * GRID_W), bias_idx)],
        out_specs=pl.BlockSpec((1, GRID_W, d), lambda bi, r: (bi, r, 0)),
        out_shape=jax.ShapeDtypeStruct(q.shape, BF16),
        compiler_params=_cparams("parallel", "arbitrary"),
        name="na_attn",
    )(q, k, v, kc, vc, bias_tab)


def _ctx_attn_kernel(q_ref, k_ref, v_ref, o_ref):
    for hd in range(NA_HEADS):
        sl = slice(hd * NA_HEAD_DIM, (hd + 1) * NA_HEAD_DIM)
        s = _dot_nt(q_ref[0, :, sl], k_ref[0, :, sl])
        p = jnp.exp(s - jnp.max(s, axis=-1, keepdims=True))
        den = jnp.sum(p, axis=-1, keepdims=True)
        o_ref[0, :, sl] = (_dot(p, v_ref[0, :, sl]) / den).astype(BF16)


def _ctx_attn(q, k, v):
    b, l, d = q.shape
    spec = pl.BlockSpec((1, l, d), lambda bi: (bi, 0, 0))
    return pl.pallas_call(
        _ctx_attn_kernel,
        grid=(b,),
        in_specs=[spec] * 3,
        out_specs=spec,
        out_shape=jax.ShapeDtypeStruct(q.shape, BF16),
        compiler_params=_cparams("parallel"),
        name="ctx_attn",
    )(q, k, v)


def _mixer_na(hx, hc, mx, mc, w_qkv, w_o, rpb, ln_g, ln_b, ctx_out):
    w_qkv = w_qkv.astype(BF16)
    w_o = w_o.astype(BF16)
    qx, kx, vx = _na_qkv(hx, mx, w_qkv)
    qc, kc, vc = _na_qkv(hc, mc, w_qkv)
    kh = min(NA_KH, hx.shape[1] // GRID_W)
    ox = _na_attn(qx, kx, vx, kc, vc, _na_bias_table(rpb, kh))
    hx = _out_proj([ox], [], _plain_prologue, w_o, None, hx, mx, ln_g, ln_b, "na_out")
    if ctx_out:
        oc = _ctx_attn(qc, kc, vc)
        hc = _out_proj([oc], [], _plain_prologue, w_o, None, hc, mc, ln_g, ln_b, "na_out_ctx")
    return hx, hc


def _conv_in_kernel(h_ref, mod_ref, w_ref, b_ref, o_ref):
    d = D_MODEL
    u = _modulate(h_ref[0], mod_ref, 1).astype(BF16)
    z = jnp.dot(u, w_ref[...], preferred_element_type=F32) + b_ref[...]
    o_ref[0] = z[:, :d] * _sigmoid(z[:, d:])


def _conv_in(h, mod, w1, b1):
    b, s, d = h.shape
    tm = _tile(s, 256)
    return pl.pallas_call(
        _conv_in_kernel,
        grid=(b, s // tm),
        in_specs=[_tok_spec(tm, d), _mod_spec(), _const_spec(w1.shape), _const_spec((1, 2 * d))],
        out_specs=_tok_spec(tm, d),
        out_shape=jax.ShapeDtypeStruct(h.shape, F32),
        compiler_params=_cparams("parallel", "parallel"),
        name="conv_in",
    )(h, mod, w1, b1.reshape(1, 2 * d))


CONV_ROWS = 16


def _conv_dw_kernel(prev_ref, cur_ref, next_ref, w_ref, bdw_ref, g_ref, b_ref, o_ref, win_ref, *, ts, nt):
    t = pl.program_id(1)
    halo = CONV_HALO
    d = cur_ref.shape[-1]
    rows = ts + 2 * halo
    win_ref[0, halo:halo + ts, :] = cur_ref[0]
    win_ref[0, 0:halo, :] = jnp.where(t > 0, prev_ref[0], 0.0)
    win_ref[0, halo + ts:rows, :] = jnp.where(t < nt - 1, next_ref[0], 0.0)
    sub = lax.broadcasted_iota(jnp.int32, (rows // SUBLANES, SUBLANES, LANES), 1)
    for cc in range(d // LANES):
        cols = slice(cc * LANES, (cc + 1) * LANES)
        w = win_ref[0, :, cols].reshape(rows // SUBLANES, SUBLANES, LANES)
        for i in range(1, SUBLANES):
            rot = pltpu.roll(w, SUBLANES - i, axis=1)
            nxt = jnp.concatenate([rot[1:], rot[:1]], axis=0)
            win_ref[i, :, cols] = jnp.where(sub < SUBLANES - i, rot, nxt).reshape(rows, LANES)
    off = halo - CONV_WIDTH // 2
    slabs = CONV_ROWS // SUBLANES
    for g in range(ts // CONV_ROWS):
        acc = jnp.broadcast_to(bdw_ref[...], (slabs, SUBLANES, d))
        for k in range(CONV_WIDTH):
            i, base = (off + k) % SUBLANES, g * CONV_ROWS + (off + k) // SUBLANES * SUBLANES
            acc = acc + win_ref[i, base:base + CONV_ROWS, :].reshape(slabs, SUBLANES, d) * w_ref[k][None]
        y = _layer_norm(acc.reshape(CONV_ROWS, d), g_ref[...], b_ref[...])
        o_ref[0, g * CONV_ROWS:(g + 1) * CONV_ROWS, :] = (y * _sigmoid(y)).astype(BF16)


def _conv_dw(u, w_dw, b_dw, ln_g, ln_b):
    b, s, d = u.shape
    ts = _tile(s, 128)
    nt = s // ts
    hb = ts // CONV_HALO
    nhb = s // CONV_HALO
    w_rep = jnp.broadcast_to(w_dw[:, None, :], (CONV_WIDTH, SUBLANES, d))
    return pl.pallas_call(
        functools.partial(_conv_dw_kernel, ts=ts, nt=nt),
        grid=(b, nt),
        in_specs=[pl.BlockSpec((1, CONV_HALO, d), lambda bi, t: (bi, jnp.maximum(t * hb - 1, 0), 0)),
                  _tok_spec(ts, d),
                  pl.BlockSpec((1, CONV_HALO, d), lambda bi, t: (bi, jnp.minimum((t + 1) * hb, nhb - 1), 0)),
                  _const_spec(w_rep.shape), _const_spec((1, d)), _const_spec((1, d)), _const_spec((1, d))],
        out_specs=_tok_spec(ts, d),
        out_shape=jax.ShapeDtypeStruct(u.shape, BF16),
        scratch_shapes=[pltpu.VMEM((SUBLANES, ts + 2 * CONV_HALO, d), F32)],
        compiler_params=_cparams("parallel", "arbitrary"),
        name="conv_dw",
    )(u, u, u, w_rep, b_dw.reshape(1, d), ln_g.reshape(1, d), ln_b.reshape(1, d))


def _mixer_conv(hx, hc, mx, mc, w1, b1, w_dw, b_dw, cln_g, cln_b, w2, b2, ln_g, ln_b, ctx_out):
    w1 = w1.astype(BF16)
    w2 = w2.astype(BF16)

    def branch(h, mod, name):
        u = _conv_in(h, mod, w1, b1)
        y = _conv_dw(u, w_dw, b_dw, cln_g, cln_b)
        return _out_proj([y], [], _plain_prologue, w2, b2, h, mod, ln_g, ln_b, name)

    hx = branch(hx, mx, "conv_out")
    if ctx_out:
        hc = branch(hc, mc, "conv_out_ctx")
    return hx, hc


GLA_NQK = GLA_HEADS * GLA_DK


def _rope_swap(x):
    quarter = GLA_DK // 4
    lane = lax.broadcasted_iota(jnp.int32, (1, GLA_DK), 1)
    first = (lane % (2 * quarter)) < quarter
    parts = []
    for hd in range(GLA_HEADS):
        xh = x[:, hd * GLA_DK:(hd + 1) * GLA_DK]
        up = pltpu.roll(xh, GLA_DK - quarter, axis=1)
        dn = pltpu.roll(xh, quarter, axis=1)
        parts.append(jnp.where(first, up, dn))
    return jnp.concatenate(parts, axis=1)


def _gla_in_kernel(*refs, rotary):
    if rotary:
        h_ref, mod_ref, w_ref, wa2_ref, ba_ref, cos_ref, sin_ref = refs[:7]
        outs = refs[7:]
    else:
        h_ref, mod_ref, w_ref, wa2_ref, ba_ref = refs[:5]
        outs = refs[5:]
    q_ref, k_ref, v_ref, g_ref, laf_ref, lab_ref = outs
    n, d = GLA_NQK, D_MODEL
    u = _modulate(h_ref[0], mod_ref, 1).astype(BF16)
    z = jnp.dot(u, w_ref[...], preferred_element_type=F32)
    q = z[:, :n]
    k = z[:, n:2 * n]
    o = 2 * n
    if rotary:
        cos = cos_ref[...]
        sin = sin_ref[...]
        q = q * cos + _rope_swap(q) * sin
        k = k * cos + _rope_swap(k) * sin
    q_ref[0] = q * (GLA_DK ** -0.5)
    k_ref[0] = k
    v_ref[0] = z[:, o:o + d].astype(BF16)
    g_ref[0] = z[:, o + d:o + 2 * d]
    t = z[:, o + 2 * d:o + 2 * d + LANES].astype(BF16)
    for dr, la_ref in enumerate((laf_ref, lab_ref)):
        x = jnp.dot(t, wa2_ref[dr], preferred_element_type=F32) + ba_ref[dr:dr + 1, :]
        la_ref[0] = -_softplus(-x) * (1.0 / GLA_NORMALIZER)


def _gla_rope_tables(n):
    quarter = GLA_DK // 4
    t = np.arange(n)
    freqs = ROPE_BASE ** (-jnp.arange(quarter, dtype=F32) / quarter)
    ang_r = jnp.asarray(t // GRID_W, F32)[:, None] * freqs
    ang_c = jnp.asarray(t % GRID_W, F32)[:, None] * freqs
    cos = jnp.concatenate([jnp.cos(ang_r), jnp.cos(ang_r), jnp.cos(ang_c), jnp.cos(ang_c)], axis=-1)
    sin = jnp.concatenate([-jnp.sin(ang_r), jnp.sin(ang_r), -jnp.sin(ang_c), jnp.sin(ang_c)], axis=-1)
    return jnp.tile(cos, (1, GLA_HEADS)), jnp.tile(sin, (1, GLA_HEADS))


def _gla_in(h, mod, w_ext, wa2_ext, ba, rope):
    b, s, d = h.shape
    tm = _tile(s, 256)
    rotary = rope is not None
    in_specs = [_tok_spec(tm, d), _mod_spec(), _const_spec(w_ext.shape), _const_spec(wa2_ext.shape),
                _const_spec(ba.shape)]
    args = [h, mod, w_ext, wa2_ext, ba]
    if rotary:
        in_specs += [pl.BlockSpec((tm, GLA_NQK), lambda bi, t: (t, 0))] * 2
        args += list(rope)
    shp = lambda w, dt: jax.ShapeDtypeStruct((b, s, w), dt)
    return pl.pallas_call(
        functools.partial(_gla_in_kernel, rotary=rotary),
        grid=(b, s // tm),
        in_specs=in_specs,
        out_specs=[_tok_spec(tm, GLA_NQK), _tok_spec(tm, GLA_NQK), _tok_spec(tm, d), _tok_spec(tm, d),
                   _tok_spec(tm, GLA_NQK), _tok_spec(tm, GLA_NQK)],
        out_shape=[shp(GLA_NQK, F32), shp(GLA_NQK, F32), shp(d, BF16), shp(d, F32),
                   shp(GLA_NQK, F32), shp(GLA_NQK, F32)],
        compiler_params=_cparams("parallel", "parallel"),
        name="gla_in_rope" if rotary else "gla_in",
    )(*args)


def _tri_masks(c):
    row = lax.broadcasted_iota(jnp.int32, (c, c), 0)
    col = lax.broadcasted_iota(jnp.int32, (c, c), 1)
    return row >= col, row <= col


def _gla_core_kernel(qf_ref, kf_ref, vf_ref, laf_ref, qb_ref, kb_ref, vb_ref, lab_ref, s0f_ref, s0b_ref,
                     of_ref, ob_ref, sf_ref, sb_ref, stf_ref, stb_ref, *, nchunks):
    i = pl.program_id(1)
    c = GLA_CHUNK

    @pl.when(i == 0)
    def _():
        stf_ref[...] = s0f_ref[0]
        stb_ref[...] = s0b_ref[0]

    lower, upper = _tri_masks(c)
    dirs = ((qf_ref, kf_ref, vf_ref, laf_ref, of_ref, stf_ref, lower, c - 1),
            (qb_ref, kb_ref, vb_ref, lab_ref, ob_ref, stb_ref, upper, 0))
    insts = []
    for q_ref, k_ref, v_ref, la_ref, o_ref, st_ref, mask, last in dirs:
        tri = mask.astype(BF16)
        hi, lo = _split(la_ref[0])
        bcum = jnp.dot(tri, hi, preferred_element_type=F32) + jnp.dot(tri, lo, preferred_element_type=F32)
        blast = bcum[last:last + 1, :]
        qd = (q_ref[0] * jnp.exp(bcum)).astype(BF16)
        k = k_ref[0]
        kd = (k * jnp.exp(-bcum)).astype(BF16)
        kt = (k * jnp.exp(blast - bcum)).astype(BF16)
        for hd in range(GLA_HEADS):
            ks = slice(hd * GLA_DK, (hd + 1) * GLA_DK)
            vs = slice(hd * GLA_DV, (hd + 1) * GLA_DV)
            insts.append(dict(qd=qd[:, ks], kd=kd[:, ks], kt=kt[:, ks], dec=jnp.exp(blast[:, ks]),
                              v=v_ref[0, :, vs], mask=mask, o_ref=o_ref, st_ref=st_ref, hd=hd, vs=vs))
    att = [jnp.where(t["mask"], _dot_nt(t["qd"], t["kd"]), 0.0) for t in insts]
    s_in = [t["st_ref"][t["hd"]] for t in insts]
    out = [_dot(a, t["v"]) + _dot_nt(t["qd"], s) for a, t, s in zip(att, insts, s_in)]
    for t, o in zip(insts, out):
        t["o_ref"][0, :, t["vs"]] = o
    upd = [_dot_tn(t["v"], t["kt"]) for t in insts]
    for t, s, u in zip(insts, s_in, upd):
        t["st_ref"][t["hd"]] = t["dec"] * s + u

    @pl.when(i == nchunks - 1)
    def _():
        sf_ref[0] = stf_ref[...]
        sb_ref[0] = stb_ref[...]


def _gla_core(q, k, v, la_f, la_b, s0f, s0b):
    b, s, _ = q.shape
    c = GLA_CHUNK
    nchunks = s // c
    fwd = lambda w: pl.BlockSpec((1, c, w), lambda bi, i: (bi, i, 0))
    bwd = lambda w: pl.BlockSpec((1, c, w), lambda bi, i: (bi, nchunks - 1 - i, 0))
    st_shape = (GLA_HEADS, GLA_DV, GLA_DK)
    st_spec = pl.BlockSpec((1,) + st_shape, lambda bi, i: (bi, 0, 0, 0))
    return pl.pallas_call(
        functools.partial(_gla_core_kernel, nchunks=nchunks),
        grid=(b, nchunks),
        in_specs=[fwd(GLA_NQK), fwd(GLA_NQK), fwd(D_MODEL), fwd(GLA_NQK),
                  bwd(GLA_NQK), bwd(GLA_NQK), bwd(D_MODEL), bwd(GLA_NQK), st_spec, st_spec],
        out_specs=[fwd(D_MODEL), bwd(D_MODEL), st_spec, st_spec],
        out_shape=[jax.ShapeDtypeStruct((b, s, D_MODEL), F32)] * 2
        + [jax.ShapeDtypeStruct((b,) + st_shape, F32)] * 2,
        scratch_shapes=[pltpu.VMEM(st_shape, F32), pltpu.VMEM(st_shape, F32)],
        compiler_params=_cparams("parallel", "arbitrary"),
        name="gla_core",
    )(q, k, v, la_f, q, k, v, la_b, s0f, s0b)


def _gla_finish_prologue(of_ref, ob_ref, g_ref, ng_ref):
    o = of_ref[0] + ob_ref[0]
    g = g_ref[0]
    ng = ng_ref[...]
    parts = []
    for hd in range(GLA_HEADS):
        vs = slice(hd * GLA_DV, (hd + 1) * GLA_DV)
        oh = o[:, vs]
        gh = g[:, vs]
        oh = oh * lax.rsqrt(jnp.mean(oh * oh, axis=-1, keepdims=True) + LN_EPS) * ng
        parts.append((oh * (gh * _sigmoid(gh))).astype(BF16))
    return jnp.concatenate(parts, axis=1)


def _mixer_gla(hx, hc, mx, mc, w_in, w_a1, w_a2, b_a, norm_g, w_o, ln_g, ln_b, ctx_out):
    b, n, d = hx.shape
    nqk = GLA_NQK
    pad = jnp.zeros((d, LANES - 2 * GLA_GATE_RANK), F32)
    w_ext = jnp.concatenate([w_in, w_a1[0], w_a1[1], pad], axis=1).astype(BF16)
    wa2_ext = jnp.zeros((2, LANES, nqk), F32)
    wa2_ext = wa2_ext.at[0, :GLA_GATE_RANK].set(w_a2[0])
    wa2_ext = wa2_ext.at[1, GLA_GATE_RANK:2 * GLA_GATE_RANK].set(w_a2[1]).astype(BF16)
    w_o = w_o.astype(BF16)

    qc, kc, vc, gc, lcf, lcb = _gla_in(hc, mc, w_ext, wa2_ext, b_a, None)
    qx, kx, vx, gx, lxf, lxb = _gla_in(hx, mx, w_ext, wa2_ext, b_a, _gla_rope_tables(n))
    s0 = jnp.zeros((b, GLA_HEADS, GLA_DV, GLA_DK), F32)
    ocf, ocb, scf, scb = _gla_core(qc, kc, vc, lcf, lcb, s0, s0)
    oxf, oxb, _, _ = _gla_core(qx, kx, vx, lxf, lxb, scf, scb)
    ng = norm_g.reshape(1, GLA_DV)
    hx = _out_proj([oxf, oxb, gx], [ng], _gla_finish_prologue, w_o, None, hx, mx, ln_g, ln_b, "gla_out")
    if ctx_out:
        hc = _out_proj([ocf, ocb, gc], [ng], _gla_finish_prologue, w_o, None, hc, mc, ln_g, ln_b, "gla_out_ctx")
    return hx, hc


def _head_sum_matrices():
    g = np.zeros((D_MODEL, LANES), np.float32)
    g[np.arange(D_MODEL), np.arange(D_MODEL) // RW_HEAD] = 1.0
    return jnp.asarray(g, BF16), jnp.asarray(g.T.copy(), BF16)


def _rw_prep_kernel(hp_ref, h_ref, hn_ref, mod_ref, mu_ref, wr_ref, wk_ref, wv_ref, g1_ref, g2_ref,
                    w1_ref, w2_ref, a1_ref, a2_ref, w0_ref, a0_ref, kkw_ref, kaw_ref, rkw_ref, gs_ref, gt_ref,
                    r_ref, kk_ref, v_ref, g_ref, bonus_ref,
                    lw0_ref, k0_ref, b0_ref, lw1_ref, k1_ref, b1_ref, *, tm, nt):
    t = pl.program_id(1)
    u = _modulate(h_ref[0], mod_ref, 1)
    up = _modulate(hp_ref[0, 7:8, :], mod_ref, 1)
    un = _modulate(hn_ref[0, 0:1, :], mod_ref, 1)
    up = jnp.where(t > 0, up, 0.0)
    un = jnp.where(t < nt - 1, un, 0.0)
    rowid = lax.broadcasted_iota(jnp.int32, (tm, 1), 0)
    prev = jnp.where(rowid == 0, up, pltpu.roll(u, 1, axis=0))
    nxt = jnp.where(rowid == tm - 1, un, pltpu.roll(u, tm - 1, axis=0))
    xx = 0.5 * (prev + nxt) - u
    mix = lambda j: (u + xx * mu_ref[j:j + 1, :]).astype(BF16)
    gs = gs_ref[...]
    gt = gt_ref[...]

    r = jnp.dot(mix(0), wr_ref[...], preferred_element_type=F32)
    k = jnp.dot(mix(2), wk_ref[...], preferred_element_type=F32)
    v = jnp.dot(mix(3), wv_ref[...], preferred_element_type=F32)
    hw = jnp.tanh(jnp.dot(mix(1), w1_ref[...], preferred_element_type=F32)).astype(BF16)
    ha = jnp.dot(mix(4), a1_ref[...], preferred_element_type=F32).astype(BF16)
    gg = _sigmoid(jnp.dot(mix(5), g1_ref[...], preferred_element_type=F32)).astype(BF16)
    g_ref[0] = jnp.dot(gg, g2_ref[...], preferred_element_type=F32)

    kk = k * kkw_ref[...]
    nrm = jnp.maximum(jnp.sqrt(_dot_sel(kk * kk, gs)), 1e-12)
    kk = kk * _dot_sel(1.0 / nrm, gt)
    ksum = jnp.zeros_like(k)
    for dr, (lw_ref, kd_ref, bd_ref) in enumerate(((lw0_ref, k0_ref, b0_ref), (lw1_ref, k1_ref, b1_ref))):
        xw = w0_ref[dr:dr + 1, :] + jnp.dot(hw, w2_ref[dr], preferred_element_type=F32)
        w_log = -_softplus(-xw) - 0.5
        lw_ref[0] = -jnp.exp(w_log)
        a = _sigmoid(a0_ref[dr:dr + 1, :] + jnp.dot(ha, a2_ref[dr], preferred_element_type=F32))
        kd = k * (1.0 + (a - 1.0) * kaw_ref[...])
        kd_ref[0] = kd
        bd_ref[0] = kk * a
        ksum = ksum + kd
    r_ref[0] = r
    kk_ref[0] = kk
    v_ref[0] = v.astype(BF16)
    bonus_ref[0] = _dot_sel(_dot_sel(r * rkw_ref[...] * ksum, gs), gt) * v


def _rw_prep(h, mod, consts):
    b, s, d = h.shape
    tm = _tile(s, 256)
    nt = s // tm
    hb = tm // 8
    nhb = s // 8
    in_specs = [pl.BlockSpec((1, 8, d), lambda bi, t: (bi, jnp.maximum(t * hb - 1, 0), 0)),
                _tok_spec(tm, d),
                pl.BlockSpec((1, 8, d), lambda bi, t: (bi, jnp.minimum((t + 1) * hb, nhb - 1), 0)),
                _mod_spec()] + [_const_spec(a.shape) for a in consts]
    f32o = jax.ShapeDtypeStruct(h.shape, F32)
    out_shape = [f32o, f32o, jax.ShapeDtypeStruct(h.shape, BF16), f32o, f32o] + [f32o] * 6
    return pl.pallas_call(
        functools.partial(_rw_prep_kernel, tm=tm, nt=nt),
        grid=(b, nt),
        in_specs=in_specs,
        out_specs=[_tok_spec(tm, d)] * 11,
        out_shape=out_shape,
        compiler_params=_cparams("parallel", "arbitrary"),
        name="rw_prep",
    )(h, h, h, mod, *consts)


def _rw_prelude(refs, forward):
    r_ref, kk_ref, v_ref, lw_ref, k_ref, b_ref = refs
    c = RW_CHUNK
    row = lax.broadcasted_iota(jnp.int32, (c, 4 * c), 0)
    col = lax.broadcasted_iota(jnp.int32, (c, 4 * c), 1) % c
    lower, upper = _tri_masks(c)
    if forward:
        strict, incl, tri, last = row > col, row >= col, lower, c - 1
    else:
        strict, incl, tri, last = row < col, row <= col, upper, 0
    tri = tri.astype(BF16)
    lw = lw_ref[0]
    hi, lo = _split(lw)
    cum = jnp.dot(tri, hi, preferred_element_type=F32) + jnp.dot(tri, lo, preferred_element_type=F32)
    tot = cum[last:last + 1, :]
    e_neg = jnp.exp(-cum)
    e_tail = jnp.exp(tot - cum)
    b_raw = b_ref[0]
    k_raw = k_ref[0]
    return dict(
        strict=strict, incl=incl, dec=jnp.exp(tot),
        a=(kk_ref[0] * jnp.exp(cum - lw)).astype(BF16), r=(r_ref[0] * jnp.exp(cum)).astype(BF16),
        bt=(b_raw * e_neg).astype(BF16), kt=(k_raw * e_neg).astype(BF16),
        bh=(b_raw * e_tail).astype(BF16), kh=(k_raw * e_tail).astype(BF16), v=v_ref[0])


def _rw_scan_chunk(pre_f, pre_b, stf_ref, stb_ref, yf_ref, yb_ref):
    c = RW_CHUNK
    rowp = lax.broadcasted_iota(jnp.int32, (c, 2 * c), 0)
    colp = lax.broadcasted_iota(jnp.int32, (c, 2 * c), 1) % c
    eye = (rowp == colp).astype(F32)
    blocks = [(rowp // s) == (colp // s) for s in (2 ** e for e in range(1, int(np.log2(c)) + 1))]
    row2 = lax.broadcasted_iota(jnp.int32, (LANES, LANES), 0)
    col2 = lax.broadcasted_iota(jnp.int32, (LANES, LANES), 1)
    same_head = (row2 < RW_HEAD) == (col2 < RW_HEAD)
    lane0 = lax.broadcasted_iota(jnp.int32, (1, LANES), 1) < RW_HEAD
    zero = jnp.zeros((c, LANES), BF16)

    def bdiag(x):
        x = x.astype(BF16)
        return jnp.concatenate([jnp.where(lane0, x, zero), jnp.where(lane0, zero, x)], axis=0)

    insts = [(pre, st_ref, y_ref, slice(p * LANES, (p + 1) * LANES), p)
             for pre, st_ref, y_ref in ((pre_f, stf_ref, yf_ref), (pre_b, stb_ref, yb_ref))
             for p in range(RW_PAIRS)]
    n = len(insts)
    cut = lambda name: [pre[name][:, sl] for pre, _, _, sl, _ in insts]
    at, rt, bt, kt, bh, kh, v = (cut(nm) for nm in ("a", "r", "bt", "kt", "bh", "kh", "v"))
    ar = [jnp.concatenate([at[i], rt[i]], axis=0) for i in range(n)]
    gram = [_dot_nt(ar[i], jnp.concatenate([bdiag(bt[i]), bdiag(kt[i])], axis=0)) for i in range(n)]
    n1 = [jnp.where(insts[i][0]["strict"][:, :2 * c], gram[i][:c, :2 * c], 0.0) for i in range(n)]
    a_ak = [jnp.where(insts[i][0]["strict"][:, :2 * c], gram[i][:c, 2 * c:], 0.0) for i in range(n)]
    r_bk = [jnp.where(insts[i][0]["incl"], gram[i][c:], 0.0) for i in range(n)]
    zt = [st_ref[p] for _, st_ref, _, _, p in insts]
    sz = [_dot_nt(ar[i], zt[i]) for i in range(n)]
    w = [sz[i][:c] + _dot(a_ak[i], bdiag(v[i])) for i in range(n)]
    inv = [eye - jnp.where(blocks[0], n1[i], 0.0) for i in range(n)]
    for lvl in range(1, len(blocks)):
        sel = blocks[lvl] & ~blocks[lvl - 1]
        tmp = [_dot(jnp.where(sel, n1[i], 0.0), bdiag(inv[i])) for i in range(n)]
        inv = [inv[i] - _dot(inv[i], bdiag(tmp[i])) for i in range(n)]
    u = [-_dot(inv[i], bdiag(w[i])) for i in range(n)]
    ys = [_dot(r_bk[i], jnp.concatenate([bdiag(u[i]), bdiag(v[i])], axis=0)) for i in range(n)]
    for i, (_, _, y_ref, sl, _) in enumerate(insts):
        y_ref[0, :, sl] = sz[i][c:] + ys[i]
    upd = [_dot_tn(jnp.concatenate([u[i].astype(BF16), v[i]], axis=0), jnp.concatenate([bh[i], kh[i]], axis=0))
           for i in range(n)]
    for i, (pre, st_ref, _, sl, p) in enumerate(insts):
        st_ref[p] = pre["dec"][:, sl] * zt[i] + jnp.where(same_head, upd[i], 0.0)


def _rw_scan_kernel(*refs, nchunks):
    fwd_in, bwd_in = refs[0:6], refs[6:12]
    s0f_ref, s0b_ref, yf_ref, yb_ref, sf_ref, sb_ref, stf_ref, stb_ref = refs[12:]
    i = pl.program_id(1)

    @pl.when(i == 0)
    def _():
        stf_ref[...] = s0f_ref[0]
        stb_ref[...] = s0b_ref[0]

    _rw_scan_chunk(_rw_prelude(fwd_in, True), _rw_prelude(bwd_in, False), stf_ref, stb_ref, yf_ref, yb_ref)

    @pl.when(i == nchunks - 1)
    def _():
        sf_ref[0] = stf_ref[...]
        sb_ref[0] = stb_ref[...]


def _rw_scan(r, kk, v, dirs, s0f, s0b):
    b, s, d = r.shape
    c = RW_CHUNK
    nchunks = s // c
    fwd = pl.BlockSpec((1, c, d), lambda bi, i: (bi, i, 0))
    bwd = pl.BlockSpec((1, c, d), lambda bi, i: (bi, nchunks - 1 - i, 0))
    st_shape = (RW_PAIRS, LANES, LANES)
    st_spec = pl.BlockSpec((1,) + st_shape, lambda bi, i: (bi, 0, 0, 0))
    (lw0, k0, b0), (lw1, k1, b1) = dirs
    return pl.pallas_call(
        functools.partial(_rw_scan_kernel, nchunks=nchunks),
        grid=(b, nchunks),
        in_specs=[fwd] * 6 + [bwd] * 6 + [st_spec, st_spec],
        out_specs=[fwd, bwd, st_spec, st_spec],
        out_shape=[jax.ShapeDtypeStruct((b, s, d), F32)] * 2 + [jax.ShapeDtypeStruct((b,) + st_shape, F32)] * 2,
        scratch_shapes=[pltpu.VMEM(st_shape, F32), pltpu.VMEM(st_shape, F32)],
        compiler_params=_cparams("parallel", "arbitrary"),
        name="rw_scan",
    )(r, kk, v, lw0, k0, b0, r, kk, v, lw1, k1, b1, s0f, s0b)


def _rw_finish_prologue(yf_ref, yb_ref, bonus_ref, g_ref, gs_ref, gt_ref, gng_ref, gnb_ref):
    y = yf_ref[0] + yb_ref[0]
    gs = gs_ref[...]
    gt = gt_ref[...]
    inv_n = 1.0 / RW_HEAD
    mean = _dot_sel(_dot_sel(y, gs) * inv_n, gt)
    yc = y - mean
    var = _dot_sel(yc * yc, gs) * inv_n
    rstd = _dot_sel(lax.rsqrt(var + RW_GN_EPS), gt)
    yn = yc * rstd * gng_ref[...] + gnb_ref[...]
    return ((yn + bonus_ref[0]) * g_ref[0]).astype(BF16)


def _mixer_rwkv(hx, hc, mx, mc, mu, w_rkv, w0, w1, w2, a0, a1, a2, g1, g2, k_k, k_a, r_k, gn_g, gn_b, w_o,
                ln_g, ln_b, ctx_out):
    b, n, d = hx.shape
    gs, gt = _head_sum_matrices()
    rank_w, rank_a = w1.shape[-1], a1.shape[-1]
    assert 2 * rank_w == LANES and 2 * rank_a == LANES

    def ext(m2):
        z = jnp.zeros((2, LANES, d), F32)
        z = z.at[0, :m2.shape[1]].set(m2[0])
        return z.at[1, m2.shape[1]:].set(m2[1]).astype(BF16)

    consts = [mu, w_rkv[0].astype(BF16), w_rkv[1].astype(BF16), w_rkv[2].astype(BF16),
              g1.astype(BF16), g2.astype(BF16),
              jnp.concatenate([w1[0], w1[1]], axis=1).astype(BF16), ext(w2),
              jnp.concatenate([a1[0], a1[1]], axis=1).astype(BF16), ext(a2),
              w0, a0, k_k.reshape(1, d), k_a.reshape(1, d), r_k.reshape(1, d), gs, gt]
    w_o = w_o.astype(BF16)

    def prep(h, mod):
        r, kk, v, g, bonus, lw0, k0, b0, lw1, k1, b1 = _rw_prep(h, mod, consts)
        return r, kk, v, g, bonus, ((lw0, k0, b0), (lw1, k1, b1))

    rc, kkc, vc, gc, bonc, dc = prep(hc, mc)
    rx, kkx, vx, gx, bonx, dx = prep(hx, mx)
    s0 = jnp.zeros((b, RW_PAIRS, LANES, LANES), F32)
    ycf, ycb, scf, scb = _rw_scan(rc, kkc, vc, dc, s0, s0)
    yxf, yxb, _, _ = _rw_scan(rx, kkx, vx, dx, scf, scb)
    fin_consts = [gs, gt, gn_g.reshape(1, d), gn_b.reshape(1, d)]
    hx = _out_proj([yxf, yxb, bonx, gx], fin_consts, _rw_finish_prologue, w_o, None, hx, mx, ln_g, ln_b, "rw_out")
    if ctx_out:
        hc = _out_proj([ycf, ycb, bonc, gc], fin_consts, _rw_finish_prologue, w_o, None, hc, mc, ln_g, ln_b,
                       "rw_out_ctx")
    return hx, hc


def kernel(x, c, ctx, c_ctx, ada_w, ada_b, ln_g, ln_b, ffn_w13, ffn_w2, na_wqkv, na_wo, na_rpb, cv_w1, cv_b1, cv_wdw, cv_bdw, cv_ln_g, cv_ln_b, cv_w2, cv_b2, gla_win, gla_wa1, gla_wa2, gla_ba, gla_norm_g, gla_wo, rw_mu, rw_wrkv, rw_w0, rw_w1, rw_w2, rw_a0, rw_a1, rw_a2, rw_g1, rw_g2, rw_kk, rw_ka, rw_rk, rw_gn_g, rw_gn_b, rw_wo):
    b, n, d = x.shape
    depth = ada_w.shape[0]
    assert d == D_MODEL and depth == DEPTH and n % GRID_W == 0
    rows = -(-(b + 1) // 8) * 8
    cond = jnp.zeros((rows, d), F32).at[:b].set(c).at[b].set(c_ctx)
    mod_all = _ada(cond, ada_w, ada_b)
    w13 = ffn_w13.astype(BF16)
    w2 = ffn_w2.astype(BF16)
    hx, hc = x, ctx
    for i in range(depth):
        last = i == depth - 1
        mx = mod_all[i, :b].reshape(b, 3 * N_SUB, d)
        mc = jnp.broadcast_to(mod_all[i, b].reshape(1, 3 * N_SUB, d), (b, 3 * N_SUB, d))
        hx = _ffn(hx, mx, w13, w2, i, 0, ln_g[i, 0], ln_b[i, 0], 0)
        hc = _ffn(hc, mc, w13, w2, i, 0, ln_g[i, 0], ln_b[i, 0], 0)
        kind, j = i % 4, i // 4
        lg, lb = ln_g[i, 1], ln_b[i, 1]
        if kind == 0:
            hx, hc = _mixer_na(hx, hc, mx, mc, na_wqkv[j], na_wo[j], na_rpb[j], lg, lb, not last)
        elif kind == 1:
            hx, hc = _mixer_conv(hx, hc, mx, mc, cv_w1[j], cv_b1[j], cv_wdw[j], cv_bdw[j], cv_ln_g[j], cv_ln_b[j],
                                 cv_w2[j], cv_b2[j], lg, lb, not last)
        elif kind == 2:
            hx, hc = _mixer_gla(hx, hc, mx, mc, gla_win[j], gla_wa1[j], gla_wa2[j], gla_ba[j], gla_norm_g[j],
                                gla_wo[j], lg, lb, not last)
        else:
            hx, hc = _mixer_rwkv(hx, hc, mx, mc, rw_mu[j], rw_wrkv[j], rw_w0[j], rw_w1[j], rw_w2[j], rw_a0[j],
                                 rw_a1[j], rw_a2[j], rw_g1[j], rw_g2[j], rw_kk[j], rw_ka[j], rw_rk[j],
                                 rw_gn_g[j], rw_gn_b[j], rw_wo[j], lg, lb, not last)
        hx = _ffn(hx, mx, w13, w2, i, 1, ln_g[i, 2], ln_b[i, 2], 2)
        if not last:
            hc = _ffn(hc, mc, w13, w2, i, 1, ln_g[i, 2], ln_b[i, 2], 2)
    return hx
```

```python
import functools

import numpy as np
import jax
import jax.numpy as jnp
from jax import lax
from jax.experimental import pallas as pl
from jax.experimental.pallas import tpu as pltpu

F32 = jnp.float32
BF16 = jnp.bfloat16

D_MODEL = 1024
DEPTH = 4
N_SUB = 3
GRID_W = 64
ALPHA = (2.0 * DEPTH) ** 0.25
LN_EPS = 1e-5
D_FF = 2816
NA_HEADS = 16
NA_HEAD_DIM = D_MODEL // NA_HEADS
NA_KH = 8
NA_KW = 16
CONV_WIDTH = 31
CONV_HALO = 16
GLA_HEADS = 4
GLA_DK = 128
GLA_DV = 256
GLA_GATE_RANK = 16
GLA_NORMALIZER = 16.0
GLA_CHUNK = 64
GLA_CHUNKS_PER_STEP = 2
ROPE_BASE = 10000.0
RW_HEAD = 64
RW_HEADS = D_MODEL // RW_HEAD
RW_PAIRS = RW_HEADS // 2
RW_CHUNK = 64
RW_CHUNKS_PER_STEP = 1
RW_GN_EPS = 64e-5
LANES = 128
SUBLANES = 8
NEG_BIG = -1e30
VMEM_LIMIT = 56 * 1024 * 1024


def _cparams(*sem):
    return pltpu.CompilerParams(dimension_semantics=sem, vmem_limit_bytes=VMEM_LIMIT)


def _dot(a, b):
    return jnp.dot(a.astype(BF16), b.astype(BF16), preferred_element_type=F32)


def _dot_nt(a, b):
    return lax.dot_general(a.astype(BF16), b.astype(BF16), (((1,), (1,)), ((), ())),
                           preferred_element_type=F32)


def _dot_tn(a, b):
    return lax.dot_general(a.astype(BF16), b.astype(BF16), (((0,), (0,)), ((), ())),
                           preferred_element_type=F32)


def _split(x):
    hi = x.astype(BF16)
    lo = (x - hi.astype(F32)).astype(BF16)
    return hi, lo


def _dot_sel(x, sel):
    hi, lo = _split(x)
    return (jnp.dot(hi, sel, preferred_element_type=F32)
            + jnp.dot(lo, sel, preferred_element_type=F32))


def _sel_dot(sel, x):
    hi, lo = _split(x)
    return (jnp.dot(sel, hi, preferred_element_type=F32)
            + jnp.dot(sel, lo, preferred_element_type=F32))


def _sigmoid(x):
    return jax.nn.sigmoid(x)


def _softplus(x):
    return jnp.maximum(x, 0.0) + jnp.log(1.0 + jnp.exp(-jnp.abs(x)))


def _layer_norm(x, g, b, eps=LN_EPS):
    mu = jnp.mean(x, axis=-1, keepdims=True)
    xc = x - mu
    var = jnp.mean(xc * xc, axis=-1, keepdims=True)
    return xc * lax.rsqrt(var + eps) * g + b


def _mod_rows(mod_ref, j):
    return (mod_ref[0, 3 * j:3 * j + 1, :], mod_ref[0, 3 * j + 1:3 * j + 2, :],
            mod_ref[0, 3 * j + 2:3 * j + 3, :])


def _modulate(h, mod_ref, j):
    shift, scale, _ = _mod_rows(mod_ref, j)
    return h * (1.0 + scale) + shift


def _post_norm(h, y, mod_ref, j, g_ref, b_ref):
    gate = mod_ref[0, 3 * j + 2:3 * j + 3, :]
    return _layer_norm(ALPHA * h + gate * y, g_ref[...], b_ref[...])


def _tok_spec(tm, d):
    return pl.BlockSpec((1, tm, d), lambda b, t: (b, t, 0))


def _mod_spec():
    return pl.BlockSpec((1, 3 * N_SUB, D_MODEL), lambda b, t: (b, 0, 0))


def _const_spec(shape):
    nd = len(shape)
    return pl.BlockSpec(shape, lambda b, t: (0,) * nd, pipeline_mode=pl.Buffered(1))


def _tile(s, pref):
    tm = min(pref, s)
    assert s % tm == 0
    return tm


def _ada_kernel(cond_ref, w_ref, b_ref, o_ref):
    c = cond_ref[...]
    o_ref[0] = _dot(c * _sigmoid(c), w_ref[0]) + b_ref[0]


def _ada(cond, ada_w, ada_b):
    r, d = cond.shape
    depth, _, n = ada_w.shape
    tn = 1024
    return pl.pallas_call(
        _ada_kernel,
        grid=(depth, n // tn),
        in_specs=[pl.BlockSpec((r, d), lambda l, j: (0, 0)),
                  pl.BlockSpec((1, d, tn), lambda l, j: (l, 0, j)),
                  pl.BlockSpec((1, 1, tn), lambda l, j: (l, 0, j))],
        out_specs=pl.BlockSpec((1, r, tn), lambda l, j: (l, 0, j)),
        out_shape=jax.ShapeDtypeStruct((depth, r, n), F32),
        compiler_params=_cparams("parallel", "parallel"),
        name="ada_mod",
    )(cond, ada_w, ada_b.reshape(depth, 1, n))


FFN_CHUNK = 256


def _ffn_kernel(h_ref, mod_ref, w13_ref, w2_ref, g_ref, b_ref, o_ref, *, j):
    h = h_ref[0]
    u = _modulate(h, mod_ref, j).astype(BF16)
    acc = jnp.zeros(h.shape, F32)
    for lo in range(0, D_FF, FFN_CHUNK):
        hi = min(lo + FFN_CHUNK, D_FF)
        a = jnp.dot(u, w13_ref[:, lo:hi], preferred_element_type=F32)
        v = jnp.dot(u, w13_ref[:, D_FF + lo:D_FF + hi], preferred_element_type=F32)
        z = (a * _sigmoid(a) * v).astype(BF16)
        acc = acc + jnp.dot(z, w2_ref[lo:hi, :], preferred_element_type=F32)
    o_ref[0] = _post_norm(h, 0.5 * acc, mod_ref, j, g_ref, b_ref)


def _ffn(h, mod, w13_all, w2_all, layer, half, ln_g, ln_b, j):
    b, s, d = h.shape
    tm = _tile(s, 512)
    pick = lambda shape: pl.BlockSpec((None, None) + shape[2:], lambda bi, t: (layer, half, 0, 0),
                                      pipeline_mode=pl.Buffered(1))
    return pl.pallas_call(
        functools.partial(_ffn_kernel, j=j),
        grid=(b, s // tm),
        in_specs=[_tok_spec(tm, d), _mod_spec(), pick(w13_all.shape), pick(w2_all.shape),
                  _const_spec((1, d)), _const_spec((1, d))],
        out_specs=_tok_spec(tm, d),
        out_shape=jax.ShapeDtypeStruct(h.shape, F32),
        compiler_params=_cparams("parallel", "parallel"),
        name="ffn_half",
    )(h, mod, w13_all, w2_all, ln_g.reshape(1, d), ln_b.reshape(1, d))


def _out_kernel(*refs, n_in, prologue, has_bias):
    ins = refs[:n_in]
    w_ref = refs[n_in]
    k = n_in + 1
    bias_ref = None
    if has_bias:
        bias_ref = refs[k]
        k += 1
    h_ref, mod_ref, g_ref, b_ref, o_ref = refs[k:k + 5]
    y = jnp.dot(prologue(*ins), w_ref[...], preferred_element_type=F32)
    if has_bias:
        y = y + bias_ref[...]
    o_ref[0] = _post_norm(h_ref[0], y, mod_ref, 1, g_ref, b_ref)


def _out_proj(tok_inputs, const_inputs, prologue, w, bias, h, mod, ln_g, ln_b, name):
    b, s, d = h.shape
    tm = _tile(s, 256)
    n_in = len(tok_inputs) + len(const_inputs)
    in_specs = [_tok_spec(tm, a.shape[-1]) for a in tok_inputs]
    in_specs += [_const_spec(a.shape) for a in const_inputs]
    in_specs.append(_const_spec(w.shape))
    args = list(tok_inputs) + list(const_inputs) + [w]
    if bias is not None:
        in_specs.append(_const_spec((1, d)))
        args.append(bias.reshape(1, d))
    in_specs += [_tok_spec(tm, d), _mod_spec(), _const_spec((1, d)), _const_spec((1, d))]
    args += [h, mod, ln_g.reshape(1, d), ln_b.reshape(1, d)]
    return pl.pallas_call(
        functools.partial(_out_kernel, n_in=n_in, prologue=prologue, has_bias=bias is not None),
        grid=(b, s // tm),
        in_specs=in_specs,
        out_specs=_tok_spec(tm, d),
        out_shape=jax.ShapeDtypeStruct(h.shape, F32),
        compiler_params=_cparams("parallel", "parallel"),
        name=name,
    )(*args)


def _plain_prologue(y_ref):
    return y_ref[0].astype(BF16)


def _na_qkv_kernel(h_ref, mod_ref, w_ref, q_ref, k_ref, v_ref):
    d = D_MODEL
    u = _modulate(h_ref[0], mod_ref, 1).astype(BF16)
    z = jnp.dot(u, w_ref[...], preferred_element_type=F32)
    q_ref[0] = (z[:, :d] * (NA_HEAD_DIM ** -0.5)).astype(BF16)
    k_ref[0] = z[:, d:2 * d].astype(BF16)
    v_ref[0] = z[:, 2 * d:].astype(BF16)


def _na_qkv(h, mod, w):
    b, s, d = h.shape
    tm = _tile(s, 256)
    out = jax.ShapeDtypeStruct(h.shape, BF16)
    return pl.pallas_call(
        _na_qkv_kernel,
        grid=(b, s // tm),
        in_specs=[_tok_spec(tm, d), _mod_spec(), _const_spec(w.shape)],
        out_specs=[_tok_spec(tm, d)] * 3,
        out_shape=[out] * 3,
        compiler_params=_cparams("parallel", "parallel"),
        name="na_qkv",
    )(h, mod, w)


def _na_row_start(r, rows, kh):
    return jnp.clip(r - kh // 2, 0, rows - kh)


def _na_attn_kernel(q_ref, k_ref, v_ref, kc_ref, vc_ref, bias_ref, o_ref, *, rows, kh):
    r = pl.program_id(1)
    start = pl.multiple_of(_na_row_start(r, rows, kh) * GRID_W, GRID_W)
    win = kh * GRID_W
    lane0 = lax.broadcasted_iota(jnp.int32, (1, LANES), 1) < NA_HEAD_DIM
    zero = jnp.zeros((GRID_W, LANES), BF16)
    pairs = range(NA_HEADS // 2)
    sls = [slice(p * LANES, (p + 1) * LANES) for p in pairs]
    qbd = []
    for sl in sls:
        q = q_ref[0, :, sl]
        qbd.append(jnp.concatenate([jnp.where(lane0, q, zero), jnp.where(lane0, zero, q)], axis=0))
    s = _na_row_start(r, rows, kh) - r + NA_KH - 1
    bias = [jnp.concatenate([bias_ref[s + a, p] for a in range(kh)], axis=0) for p in pairs]
    s_loc = [_dot_nt(k_ref[0, pl.ds(start, win), sls[p]], qbd[p]) + bias[p] for p in pairs]
    s_ctx = [_dot_nt(kc_ref[0, :, sls[p]], qbd[p]) for p in pairs]
    m = [jnp.maximum(jnp.max(s_loc[p], axis=0, keepdims=True), jnp.max(s_ctx[p], axis=0, keepdims=True))
         for p in pairs]
    p_loc = [jnp.exp(s_loc[p] - m[p]) for p in pairs]
    p_ctx = [jnp.exp(s_ctx[p] - m[p]) for p in pairs]
    inv = [1.0 / (jnp.sum(p_loc[p], axis=0, keepdims=True) + jnp.sum(p_ctx[p], axis=0, keepdims=True))
           for p in pairs]
    o = [_dot_tn(p_loc[p] * inv[p], v_ref[0, pl.ds(start, win), sls[p]])
         + _dot_tn(p_ctx[p] * inv[p], vc_ref[0, :, sls[p]]) for p in pairs]
    for p in pairs:
        o_ref[0, :, sls[p]] = jnp.where(lane0, o[p][:GRID_W], o[p][GRID_W:]).astype(BF16)


def _na_bias_table(rpb):
    w = GRID_W
    kc = np.arange(w)[:, None]
    qc = np.arange(w)[None, :]
    cstart = np.clip(qc - NA_KW // 2, 0, w - NA_KW)
    ok = (kc >= cstart) & (kc < cstart + NA_KW)
    dc = np.clip(kc - qc + NA_KW - 1, 0, 2 * NA_KW - 2)
    tbl = jnp.where(ok[None, None], rpb[:, :, dc], NEG_BIG)
    tbl = tbl.reshape(NA_HEADS // 2, 2, 2 * NA_KH - 1, w, w)
    return jnp.transpose(tbl, (2, 0, 3, 1, 4)).reshape(2 * NA_KH - 1, NA_HEADS // 2, w, 2 * w)


def _na_attn(q, k, v, kc, vc, bias_tab):
    b, n, d = q.shape
    l = kc.shape[1]
    rows = n // GRID_W
    kh = min(NA_KH, rows)
    return pl.pallas_call(
        functools.partial(_na_attn_kernel, rows=rows, kh=kh),
        grid=(b, rows),
        in_specs=[pl.BlockSpec((1, GRID_W, d), lambda bi, r: (bi, r, 0)),
                  pl.BlockSpec((1, n, d), lambda bi, r: (bi, 0, 0)),
                  pl.BlockSpec((1, n, d), lambda bi, r: (bi, 0, 0)),
                  pl.BlockSpec((1, l, d), lambda bi, r: (bi, 0, 0)),
                  pl.BlockSpec((1, l, d), lambda bi, r: (bi, 0, 0)),
                  _const_spec(bias_tab.shape)],
        out_specs=pl.BlockSpec((1, GRID_W, d), lambda bi, r: (bi, r, 0)),
        out_shape=jax.ShapeDtypeStruct(q.shape, BF16),
        compiler_params=_cparams("parallel", "arbitrary"),
        name="na_attn",
    )(q, k, v, kc, vc, bias_tab)


def _ctx_attn_kernel(q_ref, k_ref, v_ref, o_ref):
    for hd in range(NA_HEADS):
        sl = slice(hd * NA_HEAD_DIM, (hd + 1) * NA_HEAD_DIM)
        s = _dot_nt(q_ref[0, :, sl], k_ref[0, :, sl])
        p = jnp.exp(s - jnp.max(s, axis=-1, keepdims=True))
        den = jnp.sum(p, axis=-1, keepdims=True)
        o_ref[0, :, sl] = (_dot(p, v_ref[0, :, sl]) / den).astype(BF16)


def _ctx_attn(q, k, v):
    b, l, d = q.shape
    spec = pl.BlockSpec((1, l, d), lambda bi: (bi, 0, 0))
    return pl.pallas_call(
        _ctx_attn_kernel,
        grid=(b,),
        in_specs=[spec] * 3,
        out_specs=spec,
        out_shape=jax.ShapeDtypeStruct(q.shape, BF16),
        compiler_params=_cparams("parallel"),
        name="ctx_attn",
    )(q, k, v)


def _mixer_na(hx, hc, mx, mc, w_qkv, w_o, rpb, ln_g, ln_b, ctx_out):
    w_qkv = w_qkv.astype(BF16)
    w_o = w_o.astype(BF16)
    qx, kx, vx = _na_qkv(hx, mx, w_qkv)
    qc, kc, vc = _na_qkv(hc, mc, w_qkv)
    ox = _na_attn(qx, kx, vx, kc, vc, _na_bias_table(rpb))
    hx = _out_proj([ox], [], _plain_prologue, w_o, None, hx, mx, ln_g, ln_b, "na_out")
    if ctx_out:
        oc = _ctx_attn(qc, kc, vc)
        hc = _out_proj([oc], [], _plain_prologue, w_o, None, hc, mc, ln_g, ln_b, "na_out_ctx")
    return hx, hc


def _conv_in_kernel(h_ref, mod_ref, w_ref, b_ref, o_ref):
    d = D_MODEL
    u = _modulate(h_ref[0], mod_ref, 1).astype(BF16)
    z = jnp.dot(u, w_ref[...], preferred_element_type=F32) + b_ref[...]
    o_ref[0] = z[:, :d] * _sigmoid(z[:, d:])


def _conv_in(h, mod, w1, b1):
    b, s, d = h.shape
    tm = _tile(s, 256)
    return pl.pallas_call(
        _conv_in_kernel,
        grid=(b, s // tm),
        in_specs=[_tok_spec(tm, d), _mod_spec(), _const_spec(w1.shape), _const_spec((1, 2 * d))],
        out_specs=_tok_spec(tm, d),
        out_shape=jax.ShapeDtypeStruct(h.shape, F32),
        compiler_params=_cparams("parallel", "parallel"),
        name="conv_in",
    )(h, mod, w1, b1.reshape(1, 2 * d))


CONV_ROWS = 16


def _conv_dw_kernel(prev_ref, cur_ref, next_ref, w_ref, bdw_ref, g_ref, b_ref, o_ref, win_ref, *, ts, nt):
    t = pl.program_id(1)
    halo = CONV_HALO
    d = cur_ref.shape[-1]
    rows = ts + 2 * halo
    win_ref[0, halo:halo + ts, :] = cur_ref[0]
    win_ref[0, 0:halo, :] = jnp.where(t > 0, prev_ref[0], 0.0)
    win_ref[0, halo + ts:rows, :] = jnp.where(t < nt - 1, next_ref[0], 0.0)
    sub = lax.broadcasted_iota(jnp.int32, (rows // SUBLANES, SUBLANES, LANES), 1)
    for cc in range(d // LANES):
        cols = slice(cc * LANES, (cc + 1) * LANES)
        w = win_ref[0, :, cols].reshape(rows // SUBLANES, SUBLANES, LANES)
        for i in range(1, SUBLANES):
            rot = pltpu.roll(w, SUBLANES - i, axis=1)
            nxt = jnp.concatenate([rot[1:], rot[:1]], axis=0)
            win_ref[i, :, cols] = jnp.where(sub < SUBLANES - i, rot, nxt).reshape(rows, LANES)
    off = halo - CONV_WIDTH // 2
    slabs = CONV_ROWS // SUBLANES
    for g in range(ts // CONV_ROWS):
        acc = jnp.broadcast_to(bdw_ref[...], (slabs, SUBLANES, d))
        for k in range(CONV_WIDTH):
            i, base = (off + k) % SUBLANES, g * CONV_ROWS + (off + k) // SUBLANES * SUBLANES
            acc = acc + win_ref[i, base:base + CONV_ROWS, :].reshape(slabs, SUBLANES, d) * w_ref[k][None]
        y = _layer_norm(acc.reshape(CONV_ROWS, d), g_ref[...], b_ref[...])
        o_ref[0, g * CONV_ROWS:(g + 1) * CONV_ROWS, :] = (y * _sigmoid(y)).astype(BF16)


def _conv_dw(u, w_dw, b_dw, ln_g, ln_b):
    b, s, d = u.shape
    ts = _tile(s, 128)
    nt = s // ts
    hb = ts // CONV_HALO
    nhb = s // CONV_HALO
    w_rep = jnp.broadcast_to(w_dw[:, None, :], (CONV_WIDTH, SUBLANES, d))
    return pl.pallas_call(
        functools.partial(_conv_dw_kernel, ts=ts, nt=nt),
        grid=(b, nt),
        in_specs=[pl.BlockSpec((1, CONV_HALO, d), lambda bi, t: (bi, jnp.maximum(t * hb - 1, 0), 0)),
                  _tok_spec(ts, d),
                  pl.BlockSpec((1, CONV_HALO, d), lambda bi, t: (bi, jnp.minimum((t + 1) * hb, nhb - 1), 0)),
                  _const_spec(w_rep.shape), _const_spec((1, d)), _const_spec((1, d)), _const_spec((1, d))],
        out_specs=_tok_spec(ts, d),
        out_shape=jax.ShapeDtypeStruct(u.shape, BF16),
        scratch_shapes=[pltpu.VMEM((SUBLANES, ts + 2 * CONV_HALO, d), F32)],
        compiler_params=_cparams("parallel", "arbitrary"),
        name="conv_dw",
    )(u, u, u, w_rep, b_dw.reshape(1, d), ln_g.reshape(1, d), ln_b.reshape(1, d))


def _mixer_conv(hx, hc, mx, mc, w1, b1, w_dw, b_dw, cln_g, cln_b, w2, b2, ln_g, ln_b, ctx_out):
    w1 = w1.astype(BF16)
    w2 = w2.astype(BF16)

    def branch(h, mod, name):
        u = _conv_in(h, mod, w1, b1)
        y = _conv_dw(u, w_dw, b_dw, cln_g, cln_b)
        return _out_proj([y], [], _plain_prologue, w2, b2, h, mod, ln_g, ln_b, name)

    hx = branch(hx, mx, "conv_out")
    if ctx_out:
        hc = branch(hc, mc, "conv_out_ctx")
    return hx, hc


GLA_NQK = GLA_HEADS * GLA_DK


def _rope_swap(x):
    quarter = GLA_DK // 4
    lane = lax.broadcasted_iota(jnp.int32, (1, GLA_DK), 1)
    first = (lane % (2 * quarter)) < quarter
    parts = []
    for hd in range(GLA_HEADS):
        xh = x[:, hd * GLA_DK:(hd + 1) * GLA_DK]
        up = pltpu.roll(xh, GLA_DK - quarter, axis=1)
        dn = pltpu.roll(xh, quarter, axis=1)
        parts.append(jnp.where(first, up, dn))
    return jnp.concatenate(parts, axis=1)


def _gla_in_kernel(*refs, rotary):
    if rotary:
        h_ref, mod_ref, w_ref, wa2_ref, ba_ref, cos_ref, sin_ref = refs[:7]
        outs = refs[7:]
    else:
        h_ref, mod_ref, w_ref, wa2_ref, ba_ref = refs[:5]
        outs = refs[5:]
    q_ref, k_ref, v_ref, g_ref, laf_ref, lab_ref = outs
    n, d = GLA_NQK, D_MODEL
    u = _modulate(h_ref[0], mod_ref, 1).astype(BF16)
    z = jnp.dot(u, w_ref[...], preferred_element_type=F32)
    q = z[:, :n]
    k = z[:, n:2 * n]
    o = 2 * n
    if rotary:
        cos = cos_ref[...]
        sin = sin_ref[...]
        q = q * cos + _rope_swap(q) * sin
        k = k * cos + _rope_swap(k) * sin
    q_ref[0] = q * (GLA_DK ** -0.5)
    k_ref[0] = k
    v_ref[0] = z[:, o:o + d].astype(BF16)
    g_ref[0] = z[:, o + d:o + 2 * d]
    t = z[:, o + 2 * d:o + 2 * d + LANES].astype(BF16)
    for dr, la_ref in enumerate((laf_ref, lab_ref)):
        x = jnp.dot(t, wa2_ref[dr], preferred_element_type=F32) + ba_ref[dr:dr + 1, :]
        la_ref[0] = -_softplus(-x) * (1.0 / GLA_NORMALIZER)


def _gla_rope_tables(n):
    quarter = GLA_DK // 4
    t = np.arange(n)
    freqs = ROPE_BASE ** (-jnp.arange(quarter, dtype=F32) / quarter)
    ang_r = jnp.asarray(t // GRID_W, F32)[:, None] * freqs
    ang_c = jnp.asarray(t % GRID_W, F32)[:, None] * freqs
    cos = jnp.concatenate([jnp.cos(ang_r), jnp.cos(ang_r), jnp.cos(ang_c), jnp.cos(ang_c)], axis=-1)
    sin = jnp.concatenate([-jnp.sin(ang_r), jnp.sin(ang_r), -jnp.sin(ang_c), jnp.sin(ang_c)], axis=-1)
    return jnp.tile(cos, (1, GLA_HEADS)), jnp.tile(sin, (1, GLA_HEADS))


def _gla_in(h, mod, w_ext, wa2_ext, ba, rope):
    b, s, d = h.shape
    tm = _tile(s, 256)
    rotary = rope is not None
    in_specs = [_tok_spec(tm, d), _mod_spec(), _const_spec(w_ext.shape), _const_spec(wa2_ext.shape),
                _const_spec(ba.shape)]
    args = [h, mod, w_ext, wa2_ext, ba]
    if rotary:
        in_specs += [pl.BlockSpec((tm, GLA_NQK), lambda bi, t: (t, 0))] * 2
        args += list(rope)
    shp = lambda w, dt: jax.ShapeDtypeStruct((b, s, w), dt)
    return pl.pallas_call(
        functools.partial(_gla_in_kernel, rotary=rotary),
        grid=(b, s // tm),
        in_specs=in_specs,
        out_specs=[_tok_spec(tm, GLA_NQK), _tok_spec(tm, GLA_NQK), _tok_spec(tm, d), _tok_spec(tm, d),
                   _tok_spec(tm, GLA_NQK), _tok_spec(tm, GLA_NQK)],
        out_shape=[shp(GLA_NQK, F32), shp(GLA_NQK, F32), shp(d, BF16), shp(d, F32),
                   shp(GLA_NQK, F32), shp(GLA_NQK, F32)],
        compiler_params=_cparams("parallel", "parallel"),
        name="gla_in_rope" if rotary else "gla_in",
    )(*args)


def _tri_masks(c):
    row = lax.broadcasted_iota(jnp.int32, (c, c), 0)
    col = lax.broadcasted_iota(jnp.int32, (c, c), 1)
    return row >= col, row <= col


def _gla_core_kernel(qf_ref, kf_ref, vf_ref, laf_ref, qb_ref, kb_ref, vb_ref, lab_ref, s0f_ref, s0b_ref,
                     of_ref, ob_ref, sf_ref, sb_ref, stf_ref, stb_ref, *, nblocks, sub):
    i = pl.program_id(1)
    c = GLA_CHUNK

    @pl.when(i == 0)
    def _():
        stf_ref[...] = s0f_ref[0]
        stb_ref[...] = s0b_ref[0]

    lower, upper = _tri_masks(c)
    dirs = ((qf_ref, kf_ref, vf_ref, laf_ref, of_ref, stf_ref, lower, c - 1, range(sub)),
            (qb_ref, kb_ref, vb_ref, lab_ref, ob_ref, stb_ref, upper, 0, range(sub - 1, -1, -1)))
    chains = []
    for q_ref, k_ref, v_ref, la_ref, o_ref, st_ref, mask, last, order in dirs:
        tri = mask.astype(BF16)
        steps = []
        for j in order:
            rows = slice(j * c, (j + 1) * c)
            hi, lo = _split(la_ref[0, rows, :])
            bcum = jnp.dot(tri, hi, preferred_element_type=F32) + jnp.dot(tri, lo, preferred_element_type=F32)
            blast = bcum[last:last + 1, :]
            k = k_ref[0, rows, :]
            steps.append(dict(rows=rows, dec=jnp.exp(blast), qd=(q_ref[0, rows, :] * jnp.exp(bcum)).astype(BF16),
                              kd=(k * jnp.exp(-bcum)).astype(BF16), kt=(k * jnp.exp(blast - bcum)).astype(BF16)))
        for hd in range(GLA_HEADS):
            ks = slice(hd * GLA_DK, (hd + 1) * GLA_DK)
            vs = slice(hd * GLA_DV, (hd + 1) * GLA_DV)
            chains.append(dict(
                mask=mask, o_ref=o_ref, st_ref=st_ref, hd=hd, vs=vs,
                steps=[dict(rows=t["rows"], dec=t["dec"][:, ks], qd=t["qd"][:, ks], kd=t["kd"][:, ks],
                            kt=t["kt"][:, ks], v=v_ref[0, t["rows"], vs]) for t in steps]))
    for ch in chains:
        for t in ch["steps"]:
            t["att"] = jnp.where(ch["mask"], _dot_nt(t["qd"], t["kd"]), 0.0)
    for ch in chains:
        for t in ch["steps"]:
            t["local"] = _dot(t["att"], t["v"])
            t["upd"] = _dot_tn(t["v"], t["kt"])
    state = [ch["st_ref"][ch["hd"]] for ch in chains]
    for j in range(sub):
        out = [ch["steps"][j]["local"] + _dot_nt(ch["steps"][j]["qd"], s) for ch, s in zip(chains, state)]
        for ch, o in zip(chains, out):
            ch["o_ref"][0, ch["steps"][j]["rows"], ch["vs"]] = o
        state = [ch["steps"][j]["dec"] * s + ch["steps"][j]["upd"] for ch, s in zip(chains, state)]
    for ch, s in zip(chains, state):
        ch["st_ref"][ch["hd"]] = s

    @pl.when(i == nblocks - 1)
    def _():
        sf_ref[0] = stf_ref[...]
        sb_ref[0] = stb_ref[...]


def _gla_core(q, k, v, la_f, la_b, s0f, s0b):
    b, s, _ = q.shape
    sub = GLA_CHUNKS_PER_STEP if s % (GLA_CHUNKS_PER_STEP * GLA_CHUNK) == 0 else 1
    rows = sub * GLA_CHUNK
    nblocks = s // rows
    fwd = lambda w: pl.BlockSpec((1, rows, w), lambda bi, i: (bi, i, 0))
    bwd = lambda w: pl.BlockSpec((1, rows, w), lambda bi, i: (bi, nblocks - 1 - i, 0))
    st_shape = (GLA_HEADS, GLA_DV, GLA_DK)
    st_spec = pl.BlockSpec((1,) + st_shape, lambda bi, i: (bi, 0, 0, 0))
    return pl.pallas_call(
        functools.partial(_gla_core_kernel, nblocks=nblocks, sub=sub),
        grid=(b, nblocks),
        in_specs=[fwd(GLA_NQK), fwd(GLA_NQK), fwd(D_MODEL), fwd(GLA_NQK),
                  bwd(GLA_NQK), bwd(GLA_NQK), bwd(D_MODEL), bwd(GLA_NQK), st_spec, st_spec],
        out_specs=[fwd(D_MODEL), bwd(D_MODEL), st_spec, st_spec],
        out_shape=[jax.ShapeDtypeStruct((b, s, D_MODEL), F32)] * 2
        + [jax.ShapeDtypeStruct((b,) + st_shape, F32)] * 2,
        scratch_shapes=[pltpu.VMEM(st_shape, F32), pltpu.VMEM(st_shape, F32)],
        compiler_params=_cparams("parallel", "arbitrary"),
        name="gla_core",
    )(q, k, v, la_f, q, k, v, la_b, s0f, s0b)


def _gla_finish_prologue(of_ref, ob_ref, g_ref, ng_ref):
    o = of_ref[0] + ob_ref[0]
    g = g_ref[0]
    ng = ng_ref[...]
    parts = []
    for hd in range(GLA_HEADS):
        vs = slice(hd * GLA_DV, (hd + 1) * GLA_DV)
        oh = o[:, vs]
        gh = g[:, vs]
        oh = oh * lax.rsqrt(jnp.mean(oh * oh, axis=-1, keepdims=True) + LN_EPS) * ng
        parts.append((oh * (gh * _sigmoid(gh))).astype(BF16))
    return jnp.concatenate(parts, axis=1)


def _mixer_gla(hx, hc, mx, mc, w_in, w_a1, w_a2, b_a, norm_g, w_o, ln_g, ln_b, ctx_out):
    b, n, d = hx.shape
    nqk = GLA_NQK
    pad = jnp.zeros((d, LANES - 2 * GLA_GATE_RANK), F32)
    w_ext = jnp.concatenate([w_in, w_a1[0], w_a1[1], pad], axis=1).astype(BF16)
    wa2_ext = jnp.zeros((2, LANES, nqk), F32)
    wa2_ext = wa2_ext.at[0, :GLA_GATE_RANK].set(w_a2[0])
    wa2_ext = wa2_ext.at[1, GLA_GATE_RANK:2 * GLA_GATE_RANK].set(w_a2[1]).astype(BF16)
    w_o = w_o.astype(BF16)

    qc, kc, vc, gc, lcf, lcb = _gla_in(hc, mc, w_ext, wa2_ext, b_a, None)
    qx, kx, vx, gx, lxf, lxb = _gla_in(hx, mx, w_ext, wa2_ext, b_a, _gla_rope_tables(n))
    s0 = jnp.zeros((b, GLA_HEADS, GLA_DV, GLA_DK), F32)
    ocf, ocb, scf, scb = _gla_core(qc, kc, vc, lcf, lcb, s0, s0)
    oxf, oxb, _, _ = _gla_core(qx, kx, vx, lxf, lxb, scf, scb)
    ng = norm_g.reshape(1, GLA_DV)
    hx = _out_proj([oxf, oxb, gx], [ng], _gla_finish_prologue, w_o, None, hx, mx, ln_g, ln_b, "gla_out")
    if ctx_out:
        hc = _out_proj([ocf, ocb, gc], [ng], _gla_finish_prologue, w_o, None, hc, mc, ln_g, ln_b, "gla_out_ctx")
    return hx, hc


def _head_sum_matrices():
    g = np.zeros((D_MODEL, LANES), np.float32)
    g[np.arange(D_MODEL), np.arange(D_MODEL) // RW_HEAD] = 1.0
    return jnp.asarray(g, BF16), jnp.asarray(g.T.copy(), BF16)


def _rw_prep_kernel(hp_ref, h_ref, hn_ref, mod_ref, mu_ref, wr_ref, wk_ref, wv_ref, g1_ref, g2_ref,
                    w1_ref, w2_ref, a1_ref, a2_ref, w0_ref, a0_ref, kkw_ref, kaw_ref, rkw_ref, gs_ref, gt_ref,
                    r_ref, kk_ref, v_ref, g_ref, bonus_ref,
                    lw0_ref, k0_ref, b0_ref, lw1_ref, k1_ref, b1_ref, *, tm, nt):
    t = pl.program_id(1)
    u = _modulate(h_ref[0], mod_ref, 1)
    up = _modulate(hp_ref[0, 7:8, :], mod_ref, 1)
    un = _modulate(hn_ref[0, 0:1, :], mod_ref, 1)
    up = jnp.where(t > 0, up, 0.0)
    un = jnp.where(t < nt - 1, un, 0.0)
    rowid = lax.broadcasted_iota(jnp.int32, (tm, 1), 0)
    prev = jnp.where(rowid == 0, up, pltpu.roll(u, 1, axis=0))
    nxt = jnp.where(rowid == tm - 1, un, pltpu.roll(u, tm - 1, axis=0))
    xx = 0.5 * (prev + nxt) - u
    mix = lambda j: (u + xx * mu_ref[j:j + 1, :]).astype(BF16)
    gs = gs_ref[...]
    gt = gt_ref[...]

    r = jnp.dot(mix(0), wr_ref[...], preferred_element_type=F32)
    k = jnp.dot(mix(2), wk_ref[...], preferred_element_type=F32)
    v = jnp.dot(mix(3), wv_ref[...], preferred_element_type=F32)
    hw = jnp.tanh(jnp.dot(mix(1), w1_ref[...], preferred_element_type=F32)).astype(BF16)
    ha = jnp.dot(mix(4), a1_ref[...], preferred_element_type=F32).astype(BF16)
    gg = _sigmoid(jnp.dot(mix(5), g1_ref[...], preferred_element_type=F32)).astype(BF16)
    g_ref[0] = jnp.dot(gg, g2_ref[...], preferred_element_type=F32)

    kk = k * kkw_ref[...]
    nrm = jnp.maximum(jnp.sqrt(_dot_sel(kk * kk, gs)), 1e-12)
    kk = kk * _dot_sel(1.0 / nrm, gt)
    ksum = jnp.zeros_like(k)
    for dr, (lw_ref, kd_ref, bd_ref) in enumerate(((lw0_ref, k0_ref, b0_ref), (lw1_ref, k1_ref, b1_ref))):
        xw = w0_ref[dr:dr + 1, :] + jnp.dot(hw, w2_ref[dr], preferred_element_type=F32)
        lw_ref[0] = -float(np.exp(-0.5)) * _sigmoid(xw)
        a = _sigmoid(a0_ref[dr:dr + 1, :] + jnp.dot(ha, a2_ref[dr], preferred_element_type=F32))
        kd = k * (1.0 + (a - 1.0) * kaw_ref[...])
        kd_ref[0] = kd
        bd_ref[0] = kk * a
        ksum = ksum + kd
    r_ref[0] = r
    kk_ref[0] = kk
    v_ref[0] = v.astype(BF16)
    bonus_ref[0] = _dot_sel(_dot_sel(r * rkw_ref[...] * ksum, gs), gt) * v


def _rw_prep(h, mod, consts):
    b, s, d = h.shape
    tm = _tile(s, 256)
    nt = s // tm
    hb = tm // 8
    nhb = s // 8
    in_specs = [pl.BlockSpec((1, 8, d), lambda bi, t: (bi, jnp.maximum(t * hb - 1, 0), 0)),
                _tok_spec(tm, d),
                pl.BlockSpec((1, 8, d), lambda bi, t: (bi, jnp.minimum((t + 1) * hb, nhb - 1), 0)),
                _mod_spec()] + [_const_spec(a.shape) for a in consts]
    f32o = jax.ShapeDtypeStruct(h.shape, F32)
    out_shape = [f32o, f32o, jax.ShapeDtypeStruct(h.shape, BF16), f32o, f32o] + [f32o] * 6
    return pl.pallas_call(
        functools.partial(_rw_prep_kernel, tm=tm, nt=nt),
        grid=(b, nt),
        in_specs=in_specs,
        out_specs=[_tok_spec(tm, d)] * 11,
        out_shape=out_shape,
        compiler_params=_cparams("parallel", "arbitrary"),
        name="rw_prep",
    )(h, h, h, mod, *consts)


def _rw_prelude(refs, forward, rows):
    r_ref, kk_ref, v_ref, lw_ref, k_ref, b_ref = (ref.at[0, rows, :] for ref in refs)
    c = RW_CHUNK
    row = lax.broadcasted_iota(jnp.int32, (c, 4 * c), 0)
    col = lax.broadcasted_iota(jnp.int32, (c, 4 * c), 1) % c
    lower, upper = _tri_masks(c)
    if forward:
        strict, incl, tri, last = row > col, row >= col, lower, c - 1
    else:
        strict, incl, tri, last = row < col, row <= col, upper, 0
    tri = tri.astype(BF16)
    lw = lw_ref[...]
    hi, lo = _split(lw)
    cum = jnp.dot(tri, hi, preferred_element_type=F32) + jnp.dot(tri, lo, preferred_element_type=F32)
    tot = cum[last:last + 1, :]
    e_neg = jnp.exp(-cum)
    e_tail = jnp.exp(tot - cum)
    b_raw = b_ref[...]
    k_raw = k_ref[...]
    return dict(
        rows=rows, strict=strict, incl=incl, dec=jnp.exp(tot),
        a=(kk_ref[...] * jnp.exp(cum - lw)).astype(BF16), r=(r_ref[...] * jnp.exp(cum)).astype(BF16),
        bt=(b_raw * e_neg).astype(BF16), kt=(k_raw * e_neg).astype(BF16),
        bh=(b_raw * e_tail).astype(BF16), kh=(k_raw * e_tail).astype(BF16), v=v_ref[...])


def _rw_scan_block(pres_f, pres_b, stf_ref, stb_ref, yf_ref, yb_ref):
    c = RW_CHUNK
    rowp = lax.broadcasted_iota(jnp.int32, (c, 2 * c), 0)
    colp = lax.broadcasted_iota(jnp.int32, (c, 2 * c), 1) % c
    eye = (rowp == colp).astype(F32)
    blocks = [(rowp // s) == (colp // s) for s in (2 ** e for e in range(1, int(np.log2(c)) + 1))]
    row2 = lax.broadcasted_iota(jnp.int32, (LANES, LANES), 0)
    col2 = lax.broadcasted_iota(jnp.int32, (LANES, LANES), 1)
    same_head = (row2 < RW_HEAD) == (col2 < RW_HEAD)
    lane0 = lax.broadcasted_iota(jnp.int32, (1, LANES), 1) < RW_HEAD
    zero = jnp.zeros((c, LANES), BF16)

    def bdiag(x):
        x = x.astype(BF16)
        return jnp.concatenate([jnp.where(lane0, x, zero), jnp.where(lane0, zero, x)], axis=0)

    chains = [(pres, st_ref, y_ref, slice(p * LANES, (p + 1) * LANES), p)
              for pres, st_ref, y_ref in ((pres_f, stf_ref, yf_ref), (pres_b, stb_ref, yb_ref))
              for p in range(RW_PAIRS)]
    free = [(pre, {nm: pre[nm][:, sl] for nm in ("a", "r", "bt", "kt", "bh", "kh", "v")})
            for pres, _, _, sl, _ in chains for pre in pres]
    n = len(free)
    ar = [jnp.concatenate([t["a"], t["r"]], axis=0) for _, t in free]
    gram = [_dot_nt(ar[i], jnp.concatenate([bdiag(free[i][1]["bt"]), bdiag(free[i][1]["kt"])], axis=0))
            for i in range(n)]
    n1 = [jnp.where(free[i][0]["strict"][:, :2 * c], gram[i][:c, :2 * c], 0.0) for i in range(n)]
    a_ak = [jnp.where(free[i][0]["strict"][:, :2 * c], gram[i][:c, 2 * c:], 0.0) for i in range(n)]
    r_bk = [jnp.where(free[i][0]["incl"], gram[i][c:], 0.0) for i in range(n)]
    vbd = [bdiag(t["v"]) for _, t in free]
    w0 = [_dot(a_ak[i], vbd[i]) for i in range(n)]
    inv = [eye - jnp.where(blocks[0], n1[i], 0.0) for i in range(n)]
    for lvl in range(1, len(blocks)):
        sel = blocks[lvl] & ~blocks[lvl - 1]
        tmp = [_dot(jnp.where(sel, n1[i], 0.0), bdiag(inv[i])) for i in range(n)]
        inv = [inv[i] - _dot(inv[i], bdiag(tmp[i])) for i in range(n)]
    nsub = len(pres_f)
    zt = [st_ref[p] for _, st_ref, _, _, p in chains]
    for j in range(nsub):
        idx = [ci * nsub + j for ci in range(len(chains))]
        sz = [_dot_nt(ar[i], z) for i, z in zip(idx, zt)]
        u = [-_dot(inv[i], bdiag(s[:c] + w0[i])) for i, s in zip(idx, sz)]
        ys = [_dot(r_bk[i], jnp.concatenate([bdiag(uu), vbd[i]], axis=0)) for i, uu in zip(idx, u)]
        for (_, _, y_ref, sl, _), i, s, y in zip(chains, idx, sz, ys):
            y_ref[0, free[i][0]["rows"], sl] = s[c:] + y
        upd = [_dot_tn(jnp.concatenate([uu.astype(BF16), free[i][1]["v"]], axis=0),
                       jnp.concatenate([free[i][1]["bh"], free[i][1]["kh"]], axis=0)) for i, uu in zip(idx, u)]
        zt = [free[i][0]["dec"][:, sl] * z + jnp.where(same_head, up, 0.0)
              for (_, _, _, sl, _), i, z, up in zip(chains, idx, zt, upd)]
    for (_, st_ref, _, _, p), z in zip(chains, zt):
        st_ref[p] = z


def _rw_scan_kernel(*refs, nblocks, sub):
    fwd_in, bwd_in = refs[0:6], refs[6:12]
    s0f_ref, s0b_ref, yf_ref, yb_ref, sf_ref, sb_ref, stf_ref, stb_ref = refs[12:]
    i = pl.program_id(1)
    c = RW_CHUNK

    @pl.when(i == 0)
    def _():
        stf_ref[...] = s0f_ref[0]
        stb_ref[...] = s0b_ref[0]

    pres_f = [_rw_prelude(fwd_in, True, slice(j * c, (j + 1) * c)) for j in range(sub)]
    pres_b = [_rw_prelude(bwd_in, False, slice(j * c, (j + 1) * c)) for j in range(sub - 1, -1, -1)]
    _rw_scan_block(pres_f, pres_b, stf_ref, stb_ref, yf_ref, yb_ref)

    @pl.when(i == nblocks - 1)
    def _():
        sf_ref[0] = stf_ref[...]
        sb_ref[0] = stb_ref[...]


def _rw_scan(r, kk, v, dirs, s0f, s0b):
    b, s, d = r.shape
    sub = RW_CHUNKS_PER_STEP if s % (RW_CHUNKS_PER_STEP * RW_CHUNK) == 0 else 1
    rows = sub * RW_CHUNK
    nblocks = s // rows
    fwd = pl.BlockSpec((1, rows, d), lambda bi, i: (bi, i, 0))
    bwd = pl.BlockSpec((1, rows, d), lambda bi, i: (bi, nblocks - 1 - i, 0))
    st_shape = (RW_PAIRS, LANES, LANES)
    st_spec = pl.BlockSpec((1,) + st_shape, lambda bi, i: (bi, 0, 0, 0))
    (lw0, k0, b0), (lw1, k1, b1) = dirs
    return pl.pallas_call(
        functools.partial(_rw_scan_kernel, nblocks=nblocks, sub=sub),
        grid=(b, nblocks),
        in_specs=[fwd] * 6 + [bwd] * 6 + [st_spec, st_spec],
        out_specs=[fwd, bwd, st_spec, st_spec],
        out_shape=[jax.ShapeDtypeStruct((b, s, d), F32)] * 2 + [jax.ShapeDtypeStruct((b,) + st_shape, F32)] * 2,
        scratch_shapes=[pltpu.VMEM(st_shape, F32), pltpu.VMEM(st_shape, F32)],
        compiler_params=_cparams("parallel", "arbitrary"),
        name="rw_scan",
    )(r, kk, v, lw0, k0, b0, r, kk, v, lw1, k1, b1, s0f, s0b)


def _rw_finish_prologue(yf_ref, yb_ref, bonus_ref, g_ref, gs_ref, gt_ref, gng_ref, gnb_ref):
    y = yf_ref[0] + yb_ref[0]
    gs = gs_ref[...]
    gt = gt_ref[...]
    inv_n = 1.0 / RW_HEAD
    mean = _dot_sel(_dot_sel(y, gs) * inv_n, gt)
    yc = y - mean
    var = _dot_sel(yc * yc, gs) * inv_n
    rstd = _dot_sel(lax.rsqrt(var + RW_GN_EPS), gt)
    yn = yc * rstd * gng_ref[...] + gnb_ref[...]
    return ((yn + bonus_ref[0]) * g_ref[0]).astype(BF16)


def _mixer_rwkv(hx, hc, mx, mc, mu, w_rkv, w0, w1, w2, a0, a1, a2, g1, g2, k_k, k_a, r_k, gn_g, gn_b, w_o,
                ln_g, ln_b, ctx_out):
    b, n, d = hx.shape
    gs, gt = _head_sum_matrices()
    rank_w, rank_a = w1.shape[-1], a1.shape[-1]
    assert 2 * rank_w == LANES and 2 * rank_a == LANES

    def ext(m2):
        z = jnp.zeros((2, LANES, d), F32)
        z = z.at[0, :m2.shape[1]].set(m2[0])
        return z.at[1, m2.shape[1]:].set(m2[1]).astype(BF16)

    consts = [mu, w_rkv[0].astype(BF16), w_rkv[1].astype(BF16), w_rkv[2].astype(BF16),
              g1.astype(BF16), g2.astype(BF16),
              jnp.concatenate([w1[0], w1[1]], axis=1).astype(BF16), ext(w2),
              jnp.concatenate([a1[0], a1[1]], axis=1).astype(BF16), ext(a2),
              w0, a0, k_k.reshape(1, d), k_a.reshape(1, d), r_k.reshape(1, d), gs, gt]
    w_o = w_o.astype(BF16)

    def prep(h, mod):
        r, kk, v, g, bonus, lw0, k0, b0, lw1, k1, b1 = _rw_prep(h, mod, consts)
        return r, kk, v, g, bonus, ((lw0, k0, b0), (lw1, k1, b1))

    rc, kkc, vc, gc, bonc, dc = prep(hc, mc)
    rx, kkx, vx, gx, bonx, dx = prep(hx, mx)
    s0 = jnp.zeros((b, RW_PAIRS, LANES, LANES), F32)
    ycf, ycb, scf, scb = _rw_scan(rc, kkc, vc, dc, s0, s0)
    yxf, yxb, _, _ = _rw_scan(rx, kkx, vx, dx, scf, scb)
    fin_consts = [gs, gt, gn_g.reshape(1, d), gn_b.reshape(1, d)]
    hx = _out_proj([yxf, yxb, bonx, gx], fin_consts, _rw_finish_prologue, w_o, None, hx, mx, ln_g, ln_b, "rw_out")
    if ctx_out:
        hc = _out_proj([ycf, ycb, bonc, gc], fin_consts, _rw_finish_prologue, w_o, None, hc, mc, ln_g, ln_b,
                       "rw_out_ctx")
    return hx, hc


def kernel(x, c, ctx, c_ctx, ada_w, ada_b, ln_g, ln_b, ffn_w13, ffn_w2, na_wqkv, na_wo, na_rpb, cv_w1, cv_b1, cv_wdw, cv_bdw, cv_ln_g, cv_ln_b, cv_w2, cv_b2, gla_win, gla_wa1, gla_wa2, gla_ba, gla_norm_g, gla_wo, rw_mu, rw_wrkv, rw_w0, rw_w1, rw_w2, rw_a0, rw_a1, rw_a2, rw_g1, rw_g2, rw_kk, rw_ka, rw_rk, rw_gn_g, rw_gn_b, rw_wo):
    b, n, d = x.shape
    depth = ada_w.shape[0]
    assert d == D_MODEL and depth == DEPTH and n % GRID_W == 0
    rows = -(-(b + 1) // 8) * 8
    cond = jnp.zeros((rows, d), F32).at[:b].set(c).at[b].set(c_ctx)
    mod_all = _ada(cond, ada_w, ada_b)
    w13 = ffn_w13.astype(BF16)
    w2 = ffn_w2.astype(BF16)
    hx, hc = x, ctx
    for i in range(depth):
        last = i == depth - 1
        mx = mod_all[i, :b].reshape(b, 3 * N_SUB, d)
        mc = jnp.broadcast_to(mod_all[i, b].reshape(1, 3 * N_SUB, d), (b, 3 * N_SUB, d))
        hx = _ffn(hx, mx, w13, w2, i, 0, ln_g[i, 0], ln_b[i, 0], 0)
        hc = _ffn(hc, mc, w13, w2, i, 0, ln_g[i, 0], ln_b[i, 0], 0)
        kind, j = i % 4, i // 4
        lg, lb = ln_g[i, 1], ln_b[i, 1]
        if kind == 0:
            hx, hc = _mixer_na(hx, hc, mx, mc, na_wqkv[j], na_wo[j], na_rpb[j], lg, lb, not last)
        elif kind == 1:
            hx, hc = _mixer_conv(hx, hc, mx, mc, cv_w1[j], cv_b1[j], cv_wdw[j], cv_bdw[j], cv_ln_g[j], cv_ln_b[j],
                                 cv_w2[j], cv_b2[j], lg, lb, not last)
        elif kind == 2:
            hx, hc = _mixer_gla(hx, hc, mx, mc, gla_win[j], gla_wa1[j], gla_wa2[j], gla_ba[j], gla_norm_g[j],
                                gla_wo[j], lg, lb, not last)
        else:
            hx, hc = _mixer_rwkv(hx, hc, mx, mc, rw_mu[j], rw_wrkv[j], rw_w0[j], rw_w1[j], rw_w2[j], rw_a0[j],
                                 rw_a1[j], rw_a2[j], rw_g1[j], rw_g2[j], rw_kk[j], rw_ka[j], rw_rk[j],
                                 rw_gn_g[j], rw_gn_b[j], rw_wo[j], lg, lb, not last)
        hx = _ffn(hx, mx, w13, w2, i, 1, ln_g[i, 2], ln_b[i, 2], 2)
        if not last:
            hc = _ffn(hc, mc, w13, w2, i, 1, ln_g[i, 2], ln_b[i, 2], 2)
    return hx
```

```python
import functools

import numpy as np
import jax
import jax.numpy as jnp
from jax import lax
from jax.experimental import pallas as pl
from jax.experimental.pallas import tpu as pltpu

F32 = jnp.float32
BF16 = jnp.bfloat16

D_MODEL = 1024
DEPTH = 4
N_SUB = 3
GRID_W = 64
ALPHA = (2.0 * DEPTH) ** 0.25
LN_EPS = 1e-5
D_FF = 2816
NA_HEADS = 16
NA_HEAD_DIM = D_MODEL // NA_HEADS
NA_KH = 8
NA_KW = 16
CONV_WIDTH = 31
CONV_HALO = 16
GLA_HEADS = 4
GLA_DK = 128
GLA_DV = 256
GLA_GATE_RANK = 16
GLA_NORMALIZER = 16.0
GLA_CHUNK = 64
GLA_CHUNKS_PER_STEP = 2
ROPE_BASE = 10000.0
RW_HEAD = 64
RW_HEADS = D_MODEL // RW_HEAD
RW_PAIRS = RW_HEADS // 2
RW_CHUNK = 64
RW_CHUNKS_PER_STEP = 1
RW_GN_EPS = 64e-5
LANES = 128
SUBLANES = 8
NEG_BIG = -1e30
VMEM_LIMIT = 56 * 1024 * 1024
FFN_TILE = 512
PROJ_TILE = 512
CONV_TILE = 256
RW_PREP_TILE = 256


def _cparams(*sem):
    return pltpu.CompilerParams(dimension_semantics=sem, vmem_limit_bytes=VMEM_LIMIT)


def _dot(a, b):
    return jnp.dot(a.astype(BF16), b.astype(BF16), preferred_element_type=F32)


def _dot_nt(a, b):
    return lax.dot_general(a.astype(BF16), b.astype(BF16), (((1,), (1,)), ((), ())),
                           preferred_element_type=F32)


def _dot_tn(a, b):
    return lax.dot_general(a.astype(BF16), b.astype(BF16), (((0,), (0,)), ((), ())),
                           preferred_element_type=F32)


def _split(x):
    hi = x.astype(BF16)
    lo = (x - hi.astype(F32)).astype(BF16)
    return hi, lo


def _dot_sel(x, sel):
    hi, lo = _split(x)
    return (jnp.dot(hi, sel, preferred_element_type=F32)
            + jnp.dot(lo, sel, preferred_element_type=F32))


def _sel_dot(sel, x):
    hi, lo = _split(x)
    return (jnp.dot(sel, hi, preferred_element_type=F32)
            + jnp.dot(sel, lo, preferred_element_type=F32))


def _sigmoid(x):
    return jax.nn.sigmoid(x)


def _softplus(x):
    return jnp.maximum(x, 0.0) + jnp.log(1.0 + jnp.exp(-jnp.abs(x)))


def _layer_norm(x, g, b, eps=LN_EPS):
    mu = jnp.mean(x, axis=-1, keepdims=True)
    xc = x - mu
    var = jnp.mean(xc * xc, axis=-1, keepdims=True)
    return xc * lax.rsqrt(var + eps) * g + b


def _mod_rows(mod_ref, j):
    return (mod_ref[0, 3 * j:3 * j + 1, :], mod_ref[0, 3 * j + 1:3 * j + 2, :],
            mod_ref[0, 3 * j + 2:3 * j + 3, :])


def _modulate(h, mod_ref, j):
    shift, scale, _ = _mod_rows(mod_ref, j)
    return h * (1.0 + scale) + shift


def _post_norm(h, y, mod_ref, j, g_ref, b_ref):
    gate = mod_ref[0, 3 * j + 2:3 * j + 3, :]
    return _layer_norm(ALPHA * h + gate * y, g_ref[...], b_ref[...])


def _tok_spec(tm, d):
    return pl.BlockSpec((1, tm, d), lambda b, t: (b, t, 0))


def _mod_spec():
    return pl.BlockSpec((1, 3 * N_SUB, D_MODEL), lambda b, t: (b, 0, 0))


def _const_spec(shape):
    nd = len(shape)
    return pl.BlockSpec(shape, lambda b, t: (0,) * nd, pipeline_mode=pl.Buffered(1))


def _tile(s, pref):
    tm = min(pref, s)
    assert s % tm == 0
    return tm


def _ada_kernel(cond_ref, w_ref, b_ref, o_ref):
    c = cond_ref[...]
    o_ref[0] = _dot(c * _sigmoid(c), w_ref[0]) + b_ref[0]


def _ada(cond, ada_w, ada_b):
    r, d = cond.shape
    depth, _, n = ada_w.shape
    tn = 1024
    return pl.pallas_call(
        _ada_kernel,
        grid=(depth, n // tn),
        in_specs=[pl.BlockSpec((r, d), lambda l, j: (0, 0)),
                  pl.BlockSpec((1, d, tn), lambda l, j: (l, 0, j)),
                  pl.BlockSpec((1, 1, tn), lambda l, j: (l, 0, j))],
        out_specs=pl.BlockSpec((1, r, tn), lambda l, j: (l, 0, j)),
        out_shape=jax.ShapeDtypeStruct((depth, r, n), F32),
        compiler_params=_cparams("parallel", "parallel"),
        name="ada_mod",
    )(cond, ada_w, ada_b.reshape(depth, 1, n))


FFN_CHUNK = 256


def _ffn_kernel(h_ref, mod_ref, w13_ref, w2_ref, g_ref, b_ref, o_ref, *, j):
    h = h_ref[0]
    u = _modulate(h, mod_ref, j).astype(BF16)
    acc = jnp.zeros(h.shape, F32)
    for lo in range(0, D_FF, FFN_CHUNK):
        hi = min(lo + FFN_CHUNK, D_FF)
        a = jnp.dot(u, w13_ref[:, lo:hi], preferred_element_type=F32)
        v = jnp.dot(u, w13_ref[:, D_FF + lo:D_FF + hi], preferred_element_type=F32)
        z = (a * _sigmoid(a) * v).astype(BF16)
        acc = acc + jnp.dot(z, w2_ref[lo:hi, :], preferred_element_type=F32)
    o_ref[0] = _post_norm(h, 0.5 * acc, mod_ref, j, g_ref, b_ref)


def _ffn(h, mod, w13_all, w2_all, layer, half, ln_g, ln_b, j):
    b, s, d = h.shape
    tm = _tile(s, FFN_TILE)
    pick = lambda shape: pl.BlockSpec((None, None) + shape[2:], lambda bi, t: (layer, half, 0, 0),
                                      pipeline_mode=pl.Buffered(1))
    return pl.pallas_call(
        functools.partial(_ffn_kernel, j=j),
        grid=(b, s // tm),
        in_specs=[_tok_spec(tm, d), _mod_spec(), pick(w13_all.shape), pick(w2_all.shape),
                  _const_spec((1, d)), _const_spec((1, d))],
        out_specs=_tok_spec(tm, d),
        out_shape=jax.ShapeDtypeStruct(h.shape, F32),
        compiler_params=_cparams("parallel", "parallel"),
        name="ffn_half",
    )(h, mod, w13_all, w2_all, ln_g.reshape(1, d), ln_b.reshape(1, d))


def _out_kernel(*refs, n_in, prologue, has_bias):
    ins = refs[:n_in]
    w_ref = refs[n_in]
    k = n_in + 1
    bias_ref = None
    if has_bias:
        bias_ref = refs[k]
        k += 1
    h_ref, mod_ref, g_ref, b_ref, o_ref = refs[k:k + 5]
    y = jnp.dot(prologue(*ins), w_ref[...], preferred_element_type=F32)
    if has_bias:
        y = y + bias_ref[...]
    o_ref[0] = _post_norm(h_ref[0], y, mod_ref, 1, g_ref, b_ref)


def _out_proj(tok_inputs, const_inputs, prologue, w, bias, h, mod, ln_g, ln_b, name, tile=PROJ_TILE):
    b, s, d = h.shape
    tm = _tile(s, tile)
    n_in = len(tok_inputs) + len(const_inputs)
    in_specs = [_tok_spec(tm, a.shape[-1]) for a in tok_inputs]
    in_specs += [_const_spec(a.shape) for a in const_inputs]
    in_specs.append(_const_spec(w.shape))
    args = list(tok_inputs) + list(const_inputs) + [w]
    if bias is not None:
        in_specs.append(_const_spec((1, d)))
        args.append(bias.reshape(1, d))
    in_specs += [_tok_spec(tm, d), _mod_spec(), _const_spec((1, d)), _const_spec((1, d))]
    args += [h, mod, ln_g.reshape(1, d), ln_b.reshape(1, d)]
    return pl.pallas_call(
        functools.partial(_out_kernel, n_in=n_in, prologue=prologue, has_bias=bias is not None),
        grid=(b, s // tm),
        in_specs=in_specs,
        out_specs=_tok_spec(tm, d),
        out_shape=jax.ShapeDtypeStruct(h.shape, F32),
        compiler_params=_cparams("parallel", "parallel"),
        name=name,
    )(*args)


def _plain_prologue(y_ref):
    return y_ref[0].astype(BF16)


def _na_qkv_kernel(h_ref, mod_ref, w_ref, q_ref, k_ref, v_ref):
    d = D_MODEL
    u = _modulate(h_ref[0], mod_ref, 1).astype(BF16)
    z = jnp.dot(u, w_ref[...], preferred_element_type=F32)
    q_ref[0] = (z[:, :d] * (NA_HEAD_DIM ** -0.5)).astype(BF16)
    k_ref[0] = z[:, d:2 * d].astype(BF16)
    v_ref[0] = z[:, 2 * d:].astype(BF16)


def _na_qkv(h, mod, w):
    b, s, d = h.shape
    tm = _tile(s, PROJ_TILE)
    out = jax.ShapeDtypeStruct(h.shape, BF16)
    return pl.pallas_call(
        _na_qkv_kernel,
        grid=(b, s // tm),
        in_specs=[_tok_spec(tm, d), _mod_spec(), _const_spec(w.shape)],
        out_specs=[_tok_spec(tm, d)] * 3,
        out_shape=[out] * 3,
        compiler_params=_cparams("parallel", "parallel"),
        name="na_qkv",
    )(h, mod, w)


def _na_row_start(r, rows, kh):
    return jnp.clip(r - kh // 2, 0, rows - kh)


def _na_attn_kernel(q_ref, k_ref, v_ref, kc_ref, vc_ref, bias_ref, o_ref, *, rows, kh):
    r = pl.program_id(1)
    start = pl.multiple_of(_na_row_start(r, rows, kh) * GRID_W, GRID_W)
    win = kh * GRID_W
    lane0 = lax.broadcasted_iota(jnp.int32, (1, LANES), 1) < NA_HEAD_DIM
    zero = jnp.zeros((GRID_W, LANES), BF16)
    pairs = range(NA_HEADS // 2)
    sls = [slice(p * LANES, (p + 1) * LANES) for p in pairs]
    qbd = []
    for sl in sls:
        q = q_ref[0, :, sl]
        qbd.append(jnp.concatenate([jnp.where(lane0, q, zero), jnp.where(lane0, zero, q)], axis=0))
    s = _na_row_start(r, rows, kh) - r + NA_KH - 1
    bias = [jnp.concatenate([bias_ref[s + a, p] for a in range(kh)], axis=0) for p in pairs]
    s_loc = [_dot_nt(k_ref[0, pl.ds(start, win), sls[p]], qbd[p]) + bias[p] for p in pairs]
    s_ctx = [_dot_nt(kc_ref[0, :, sls[p]], qbd[p]) for p in pairs]
    m = [jnp.maximum(jnp.max(s_loc[p], axis=0, keepdims=True), jnp.max(s_ctx[p], axis=0, keepdims=True))
         for p in pairs]
    p_loc = [jnp.exp(s_loc[p] - m[p]) for p in pairs]
    p_ctx = [jnp.exp(s_ctx[p] - m[p]) for p in pairs]
    inv = [1.0 / (jnp.sum(p_loc[p], axis=0, keepdims=True) + jnp.sum(p_ctx[p], axis=0, keepdims=True))
           for p in pairs]
    o = [_dot_tn(p_loc[p] * inv[p], v_ref[0, pl.ds(start, win), sls[p]])
         + _dot_tn(p_ctx[p] * inv[p], vc_ref[0, :, sls[p]]) for p in pairs]
    for p in pairs:
        o_ref[0, :, sls[p]] = jnp.where(lane0, o[p][:GRID_W], o[p][GRID_W:]).astype(BF16)


def _na_bias_table(rpb):
    w = GRID_W
    nrow = 2 * NA_KH - 1
    qc = np.arange(w)[:, None]
    kc = np.arange(w)[None, :]
    cstart = np.clip(qc - NA_KW // 2, 0, w - NA_KW)
    ok = (kc >= cstart) & (kc < cstart + NA_KW)
    lpad = w - NA_KW
    g = jnp.pad(rpb, ((0, 0), (0, 0), (lpad, lpad)))
    flat = jnp.broadcast_to(g[:, :, None, :], (NA_HEADS, nrow, w, 2 * w - 1)).reshape(NA_HEADS, nrow, -1)
    t = flat[:, :, w - 1:w - 1 + w * (2 * w - 2)].reshape(NA_HEADS, nrow, w, 2 * w - 2)[..., :w]
    tbl = jnp.where(ok[None, None], t, NEG_BIG)
    tbl = tbl.reshape(NA_HEADS // 2, 2, nrow, w, w)
    return jnp.transpose(tbl, (2, 0, 4, 1, 3)).reshape(nrow, NA_HEADS // 2, w, 2 * w)


def _na_attn(q, k, v, kc, vc, bias_tab):
    b, n, d = q.shape
    l = kc.shape[1]
    rows = n // GRID_W
    kh = min(NA_KH, rows)
    return pl.pallas_call(
        functools.partial(_na_attn_kernel, rows=rows, kh=kh),
        grid=(b, rows),
        in_specs=[pl.BlockSpec((1, GRID_W, d), lambda bi, r: (bi, r, 0)),
                  pl.BlockSpec((1, n, d), lambda bi, r: (bi, 0, 0)),
                  pl.BlockSpec((1, n, d), lambda bi, r: (bi, 0, 0)),
                  pl.BlockSpec((1, l, d), lambda bi, r: (bi, 0, 0)),
                  pl.BlockSpec((1, l, d), lambda bi, r: (bi, 0, 0)),
                  _const_spec(bias_tab.shape)],
        out_specs=pl.BlockSpec((1, GRID_W, d), lambda bi, r: (bi, r, 0)),
        out_shape=jax.ShapeDtypeStruct(q.shape, BF16),
        compiler_params=_cparams("parallel", "arbitrary"),
        name="na_attn",
    )(q, k, v, kc, vc, bias_tab)


def _ctx_attn_kernel(q_ref, k_ref, v_ref, o_ref):
    for hd in range(NA_HEADS):
        sl = slice(hd * NA_HEAD_DIM, (hd + 1) * NA_HEAD_DIM)
        s = _dot_nt(q_ref[0, :, sl], k_ref[0, :, sl])
        p = jnp.exp(s - jnp.max(s, axis=-1, keepdims=True))
        den = jnp.sum(p, axis=-1, keepdims=True)
        o_ref[0, :, sl] = (_dot(p, v_ref[0, :, sl]) / den).astype(BF16)


def _ctx_attn(q, k, v):
    b, l, d = q.shape
    spec = pl.BlockSpec((1, l, d), lambda bi: (bi, 0, 0))
    return pl.pallas_call(
        _ctx_attn_kernel,
        grid=(b,),
        in_specs=[spec] * 3,
        out_specs=spec,
        out_shape=jax.ShapeDtypeStruct(q.shape, BF16),
        compiler_params=_cparams("parallel"),
        name="ctx_attn",
    )(q, k, v)


def _mixer_na(hx, hc, mx, mc, w_qkv, w_o, rpb, ln_g, ln_b, ctx_out):
    w_qkv = w_qkv.astype(BF16)
    w_o = w_o.astype(BF16)
    qx, kx, vx = _na_qkv(hx, mx, w_qkv)
    qc, kc, vc = _na_qkv(hc, mc, w_qkv)
    ox = _na_attn(qx, kx, vx, kc, vc, _na_bias_table(rpb))
    hx = _out_proj([ox], [], _plain_prologue, w_o, None, hx, mx, ln_g, ln_b, "na_out")
    if ctx_out:
        oc = _ctx_attn(qc, kc, vc)
        hc = _out_proj([oc], [], _plain_prologue, w_o, None, hc, mc, ln_g, ln_b, "na_out_ctx")
    return hx, hc


def _conv_in_kernel(h_ref, mod_ref, w_ref, b_ref, o_ref):
    d = D_MODEL
    u = _modulate(h_ref[0], mod_ref, 1).astype(BF16)
    z = jnp.dot(u, w_ref[...], preferred_element_type=F32) + b_ref[...]
    o_ref[0] = z[:, :d] * _sigmoid(z[:, d:])


def _conv_in(h, mod, w1, b1):
    b, s, d = h.shape
    tm = _tile(s, PROJ_TILE)
    return pl.pallas_call(
        _conv_in_kernel,
        grid=(b, s // tm),
        in_specs=[_tok_spec(tm, d), _mod_spec(), _const_spec(w1.shape), _const_spec((1, 2 * d))],
        out_specs=_tok_spec(tm, d),
        out_shape=jax.ShapeDtypeStruct(h.shape, F32),
        compiler_params=_cparams("parallel", "parallel"),
        name="conv_in",
    )(h, mod, w1, b1.reshape(1, 2 * d))


CONV_ROWS = 16


def _conv_dw_kernel(prev_ref, cur_ref, next_ref, w_ref, bdw_ref, g_ref, b_ref, o_ref, win_ref, *, ts, nt):
    t = pl.program_id(1)
    halo = CONV_HALO
    d = cur_ref.shape[-1]
    rows = ts + 2 * halo
    win_ref[0, halo:halo + ts, :] = cur_ref[0]
    win_ref[0, 0:halo, :] = jnp.where(t > 0, prev_ref[0], 0.0)
    win_ref[0, halo + ts:rows, :] = jnp.where(t < nt - 1, next_ref[0], 0.0)
    sub = lax.broadcasted_iota(jnp.int32, (rows // SUBLANES, SUBLANES, LANES), 1)
    for cc in range(d // LANES):
        cols = slice(cc * LANES, (cc + 1) * LANES)
        w = win_ref[0, :, cols].reshape(rows // SUBLANES, SUBLANES, LANES)
        for i in range(1, SUBLANES):
            rot = pltpu.roll(w, SUBLANES - i, axis=1)
            nxt = jnp.concatenate([rot[1:], rot[:1]], axis=0)
            win_ref[i, :, cols] = jnp.where(sub < SUBLANES - i, rot, nxt).reshape(rows, LANES)
    off = halo - CONV_WIDTH // 2
    slabs = CONV_ROWS // SUBLANES
    for g in range(ts // CONV_ROWS):
        acc = jnp.broadcast_to(bdw_ref[...], (slabs, SUBLANES, d))
        for k in range(CONV_WIDTH):
            i, base = (off + k) % SUBLANES, g * CONV_ROWS + (off + k) // SUBLANES * SUBLANES
            acc = acc + win_ref[i, base:base + CONV_ROWS, :].reshape(slabs, SUBLANES, d) * w_ref[k][None]
        y = _layer_norm(acc.reshape(CONV_ROWS, d), g_ref[...], b_ref[...])
        o_ref[0, g * CONV_ROWS:(g + 1) * CONV_ROWS, :] = (y * _sigmoid(y)).astype(BF16)


def _conv_dw(u, w_dw, b_dw, ln_g, ln_b):
    b, s, d = u.shape
    ts = _tile(s, CONV_TILE)
    nt = s // ts
    hb = ts // CONV_HALO
    nhb = s // CONV_HALO
    w_rep = jnp.broadcast_to(w_dw[:, None, :], (CONV_WIDTH, SUBLANES, d))
    return pl.pallas_call(
        functools.partial(_conv_dw_kernel, ts=ts, nt=nt),
        grid=(b, nt),
        in_specs=[pl.BlockSpec((1, CONV_HALO, d), lambda bi, t: (bi, jnp.maximum(t * hb - 1, 0), 0)),
                  _tok_spec(ts, d),
                  pl.BlockSpec((1, CONV_HALO, d), lambda bi, t: (bi, jnp.minimum((t + 1) * hb, nhb - 1), 0)),
                  _const_spec(w_rep.shape), _const_spec((1, d)), _const_spec((1, d)), _const_spec((1, d))],
        out_specs=_tok_spec(ts, d),
        out_shape=jax.ShapeDtypeStruct(u.shape, BF16),
        scratch_shapes=[pltpu.VMEM((SUBLANES, ts + 2 * CONV_HALO, d), F32)],
        compiler_params=_cparams("parallel", "arbitrary"),
        name="conv_dw",
    )(u, u, u, w_rep, b_dw.reshape(1, d), ln_g.reshape(1, d), ln_b.reshape(1, d))


def _mixer_conv(hx, hc, mx, mc, w1, b1, w_dw, b_dw, cln_g, cln_b, w2, b2, ln_g, ln_b, ctx_out):
    w1 = w1.astype(BF16)
    w2 = w2.astype(BF16)

    def branch(h, mod, name):
        u = _conv_in(h, mod, w1, b1)
        y = _conv_dw(u, w_dw, b_dw, cln_g, cln_b)
        return _out_proj([y], [], _plain_prologue, w2, b2, h, mod, ln_g, ln_b, name)

    hx = branch(hx, mx, "conv_out")
    if ctx_out:
        hc = branch(hc, mc, "conv_out_ctx")
    return hx, hc


GLA_NQK = GLA_HEADS * GLA_DK


def _rope_swap(x):
    quarter = GLA_DK // 4
    lane = lax.broadcasted_iota(jnp.int32, (1, GLA_DK), 1)
    first = (lane % (2 * quarter)) < quarter
    parts = []
    for hd in range(GLA_HEADS):
        xh = x[:, hd * GLA_DK:(hd + 1) * GLA_DK]
        up = pltpu.roll(xh, GLA_DK - quarter, axis=1)
        dn = pltpu.roll(xh, quarter, axis=1)
        parts.append(jnp.where(first, up, dn))
    return jnp.concatenate(parts, axis=1)


def _gla_in_kernel(*refs, rotary):
    if rotary:
        h_ref, mod_ref, w_ref, wa2_ref, ba_ref, cos_ref, sin_ref = refs[:7]
        outs = refs[7:]
    else:
        h_ref, mod_ref, w_ref, wa2_ref, ba_ref = refs[:5]
        outs = refs[5:]
    q_ref, k_ref, v_ref, g_ref, laf_ref, lab_ref = outs
    n, d = GLA_NQK, D_MODEL
    u = _modulate(h_ref[0], mod_ref, 1).astype(BF16)
    z = jnp.dot(u, w_ref[...], preferred_element_type=F32)
    q = z[:, :n]
    k = z[:, n:2 * n]
    o = 2 * n
    if rotary:
        cos = cos_ref[...]
        sin = sin_ref[...]
        q = q * cos + _rope_swap(q) * sin
        k = k * cos + _rope_swap(k) * sin
    q_ref[0] = q * (GLA_DK ** -0.5)
    k_ref[0] = k
    v_ref[0] = z[:, o:o + d].astype(BF16)
    g_ref[0] = z[:, o + d:o + 2 * d].astype(BF16)
    t = z[:, o + 2 * d:o + 2 * d + LANES].astype(BF16)
    for dr, la_ref in enumerate((laf_ref, lab_ref)):
        x = jnp.dot(t, wa2_ref[dr], preferred_element_type=F32) + ba_ref[dr:dr + 1, :]
        la_ref[0] = -_softplus(-x) * (1.0 / GLA_NORMALIZER)


def _gla_rope_tables(n):
    quarter = GLA_DK // 4
    t = np.arange(n)
    freqs = ROPE_BASE ** (-jnp.arange(quarter, dtype=F32) / quarter)
    ang_r = jnp.asarray(t // GRID_W, F32)[:, None] * freqs
    ang_c = jnp.asarray(t % GRID_W, F32)[:, None] * freqs
    cos = jnp.concatenate([jnp.cos(ang_r), jnp.cos(ang_r), jnp.cos(ang_c), jnp.cos(ang_c)], axis=-1)
    sin = jnp.concatenate([-jnp.sin(ang_r), jnp.sin(ang_r), -jnp.sin(ang_c), jnp.sin(ang_c)], axis=-1)
    return jnp.tile(cos, (1, GLA_HEADS)), jnp.tile(sin, (1, GLA_HEADS))


def _gla_in(h, mod, w_ext, wa2_ext, ba, rope):
    b, s, d = h.shape
    tm = _tile(s, PROJ_TILE)
    rotary = rope is not None
    in_specs = [_tok_spec(tm, d), _mod_spec(), _const_spec(w_ext.shape), _const_spec(wa2_ext.shape),
                _const_spec(ba.shape)]
    args = [h, mod, w_ext, wa2_ext, ba]
    if rotary:
        in_specs += [pl.BlockSpec((tm, GLA_NQK), lambda bi, t: (t, 0))] * 2
        args += list(rope)
    shp = lambda w, dt: jax.ShapeDtypeStruct((b, s, w), dt)
    return pl.pallas_call(
        functools.partial(_gla_in_kernel, rotary=rotary),
        grid=(b, s // tm),
        in_specs=in_specs,
        out_specs=[_tok_spec(tm, GLA_NQK), _tok_spec(tm, GLA_NQK), _tok_spec(tm, d), _tok_spec(tm, d),
                   _tok_spec(tm, GLA_NQK), _tok_spec(tm, GLA_NQK)],
        out_shape=[shp(GLA_NQK, F32), shp(GLA_NQK, F32), shp(d, BF16), shp(d, BF16),
                   shp(GLA_NQK, F32), shp(GLA_NQK, F32)],
        compiler_params=_cparams("parallel", "parallel"),
        name="gla_in_rope" if rotary else "gla_in",
    )(*args)


def _tri_masks(c):
    row = lax.broadcasted_iota(jnp.int32, (c, c), 0)
    col = lax.broadcasted_iota(jnp.int32, (c, c), 1)
    return row >= col, row <= col


def _gla_core_kernel(qf_ref, kf_ref, vf_ref, laf_ref, qb_ref, kb_ref, vb_ref, lab_ref, s0f_ref, s0b_ref,
                     of_ref, ob_ref, sf_ref, sb_ref, stf_ref, stb_ref, *, nblocks, sub):
    i = pl.program_id(1)
    c = GLA_CHUNK

    @pl.when(i == 0)
    def _():
        stf_ref[...] = s0f_ref[0]
        stb_ref[...] = s0b_ref[0]

    lower, upper = _tri_masks(c)
    dirs = ((qf_ref, kf_ref, vf_ref, laf_ref, of_ref, stf_ref, lower, c - 1, range(sub)),
            (qb_ref, kb_ref, vb_ref, lab_ref, ob_ref, stb_ref, upper, 0, range(sub - 1, -1, -1)))
    chains = []
    for q_ref, k_ref, v_ref, la_ref, o_ref, st_ref, mask, last, order in dirs:
        tri = mask.astype(BF16)
        steps = []
        for j in order:
            rows = slice(j * c, (j + 1) * c)
            hi, lo = _split(la_ref[0, rows, :])
            bcum = jnp.dot(tri, hi, preferred_element_type=F32) + jnp.dot(tri, lo, preferred_element_type=F32)
            blast = bcum[last:last + 1, :]
            k = k_ref[0, rows, :]
            steps.append(dict(rows=rows, dec=jnp.exp(blast), qd=(q_ref[0, rows, :] * jnp.exp(bcum)).astype(BF16),
                              kd=(k * jnp.exp(-bcum)).astype(BF16), kt=(k * jnp.exp(blast - bcum)).astype(BF16)))
        for hd in range(GLA_HEADS):
            ks = slice(hd * GLA_DK, (hd + 1) * GLA_DK)
            vs = slice(hd * GLA_DV, (hd + 1) * GLA_DV)
            chains.append(dict(
                mask=mask, o_ref=o_ref, st_ref=st_ref, hd=hd, vs=vs,
                steps=[dict(rows=t["rows"], dec=t["dec"][:, ks], qd=t["qd"][:, ks], kd=t["kd"][:, ks],
                            kt=t["kt"][:, ks], v=v_ref[0, t["rows"], vs]) for t in steps]))
    for ch in chains:
        for t in ch["steps"]:
            t["att"] = jnp.where(ch["mask"], _dot_nt(t["qd"], t["kd"]), 0.0)
    for ch in chains:
        for t in ch["steps"]:
            t["local"] = _dot(t["att"], t["v"])
            t["upd"] = _dot_tn(t["v"], t["kt"])
    state = [ch["st_ref"][ch["hd"]] for ch in chains]
    for j in range(sub):
        out = [ch["steps"][j]["local"] + _dot_nt(ch["steps"][j]["qd"], s) for ch, s in zip(chains, state)]
        for ch, o in zip(chains, out):
            ch["o_ref"][0, ch["steps"][j]["rows"], ch["vs"]] = o.astype(BF16)
        state = [ch["steps"][j]["dec"] * s + ch["steps"][j]["upd"] for ch, s in zip(chains, state)]
    for ch, s in zip(chains, state):
        ch["st_ref"][ch["hd"]] = s

    @pl.when(i == nblocks - 1)
    def _():
        sf_ref[0] = stf_ref[...]
        sb_ref[0] = stb_ref[...]


def _gla_core(q, k, v, la_f, la_b, s0f, s0b):
    b, s, _ = q.shape
    sub = GLA_CHUNKS_PER_STEP if s % (GLA_CHUNKS_PER_STEP * GLA_CHUNK) == 0 else 1
    rows = sub * GLA_CHUNK
    nblocks = s // rows
    fwd = lambda w: pl.BlockSpec((1, rows, w), lambda bi, i: (bi, i, 0))
    bwd = lambda w: pl.BlockSpec((1, rows, w), lambda bi, i: (bi, nblocks - 1 - i, 0))
    st_shape = (GLA_HEADS, GLA_DV, GLA_DK)
    st_spec = pl.BlockSpec((1,) + st_shape, lambda bi, i: (bi, 0, 0, 0))
    return pl.pallas_call(
        functools.partial(_gla_core_kernel, nblocks=nblocks, sub=sub),
        grid=(b, nblocks),
        in_specs=[fwd(GLA_NQK), fwd(GLA_NQK), fwd(D_MODEL), fwd(GLA_NQK),
                  bwd(GLA_NQK), bwd(GLA_NQK), bwd(D_MODEL), bwd(GLA_NQK), st_spec, st_spec],
        out_specs=[fwd(D_MODEL), bwd(D_MODEL), st_spec, st_spec],
        out_shape=[jax.ShapeDtypeStruct((b, s, D_MODEL), BF16)] * 2
        + [jax.ShapeDtypeStruct((b,) + st_shape, F32)] * 2,
        scratch_shapes=[pltpu.VMEM(st_shape, F32), pltpu.VMEM(st_shape, F32)],
        compiler_params=_cparams("parallel", "arbitrary"),
        name="gla_core",
    )(q, k, v, la_f, q, k, v, la_b, s0f, s0b)


def _gla_finish_prologue(of_ref, ob_ref, g_ref, ng_ref):
    o = of_ref[0].astype(F32) + ob_ref[0].astype(F32)
    g = g_ref[0].astype(F32)
    ng = ng_ref[...]
    parts = []
    for hd in range(GLA_HEADS):
        vs = slice(hd * GLA_DV, (hd + 1) * GLA_DV)
        oh = o[:, vs]
        gh = g[:, vs]
        oh = oh * lax.rsqrt(jnp.mean(oh * oh, axis=-1, keepdims=True) + LN_EPS) * ng
        parts.append((oh * (gh * _sigmoid(gh))).astype(BF16))
    return jnp.concatenate(parts, axis=1)


def _mixer_gla(hx, hc, mx, mc, w_in, w_a1, w_a2, b_a, norm_g, w_o, ln_g, ln_b, ctx_out):
    b, n, d = hx.shape
    nqk = GLA_NQK
    pad = jnp.zeros((d, LANES - 2 * GLA_GATE_RANK), F32)
    w_ext = jnp.concatenate([w_in, w_a1[0], w_a1[1], pad], axis=1).astype(BF16)
    wa2_ext = jnp.zeros((2, LANES, nqk), F32)
    wa2_ext = wa2_ext.at[0, :GLA_GATE_RANK].set(w_a2[0])
    wa2_ext = wa2_ext.at[1, GLA_GATE_RANK:2 * GLA_GATE_RANK].set(w_a2[1]).astype(BF16)
    w_o = w_o.astype(BF16)

    qc, kc, vc, gc, lcf, lcb = _gla_in(hc, mc, w_ext, wa2_ext, b_a, None)
    qx, kx, vx, gx, lxf, lxb = _gla_in(hx, mx, w_ext, wa2_ext, b_a, _gla_rope_tables(n))
    s0 = jnp.zeros((b, GLA_HEADS, GLA_DV, GLA_DK), F32)
    ocf, ocb, scf, scb = _gla_core(qc, kc, vc, lcf, lcb, s0, s0)
    oxf, oxb, _, _ = _gla_core(qx, kx, vx, lxf, lxb, scf, scb)
    ng = norm_g.reshape(1, GLA_DV)
    hx = _out_proj([oxf, oxb, gx], [ng], _gla_finish_prologue, w_o, None, hx, mx, ln_g, ln_b, "gla_out")
    if ctx_out:
        hc = _out_proj([ocf, ocb, gc], [ng], _gla_finish_prologue, w_o, None, hc, mc, ln_g, ln_b, "gla_out_ctx")
    return hx, hc


def _head_sum_matrices():
    g = np.zeros((D_MODEL, LANES), np.float32)
    g[np.arange(D_MODEL), np.arange(D_MODEL) // RW_HEAD] = 1.0
    return jnp.asarray(g, BF16), jnp.asarray(g.T.copy(), BF16)


def _rw_prep_kernel(hp_ref, h_ref, hn_ref, mod_ref, mu_ref, wr_ref, wk_ref, wv_ref, g1_ref, g2_ref,
                    w1_ref, w2_ref, a1_ref, a2_ref, w0_ref, a0_ref, kkw_ref, kaw_ref, rkw_ref, gs_ref, gt_ref,
                    r_ref, kk_ref, v_ref, g_ref, bonus_ref,
                    lw0_ref, k0_ref, b0_ref, lw1_ref, k1_ref, b1_ref, *, tm, nt):
    t = pl.program_id(1)
    u = _modulate(h_ref[0], mod_ref, 1)
    up = _modulate(hp_ref[0, 7:8, :], mod_ref, 1)
    un = _modulate(hn_ref[0, 0:1, :], mod_ref, 1)
    up = jnp.where(t > 0, up, 0.0)
    un = jnp.where(t < nt - 1, un, 0.0)
    rowid = lax.broadcasted_iota(jnp.int32, (tm, 1), 0)
    prev = jnp.where(rowid == 0, up, pltpu.roll(u, 1, axis=0))
    nxt = jnp.where(rowid == tm - 1, un, pltpu.roll(u, tm - 1, axis=0))
    xx = 0.5 * (prev + nxt) - u
    mix = lambda j: (u + xx * mu_ref[j:j + 1, :]).astype(BF16)
    gs = gs_ref[...]
    gt = gt_ref[...]

    r = jnp.dot(mix(0), wr_ref[...], preferred_element_type=F32)
    k = jnp.dot(mix(2), wk_ref[...], preferred_element_type=F32)
    v = jnp.dot(mix(3), wv_ref[...], preferred_element_type=F32)
    hw = jnp.tanh(jnp.dot(mix(1), w1_ref[...], preferred_element_type=F32)).astype(BF16)
    ha = jnp.dot(mix(4), a1_ref[...], preferred_element_type=F32).astype(BF16)
    gg = _sigmoid(jnp.dot(mix(5), g1_ref[...], preferred_element_type=F32)).astype(BF16)
    g_ref[0] = jnp.dot(gg, g2_ref[...], preferred_element_type=F32).astype(BF16)

    kk = k * kkw_ref[...]
    nrm = jnp.maximum(jnp.sqrt(_dot_sel(kk * kk, gs)), 1e-12)
    kk = kk * _dot_sel(1.0 / nrm, gt)
    ksum = jnp.zeros_like(k)
    for dr, (lw_ref, kd_ref, bd_ref) in enumerate(((lw0_ref, k0_ref, b0_ref), (lw1_ref, k1_ref, b1_ref))):
        xw = w0_ref[dr:dr + 1, :] + jnp.dot(hw, w2_ref[dr], preferred_element_type=F32)
        lw_ref[0] = -float(np.exp(-0.5)) * _sigmoid(xw)
        a = _sigmoid(a0_ref[dr:dr + 1, :] + jnp.dot(ha, a2_ref[dr], preferred_element_type=F32))
        kd = k * (1.0 + (a - 1.0) * kaw_ref[...])
        kd_ref[0] = kd
        bd_ref[0] = kk * a
        ksum = ksum + kd
    r_ref[0] = r
    kk_ref[0] = kk
    v_ref[0] = v.astype(BF16)
    bonus_ref[0] = (_dot_sel(_dot_sel(r * rkw_ref[...] * ksum, gs), gt) * v).astype(BF16)


def _rw_prep(h, mod, consts):
    b, s, d = h.shape
    tm = _tile(s, RW_PREP_TILE)
    nt = s // tm
    hb = tm // 8
    nhb = s // 8
    in_specs = [pl.BlockSpec((1, 8, d), lambda bi, t: (bi, jnp.maximum(t * hb - 1, 0), 0)),
                _tok_spec(tm, d),
                pl.BlockSpec((1, 8, d), lambda bi, t: (bi, jnp.minimum((t + 1) * hb, nhb - 1), 0)),
                _mod_spec()] + [_const_spec(a.shape) for a in consts]
    f32o = jax.ShapeDtypeStruct(h.shape, F32)
    bf16o = jax.ShapeDtypeStruct(h.shape, BF16)
    out_shape = [f32o, f32o, bf16o, bf16o, bf16o] + [f32o] * 6
    return pl.pallas_call(
        functools.partial(_rw_prep_kernel, tm=tm, nt=nt),
        grid=(b, nt),
        in_specs=in_specs,
        out_specs=[_tok_spec(tm, d)] * 11,
        out_shape=out_shape,
        compiler_params=_cparams("parallel", "arbitrary"),
        name="rw_prep",
    )(h, h, h, mod, *consts)


def _rw_prelude(refs, forward, rows):
    r_ref, kk_ref, v_ref, lw_ref, k_ref, b_ref = (ref.at[0, rows, :] for ref in refs)
    c = RW_CHUNK
    row = lax.broadcasted_iota(jnp.int32, (c, 4 * c), 0)
    col = lax.broadcasted_iota(jnp.int32, (c, 4 * c), 1) % c
    lower, upper = _tri_masks(c)
    if forward:
        strict, incl, tri, last = row > col, row >= col, lower, c - 1
    else:
        strict, incl, tri, last = row < col, row <= col, upper, 0
    tri = tri.astype(BF16)
    lw = lw_ref[...]
    hi, lo = _split(lw)
    cum = jnp.dot(tri, hi, preferred_element_type=F32) + jnp.dot(tri, lo, preferred_element_type=F32)
    tot = cum[last:last + 1, :]
    e_neg = jnp.exp(-cum)
    e_tail = jnp.exp(tot - cum)
    b_raw = b_ref[...]
    k_raw = k_ref[...]
    return dict(
        rows=rows, strict=strict, incl=incl, dec=jnp.exp(tot),
        a=(kk_ref[...] * jnp.exp(cum - lw)).astype(BF16), r=(r_ref[...] * jnp.exp(cum)).astype(BF16),
        bt=(b_raw * e_neg).astype(BF16), kt=(k_raw * e_neg).astype(BF16),
        bh=(b_raw * e_tail).astype(BF16), kh=(k_raw * e_tail).astype(BF16), v=v_ref[...])


def _rw_scan_block(pres_f, pres_b, stf_ref, stb_ref, yf_ref, yb_ref):
    c = RW_CHUNK
    rowp = lax.broadcasted_iota(jnp.int32, (c, 2 * c), 0)
    colp = lax.broadcasted_iota(jnp.int32, (c, 2 * c), 1) % c
    eye = (rowp == colp).astype(F32)
    blocks = [(rowp // s) == (colp // s) for s in (2 ** e for e in range(1, int(np.log2(c)) + 1))]
    row2 = lax.broadcasted_iota(jnp.int32, (LANES, LANES), 0)
    col2 = lax.broadcasted_iota(jnp.int32, (LANES, LANES), 1)
    same_head = (row2 < RW_HEAD) == (col2 < RW_HEAD)
    lane0 = lax.broadcasted_iota(jnp.int32, (1, LANES), 1) < RW_HEAD
    zero = jnp.zeros((c, LANES), BF16)

    def bdiag(x):
        x = x.astype(BF16)
        return jnp.concatenate([jnp.where(lane0, x, zero), jnp.where(lane0, zero, x)], axis=0)

    chains = [(pres, st_ref, y_ref, slice(p * LANES, (p + 1) * LANES), p)
              for pres, st_ref, y_ref in ((pres_f, stf_ref, yf_ref), (pres_b, stb_ref, yb_ref))
              for p in range(RW_PAIRS)]
    free = [(pre, {nm: pre[nm][:, sl] for nm in ("a", "r", "bt", "kt", "bh", "kh", "v")})
            for pres, _, _, sl, _ in chains for pre in pres]
    n = len(free)
    ar = [jnp.concatenate([t["a"], t["r"]], axis=0) for _, t in free]
    gram = [_dot_nt(ar[i], jnp.concatenate([bdiag(free[i][1]["bt"]), bdiag(free[i][1]["kt"])], axis=0))
            for i in range(n)]
    n1 = [jnp.where(free[i][0]["strict"][:, :2 * c], gram[i][:c, :2 * c], 0.0) for i in range(n)]
    a_ak = [jnp.where(free[i][0]["strict"][:, :2 * c], gram[i][:c, 2 * c:], 0.0) for i in range(n)]
    r_bk = [jnp.where(free[i][0]["incl"], gram[i][c:], 0.0) for i in range(n)]
    vbd = [bdiag(t["v"]) for _, t in free]
    w0 = [_dot(a_ak[i], vbd[i]) for i in range(n)]
    inv = [eye - jnp.where(blocks[0], n1[i], 0.0) for i in range(n)]
    for lvl in range(1, len(blocks)):
        sel = blocks[lvl] & ~blocks[lvl - 1]
        tmp = [_dot(jnp.where(sel, n1[i], 0.0), bdiag(inv[i])) for i in range(n)]
        inv = [inv[i] - _dot(inv[i], bdiag(tmp[i])) for i in range(n)]
    nsub = len(pres_f)
    zt = [st_ref[p] for _, st_ref, _, _, p in chains]
    for j in range(nsub):
        idx = [ci * nsub + j for ci in range(len(chains))]
        sz = [_dot_nt(ar[i], z) for i, z in zip(idx, zt)]
        u = [-_dot(inv[i], bdiag(s[:c] + w0[i])) for i, s in zip(idx, sz)]
        ys = [_dot(r_bk[i], jnp.concatenate([bdiag(uu), vbd[i]], axis=0)) for i, uu in zip(idx, u)]
        for (_, _, y_ref, sl, _), i, s, y in zip(chains, idx, sz, ys):
            y_ref[0, free[i][0]["rows"], sl] = (s[c:] + y).astype(BF16)
        upd = [_dot_tn(jnp.concatenate([uu.astype(BF16), free[i][1]["v"]], axis=0),
                       jnp.concatenate([free[i][1]["bh"], free[i][1]["kh"]], axis=0)) for i, uu in zip(idx, u)]
        zt = [free[i][0]["dec"][:, sl] * z + jnp.where(same_head, up, 0.0)
              for (_, _, _, sl, _), i, z, up in zip(chains, idx, zt, upd)]
    for (_, st_ref, _, _, p), z in zip(chains, zt):
        st_ref[p] = z


def _rw_scan_kernel(*refs, nblocks, sub):
    fwd_in, bwd_in = refs[0:6], refs[6:12]
    s0f_ref, s0b_ref, yf_ref, yb_ref, sf_ref, sb_ref, stf_ref, stb_ref = refs[12:]
    i = pl.program_id(1)
    c = RW_CHUNK

    @pl.when(i == 0)
    def _():
        stf_ref[...] = s0f_ref[0]
        stb_ref[...] = s0b_ref[0]

    pres_f = [_rw_prelude(fwd_in, True, slice(j * c, (j + 1) * c)) for j in range(sub)]
    pres_b = [_rw_prelude(bwd_in, False, slice(j * c, (j + 1) * c)) for j in range(sub - 1, -1, -1)]
    _rw_scan_block(pres_f, pres_b, stf_ref, stb_ref, yf_ref, yb_ref)

    @pl.when(i == nblocks - 1)
    def _():
        sf_ref[0] = stf_ref[...]
        sb_ref[0] = stb_ref[...]


def _rw_scan(r, kk, v, dirs, s0f, s0b):
    b, s, d = r.shape
    sub = RW_CHUNKS_PER_STEP if s % (RW_CHUNKS_PER_STEP * RW_CHUNK) == 0 else 1
    rows = sub * RW_CHUNK
    nblocks = s // rows
    fwd = pl.BlockSpec((1, rows, d), lambda bi, i: (bi, i, 0))
    bwd = pl.BlockSpec((1, rows, d), lambda bi, i: (bi, nblocks - 1 - i, 0))
    st_shape = (RW_PAIRS, LANES, LANES)
    st_spec = pl.BlockSpec((1,) + st_shape, lambda bi, i: (bi, 0, 0, 0))
    (lw0, k0, b0), (lw1, k1, b1) = dirs
    return pl.pallas_call(
        functools.partial(_rw_scan_kernel, nblocks=nblocks, sub=sub),
        grid=(b, nblocks),
        in_specs=[fwd] * 6 + [bwd] * 6 + [st_spec, st_spec],
        out_specs=[fwd, bwd, st_spec, st_spec],
        out_shape=[jax.ShapeDtypeStruct((b, s, d), BF16)] * 2 + [jax.ShapeDtypeStruct((b,) + st_shape, F32)] * 2,
        scratch_shapes=[pltpu.VMEM(st_shape, F32), pltpu.VMEM(st_shape, F32)],
        compiler_params=_cparams("parallel", "arbitrary"),
        name="rw_scan",
    )(r, kk, v, lw0, k0, b0, r, kk, v, lw1, k1, b1, s0f, s0b)


def _rw_finish_prologue(yf_ref, yb_ref, bonus_ref, g_ref, gs_ref, gt_ref, gng_ref, gnb_ref):
    y = yf_ref[0].astype(F32) + yb_ref[0].astype(F32)
    gs = gs_ref[...]
    gt = gt_ref[...]
    inv_n = 1.0 / RW_HEAD
    mean = _dot_sel(_dot_sel(y, gs) * inv_n, gt)
    yc = y - mean
    var = _dot_sel(yc * yc, gs) * inv_n
    rstd = _dot_sel(lax.rsqrt(var + RW_GN_EPS), gt)
    yn = yc * rstd * gng_ref[...] + gnb_ref[...]
    return ((yn + bonus_ref[0].astype(F32)) * g_ref[0].astype(F32)).astype(BF16)


def _mixer_rwkv(hx, hc, mx, mc, mu, w_rkv, w0, w1, w2, a0, a1, a2, g1, g2, k_k, k_a, r_k, gn_g, gn_b, w_o,
                ln_g, ln_b, ctx_out):
    b, n, d = hx.shape
    gs, gt = _head_sum_matrices()
    rank_w, rank_a = w1.shape[-1], a1.shape[-1]
    assert 2 * rank_w == LANES and 2 * rank_a == LANES

    def ext(m2):
        z = jnp.zeros((2, LANES, d), F32)
        z = z.at[0, :m2.shape[1]].set(m2[0])
        return z.at[1, m2.shape[1]:].set(m2[1]).astype(BF16)

    consts = [mu, w_rkv[0].astype(BF16), w_rkv[1].astype(BF16), w_rkv[2].astype(BF16),
              g1.astype(BF16), g2.astype(BF16),
              jnp.concatenate([w1[0], w1[1]], axis=1).astype(BF16), ext(w2),
              jnp.concatenate([a1[0], a1[1]], axis=1).astype(BF16), ext(a2),
              w0, a0, k_k.reshape(1, d), k_a.reshape(1, d), r_k.reshape(1, d), gs, gt]
    w_o = w_o.astype(BF16)

    def prep(h, mod):
        r, kk, v, g, bonus, lw0, k0, b0, lw1, k1, b1 = _rw_prep(h, mod, consts)
        return r, kk, v, g, bonus, ((lw0, k0, b0), (lw1, k1, b1))

    rc, kkc, vc, gc, bonc, dc = prep(hc, mc)
    rx, kkx, vx, gx, bonx, dx = prep(hx, mx)
    s0 = jnp.zeros((b, RW_PAIRS, LANES, LANES), F32)
    ycf, ycb, scf, scb = _rw_scan(rc, kkc, vc, dc, s0, s0)
    yxf, yxb, _, _ = _rw_scan(rx, kkx, vx, dx, scf, scb)
    fin_consts = [gs, gt, gn_g.reshape(1, d), gn_b.reshape(1, d)]
    hx = _out_proj([yxf, yxb, bonx, gx], fin_consts, _rw_finish_prologue, w_o, None, hx, mx, ln_g, ln_b, "rw_out",
                   tile=RW_PREP_TILE)
    if ctx_out:
        hc = _out_proj([ycf, ycb, bonc, gc], fin_consts, _rw_finish_prologue, w_o, None, hc, mc, ln_g, ln_b,
                       "rw_out_ctx", tile=RW_PREP_TILE)
    return hx, hc


def kernel(x, c, ctx, c_ctx, ada_w, ada_b, ln_g, ln_b, ffn_w13, ffn_w2, na_wqkv, na_wo, na_rpb, cv_w1, cv_b1, cv_wdw, cv_bdw, cv_ln_g, cv_ln_b, cv_w2, cv_b2, gla_win, gla_wa1, gla_wa2, gla_ba, gla_norm_g, gla_wo, rw_mu, rw_wrkv, rw_w0, rw_w1, rw_w2, rw_a0, rw_a1, rw_a2, rw_g1, rw_g2, rw_kk, rw_ka, rw_rk, rw_gn_g, rw_gn_b, rw_wo):
    b, n, d = x.shape
    depth = ada_w.shape[0]
    assert d == D_MODEL and depth == DEPTH and n % GRID_W == 0
    rows = -(-(b + 1) // 8) * 8
    cond = jnp.zeros((rows, d), F32).at[:b].set(c).at[b].set(c_ctx)
    mod_all = _ada(cond, ada_w, ada_b)
    w13 = ffn_w13.astype(BF16)
    w2 = ffn_w2.astype(BF16)
    hx, hc = x, ctx
    for i in range(depth):
        last = i == depth - 1
        mx = mod_all[i, :b].reshape(b, 3 * N_SUB, d)
        mc = jnp.broadcast_to(mod_all[i, b].reshape(1, 3 * N_SUB, d), (b, 3 * N_SUB, d))
        hx = _ffn(hx, mx, w13, w2, i, 0, ln_g[i, 0], ln_b[i, 0], 0)
        hc = _ffn(hc, mc, w13, w2, i, 0, ln_g[i, 0], ln_b[i, 0], 0)
        kind, j = i % 4, i // 4
        lg, lb = ln_g[i, 1], ln_b[i, 1]
        if kind == 0:
            hx, hc = _mixer_na(hx, hc, mx, mc, na_wqkv[j], na_wo[j], na_rpb[j], lg, lb, not last)
        elif kind == 1:
            hx, hc = _mixer_conv(hx, hc, mx, mc, cv_w1[j], cv_b1[j], cv_wdw[j], cv_bdw[j], cv_ln_g[j], cv_ln_b[j],
                                 cv_w2[j], cv_b2[j], lg, lb, not last)
        elif kind == 2:
            hx, hc = _mixer_gla(hx, hc, mx, mc, gla_win[j], gla_wa1[j], gla_wa2[j], gla_ba[j], gla_norm_g[j],
                                gla_wo[j], lg, lb, not last)
        else:
            hx, hc = _mixer_rwkv(hx, hc, mx, mc, rw_mu[j], rw_wrkv[j], rw_w0[j], rw_w1[j], rw_w2[j], rw_a0[j],
                                 rw_a1[j], rw_a2[j], rw_g1[j], rw_g2[j], rw_kk[j], rw_ka[j], rw_rk[j],
                                 rw_gn_g[j], rw_gn_b[j], rw_wo[j], lg, lb, not last)
        hx = _ffn(hx, mx, w13, w2, i, 1, ln_g[i, 2], ln_b[i, 2], 2)
        if not last:
            hc = _ffn(hc, mc, w13, w2, i, 1, ln_g[i, 2], ln_b[i, 2], 2)
    return hx
```

```python
import functools

import numpy as np
import jax
import jax.numpy as jnp
from jax import lax
from jax.experimental import pallas as pl
from jax.experimental.pallas import tpu as pltpu

F32 = jnp.float32
BF16 = jnp.bfloat16

D_MODEL = 1024
DEPTH = 4
N_SUB = 3
GRID_W = 64
ALPHA = (2.0 * DEPTH) ** 0.25
LN_EPS = 1e-5
D_FF = 2816
NA_HEADS = 16
NA_HEAD_DIM = D_MODEL // NA_HEADS
NA_KH = 8
NA_KW = 16
CONV_WIDTH = 31
CONV_HALO = 16
GLA_HEADS = 4
GLA_DK = 128
GLA_DV = 256
GLA_GATE_RANK = 16
GLA_NORMALIZER = 16.0
GLA_CHUNK = 64
GLA_CHUNKS_PER_STEP = 2
ROPE_BASE = 10000.0
RW_HEAD = 64
RW_HEADS = D_MODEL // RW_HEAD
RW_PAIRS = RW_HEADS // 2
RW_CHUNK = 64
RW_CHUNKS_PER_STEP = 1
RW_GN_EPS = 64e-5
LANES = 128
SUBLANES = 8
NEG_BIG = -1e30
VMEM_LIMIT = 56 * 1024 * 1024
FFN_TILE = 512
PROJ_TILE = 512
CONV_TILE = 256
RW_PREP_TILE = 256


def _cparams(*sem):
    return pltpu.CompilerParams(dimension_semantics=sem, vmem_limit_bytes=VMEM_LIMIT)


def _dot(a, b):
    return jnp.dot(a.astype(BF16), b.astype(BF16), preferred_element_type=F32)


def _dot_nt(a, b):
    return lax.dot_general(a.astype(BF16), b.astype(BF16), (((1,), (1,)), ((), ())),
                           preferred_element_type=F32)


def _dot_tn(a, b):
    return lax.dot_general(a.astype(BF16), b.astype(BF16), (((0,), (0,)), ((), ())),
                           preferred_element_type=F32)


def _split(x):
    hi = x.astype(BF16)
    lo = (x - hi.astype(F32)).astype(BF16)
    return hi, lo


def _dot_sel(x, sel):
    hi, lo = _split(x)
    return (jnp.dot(hi, sel, preferred_element_type=F32)
            + jnp.dot(lo, sel, preferred_element_type=F32))


def _sel_dot(sel, x):
    hi, lo = _split(x)
    return (jnp.dot(sel, hi, preferred_element_type=F32)
            + jnp.dot(sel, lo, preferred_element_type=F32))


def _sigmoid(x):
    return jax.nn.sigmoid(x)


def _softplus(x):
    return jnp.maximum(x, 0.0) + jnp.log(1.0 + jnp.exp(-jnp.abs(x)))


def _layer_norm(x, g, b, eps=LN_EPS):
    mu = jnp.mean(x, axis=-1, keepdims=True)
    xc = x - mu
    var = jnp.mean(xc * xc, axis=-1, keepdims=True)
    return xc * lax.rsqrt(var + eps) * g + b


def _mod_rows(mod_ref, j):
    return (mod_ref[0, 3 * j:3 * j + 1, :], mod_ref[0, 3 * j + 1:3 * j + 2, :],
            mod_ref[0, 3 * j + 2:3 * j + 3, :])


def _modulate(h, mod_ref, j):
    shift, scale, _ = _mod_rows(mod_ref, j)
    return h * (1.0 + scale) + shift


def _post_norm(h, y, mod_ref, j, g_ref, b_ref):
    gate = mod_ref[0, 3 * j + 2:3 * j + 3, :]
    return _layer_norm(ALPHA * h + gate * y, g_ref[...], b_ref[...])


def _tok_spec(tm, d):
    return pl.BlockSpec((1, tm, d), lambda b, t: (b, t, 0))


def _mod_spec():
    return pl.BlockSpec((1, 3 * N_SUB, D_MODEL), lambda b, t: (b, 0, 0))


def _const_spec(shape):
    nd = len(shape)
    return pl.BlockSpec(shape, lambda b, t: (0,) * nd, pipeline_mode=pl.Buffered(1))


def _tile(s, pref):
    tm = min(pref, s)
    assert s % tm == 0
    return tm


def _ada_kernel(cond_ref, w_ref, b_ref, o_ref):
    c = cond_ref[...]
    o_ref[0] = _dot(c * _sigmoid(c), w_ref[0]) + b_ref[0]


def _ada(cond, ada_w, ada_b):
    r, d = cond.shape
    depth, _, n = ada_w.shape
    tn = 1024
    return pl.pallas_call(
        _ada_kernel,
        grid=(depth, n // tn),
        in_specs=[pl.BlockSpec((r, d), lambda l, j: (0, 0)),
                  pl.BlockSpec((1, d, tn), lambda l, j: (l, 0, j)),
                  pl.BlockSpec((1, 1, tn), lambda l, j: (l, 0, j))],
        out_specs=pl.BlockSpec((1, r, tn), lambda l, j: (l, 0, j)),
        out_shape=jax.ShapeDtypeStruct((depth, r, n), F32),
        compiler_params=_cparams("parallel", "parallel"),
        name="ada_mod",
    )(cond, ada_w, ada_b.reshape(depth, 1, n))


FFN_CHUNK = 256


def _ffn_tile(h_ref, mod_ref, w13_ref, w2_ref, g_ref, b_ref, o_ref, j):
    h = h_ref[0]
    u = _modulate(h, mod_ref, j).astype(BF16)
    acc = jnp.zeros(h.shape, F32)
    for lo in range(0, D_FF, FFN_CHUNK):
        hi = min(lo + FFN_CHUNK, D_FF)
        a = jnp.dot(u, w13_ref[:, lo:hi], preferred_element_type=F32)
        v = jnp.dot(u, w13_ref[:, D_FF + lo:D_FF + hi], preferred_element_type=F32)
        z = (a * _sigmoid(a) * v).astype(BF16)
        acc = acc + jnp.dot(z, w2_ref[lo:hi, :], preferred_element_type=F32)
    o_ref[0] = _post_norm(h, 0.5 * acc, mod_ref, j, g_ref, b_ref)


def _ffn_kernel(h_ref, mod_ref, w13_ref, w2_ref, g_ref, b_ref, o_ref, *, j):
    _ffn_tile(h_ref, mod_ref, w13_ref, w2_ref, g_ref, b_ref, o_ref, j)


def _ffn_pair_kernel(hx_ref, hc_ref, mx_ref, mc_ref, w13_ref, w2_ref, g_ref, b_ref, ox_ref, oc_ref, *, j, nx):
    t = pl.program_id(1)

    @pl.when(t < nx)
    def _():
        _ffn_tile(hx_ref, mx_ref, w13_ref, w2_ref, g_ref, b_ref, ox_ref, j)

    @pl.when(t == nx)
    def _():
        _ffn_tile(hc_ref, mc_ref, w13_ref, w2_ref, g_ref, b_ref, oc_ref, j)


def _ffn_weight_spec(shape, layer, half):
    return pl.BlockSpec((None, None) + shape[2:], lambda bi, t: (layer, half, 0, 0), pipeline_mode=pl.Buffered(1))


def _ffn_pair(hx, hc, mx, mc, w13_all, w2_all, layer, half, ln_g, ln_b, j):
    b, n, d = hx.shape
    l = hc.shape[1]
    tm = _tile(n, FFN_TILE)
    nx = n // tm
    x_spec = pl.BlockSpec((1, tm, d), lambda bi, t: (bi, jnp.minimum(t, nx - 1), 0))
    c_spec = pl.BlockSpec((1, l, d), lambda bi, t: (bi, 0, 0))
    return pl.pallas_call(
        functools.partial(_ffn_pair_kernel, j=j, nx=nx),
        grid=(b, nx + 1),
        in_specs=[x_spec, c_spec, _mod_spec(), _mod_spec(), _ffn_weight_spec(w13_all.shape, layer, half),
                  _ffn_weight_spec(w2_all.shape, layer, half), _const_spec((1, d)), _const_spec((1, d))],
        out_specs=[x_spec, c_spec],
        out_shape=[jax.ShapeDtypeStruct(hx.shape, F32), jax.ShapeDtypeStruct(hc.shape, F32)],
        compiler_params=_cparams("parallel", "arbitrary"),
        name="ffn_pair",
    )(hx, hc, mx, mc, w13_all, w2_all, ln_g.reshape(1, d), ln_b.reshape(1, d))


def _ffn(h, mod, w13_all, w2_all, layer, half, ln_g, ln_b, j):
    b, s, d = h.shape
    tm = _tile(s, FFN_TILE)
    pick = lambda shape: _ffn_weight_spec(shape, layer, half)
    return pl.pallas_call(
        functools.partial(_ffn_kernel, j=j),
        grid=(b, s // tm),
        in_specs=[_tok_spec(tm, d), _mod_spec(), pick(w13_all.shape), pick(w2_all.shape),
                  _const_spec((1, d)), _const_spec((1, d))],
        out_specs=_tok_spec(tm, d),
        out_shape=jax.ShapeDtypeStruct(h.shape, F32),
        compiler_params=_cparams("parallel", "parallel"),
        name="ffn_half",
    )(h, mod, w13_all, w2_all, ln_g.reshape(1, d), ln_b.reshape(1, d))


def _out_kernel(*refs, n_in, prologue, has_bias):
    ins = refs[:n_in]
    w_ref = refs[n_in]
    k = n_in + 1
    bias_ref = None
    if has_bias:
        bias_ref = refs[k]
        k += 1
    h_ref, mod_ref, g_ref, b_ref, o_ref = refs[k:k + 5]
    y = jnp.dot(prologue(*ins), w_ref[...], preferred_element_type=F32)
    if has_bias:
        y = y + bias_ref[...]
    o_ref[0] = _post_norm(h_ref[0], y, mod_ref, 1, g_ref, b_ref)


def _out_proj(tok_inputs, const_inputs, prologue, w, bias, h, mod, ln_g, ln_b, name, tile=PROJ_TILE):
    b, s, d = h.shape
    tm = _tile(s, tile)
    n_in = len(tok_inputs) + len(const_inputs)
    in_specs = [_tok_spec(tm, a.shape[-1]) for a in tok_inputs]
    in_specs += [_const_spec(a.shape) for a in const_inputs]
    in_specs.append(_const_spec(w.shape))
    args = list(tok_inputs) + list(const_inputs) + [w]
    if bias is not None:
        in_specs.append(_const_spec((1, d)))
        args.append(bias.reshape(1, d))
    in_specs += [_tok_spec(tm, d), _mod_spec(), _const_spec((1, d)), _const_spec((1, d))]
    args += [h, mod, ln_g.reshape(1, d), ln_b.reshape(1, d)]
    return pl.pallas_call(
        functools.partial(_out_kernel, n_in=n_in, prologue=prologue, has_bias=bias is not None),
        grid=(b, s // tm),
        in_specs=in_specs,
        out_specs=_tok_spec(tm, d),
        out_shape=jax.ShapeDtypeStruct(h.shape, F32),
        compiler_params=_cparams("parallel", "parallel"),
        name=name,
    )(*args)


def _plain_prologue(y_ref):
    return y_ref[0].astype(BF16)


def _na_qkv_kernel(h_ref, mod_ref, w_ref, q_ref, k_ref, v_ref):
    d = D_MODEL
    u = _modulate(h_ref[0], mod_ref, 1).astype(BF16)
    z = jnp.dot(u, w_ref[...], preferred_element_type=F32)
    q_ref[0] = (z[:, :d] * (NA_HEAD_DIM ** -0.5)).astype(BF16)
    k_ref[0] = z[:, d:2 * d].astype(BF16)
    v_ref[0] = z[:, 2 * d:].astype(BF16)


def _na_qkv(h, mod, w):
    b, s, d = h.shape
    tm = _tile(s, PROJ_TILE)
    out = jax.ShapeDtypeStruct(h.shape, BF16)
    return pl.pallas_call(
        _na_qkv_kernel,
        grid=(b, s // tm),
        in_specs=[_tok_spec(tm, d), _mod_spec(), _const_spec(w.shape)],
        out_specs=[_tok_spec(tm, d)] * 3,
        out_shape=[out] * 3,
        compiler_params=_cparams("parallel", "parallel"),
        name="na_qkv",
    )(h, mod, w)


def _na_row_start(r, rows, kh):
    return jnp.clip(r - kh // 2, 0, rows - kh)


def _na_attn_kernel(q_ref, k_ref, v_ref, kc_ref, vc_ref, bias_ref, o_ref, *, rows, kh):
    r = pl.program_id(1)
    start = pl.multiple_of(_na_row_start(r, rows, kh) * GRID_W, GRID_W)
    win = kh * GRID_W
    lane0 = lax.broadcasted_iota(jnp.int32, (1, LANES), 1) < NA_HEAD_DIM
    zero = jnp.zeros((GRID_W, LANES), BF16)
    pairs = range(NA_HEADS // 2)
    sls = [slice(p * LANES, (p + 1) * LANES) for p in pairs]
    qbd = []
    for sl in sls:
        q = q_ref[0, :, sl]
        qbd.append(jnp.concatenate([jnp.where(lane0, q, zero), jnp.where(lane0, zero, q)], axis=0))
    s = _na_row_start(r, rows, kh) - r + NA_KH - 1
    bias = [jnp.concatenate([bias_ref[s + a, p] for a in range(kh)], axis=0) for p in pairs]
    s_loc = [_dot_nt(k_ref[0, pl.ds(start, win), sls[p]], qbd[p]) + bias[p] for p in pairs]
    s_ctx = [_dot_nt(kc_ref[0, :, sls[p]], qbd[p]) for p in pairs]
    m = [jnp.maximum(jnp.max(s_loc[p], axis=0, keepdims=True), jnp.max(s_ctx[p], axis=0, keepdims=True))
         for p in pairs]
    p_loc = [jnp.exp(s_loc[p] - m[p]) for p in pairs]
    p_ctx = [jnp.exp(s_ctx[p] - m[p]) for p in pairs]
    inv = [1.0 / (jnp.sum(p_loc[p], axis=0, keepdims=True) + jnp.sum(p_ctx[p], axis=0, keepdims=True))
           for p in pairs]
    o = [_dot_tn(p_loc[p] * inv[p], v_ref[0, pl.ds(start, win), sls[p]])
         + _dot_tn(p_ctx[p] * inv[p], vc_ref[0, :, sls[p]]) for p in pairs]
    for p in pairs:
        o_ref[0, :, sls[p]] = jnp.where(lane0, o[p][:GRID_W], o[p][GRID_W:]).astype(BF16)


def _na_bias_table(rpb):
    w = GRID_W
    nrow = 2 * NA_KH - 1
    qc = np.arange(w)[:, None]
    kc = np.arange(w)[None, :]
    cstart = np.clip(qc - NA_KW // 2, 0, w - NA_KW)
    ok = (kc >= cstart) & (kc < cstart + NA_KW)
    lpad = w - NA_KW
    g = jnp.pad(rpb, ((0, 0), (0, 0), (lpad, lpad)))
    flat = jnp.broadcast_to(g[:, :, None, :], (NA_HEADS, nrow, w, 2 * w - 1)).reshape(NA_HEADS, nrow, -1)
    t = flat[:, :, w - 1:w - 1 + w * (2 * w - 2)].reshape(NA_HEADS, nrow, w, 2 * w - 2)[..., :w]
    tbl = jnp.where(ok[None, None], t, NEG_BIG)
    tbl = tbl.reshape(NA_HEADS // 2, 2, nrow, w, w)
    return jnp.transpose(tbl, (2, 0, 4, 1, 3)).reshape(nrow, NA_HEADS // 2, w, 2 * w)


def _na_attn(q, k, v, kc, vc, bias_tab):
    b, n, d = q.shape
    l = kc.shape[1]
    rows = n // GRID_W
    kh = min(NA_KH, rows)
    return pl.pallas_call(
        functools.partial(_na_attn_kernel, rows=rows, kh=kh),
        grid=(b, rows),
        in_specs=[pl.BlockSpec((1, GRID_W, d), lambda bi, r: (bi, r, 0)),
                  pl.BlockSpec((1, n, d), lambda bi, r: (bi, 0, 0)),
                  pl.BlockSpec((1, n, d), lambda bi, r: (bi, 0, 0)),
                  pl.BlockSpec((1, l, d), lambda bi, r: (bi, 0, 0)),
                  pl.BlockSpec((1, l, d), lambda bi, r: (bi, 0, 0)),
                  _const_spec(bias_tab.shape)],
        out_specs=pl.BlockSpec((1, GRID_W, d), lambda bi, r: (bi, r, 0)),
        out_shape=jax.ShapeDtypeStruct(q.shape, BF16),
        compiler_params=_cparams("parallel", "arbitrary"),
        name="na_attn",
    )(q, k, v, kc, vc, bias_tab)


def _ctx_attn_kernel(q_ref, k_ref, v_ref, o_ref):
    for hd in range(NA_HEADS):
        sl = slice(hd * NA_HEAD_DIM, (hd + 1) * NA_HEAD_DIM)
        s = _dot_nt(q_ref[0, :, sl], k_ref[0, :, sl])
        p = jnp.exp(s - jnp.max(s, axis=-1, keepdims=True))
        den = jnp.sum(p, axis=-1, keepdims=True)
        o_ref[0, :, sl] = (_dot(p, v_ref[0, :, sl]) / den).astype(BF16)


def _ctx_attn(q, k, v):
    b, l, d = q.shape
    spec = pl.BlockSpec((1, l, d), lambda bi: (bi, 0, 0))
    return pl.pallas_call(
        _ctx_attn_kernel,
        grid=(b,),
        in_specs=[spec] * 3,
        out_specs=spec,
        out_shape=jax.ShapeDtypeStruct(q.shape, BF16),
        compiler_params=_cparams("parallel"),
        name="ctx_attn",
    )(q, k, v)


def _mixer_na(hx, hc, mx, mc, w_qkv, w_o, rpb, ln_g, ln_b, ctx_out):
    w_qkv = w_qkv.astype(BF16)
    w_o = w_o.astype(BF16)
    qx, kx, vx = _na_qkv(hx, mx, w_qkv)
    qc, kc, vc = _na_qkv(hc, mc, w_qkv)
    ox = _na_attn(qx, kx, vx, kc, vc, _na_bias_table(rpb))
    hx = _out_proj([ox], [], _plain_prologue, w_o, None, hx, mx, ln_g, ln_b, "na_out")
    if ctx_out:
        oc = _ctx_attn(qc, kc, vc)
        hc = _out_proj([oc], [], _plain_prologue, w_o, None, hc, mc, ln_g, ln_b, "na_out_ctx")
    return hx, hc


def _conv_in_kernel(h_ref, mod_ref, w_ref, b_ref, o_ref):
    d = D_MODEL
    u = _modulate(h_ref[0], mod_ref, 1).astype(BF16)
    z = jnp.dot(u, w_ref[...], preferred_element_type=F32) + b_ref[...]
    o_ref[0] = z[:, :d] * _sigmoid(z[:, d:])


def _conv_in(h, mod, w1, b1):
    b, s, d = h.shape
    tm = _tile(s, PROJ_TILE)
    return pl.pallas_call(
        _conv_in_kernel,
        grid=(b, s // tm),
        in_specs=[_tok_spec(tm, d), _mod_spec(), _const_spec(w1.shape), _const_spec((1, 2 * d))],
        out_specs=_tok_spec(tm, d),
        out_shape=jax.ShapeDtypeStruct(h.shape, F32),
        compiler_params=_cparams("parallel", "parallel"),
        name="conv_in",
    )(h, mod, w1, b1.reshape(1, 2 * d))


CONV_ROWS = 16


def _conv_dw_kernel(prev_ref, cur_ref, next_ref, w_ref, bdw_ref, g_ref, b_ref, o_ref, win_ref, *, ts, nt):
    t = pl.program_id(1)
    halo = CONV_HALO
    d = cur_ref.shape[-1]
    rows = ts + 2 * halo
    win_ref[0, halo:halo + ts, :] = cur_ref[0]
    win_ref[0, 0:halo, :] = jnp.where(t > 0, prev_ref[0], 0.0)
    win_ref[0, halo + ts:rows, :] = jnp.where(t < nt - 1, next_ref[0], 0.0)
    sub = lax.broadcasted_iota(jnp.int32, (rows // SUBLANES, SUBLANES, LANES), 1)
    for cc in range(d // LANES):
        cols = slice(cc * LANES, (cc + 1) * LANES)
        w = win_ref[0, :, cols].reshape(rows // SUBLANES, SUBLANES, LANES)
        for i in range(1, SUBLANES):
            rot = pltpu.roll(w, SUBLANES - i, axis=1)
            nxt = jnp.concatenate([rot[1:], rot[:1]], axis=0)
            win_ref[i, :, cols] = jnp.where(sub < SUBLANES - i, rot, nxt).reshape(rows, LANES)
    off = halo - CONV_WIDTH // 2
    slabs = CONV_ROWS // SUBLANES
    for g in range(ts // CONV_ROWS):
        acc = jnp.broadcast_to(bdw_ref[...], (slabs, SUBLANES, d))
        for k in range(CONV_WIDTH):
            i, base = (off + k) % SUBLANES, g * CONV_ROWS + (off + k) // SUBLANES * SUBLANES
            acc = acc + win_ref[i, base:base + CONV_ROWS, :].reshape(slabs, SUBLANES, d) * w_ref[k][None]
        y = _layer_norm(acc.reshape(CONV_ROWS, d), g_ref[...], b_ref[...])
        o_ref[0, g * CONV_ROWS:(g + 1) * CONV_ROWS, :] = (y * _sigmoid(y)).astype(BF16)


def _conv_dw(u, w_dw, b_dw, ln_g, ln_b):
    b, s, d = u.shape
    ts = _tile(s, CONV_TILE)
    nt = s // ts
    hb = ts // CONV_HALO
    nhb = s // CONV_HALO
    w_rep = jnp.broadcast_to(w_dw[:, None, :], (CONV_WIDTH, SUBLANES, d))
    return pl.pallas_call(
        functools.partial(_conv_dw_kernel, ts=ts, nt=nt),
        grid=(b, nt),
        in_specs=[pl.BlockSpec((1, CONV_HALO, d), lambda bi, t: (bi, jnp.maximum(t * hb - 1, 0), 0)),
                  _tok_spec(ts, d),
                  pl.BlockSpec((1, CONV_HALO, d), lambda bi, t: (bi, jnp.minimum((t + 1) * hb, nhb - 1), 0)),
                  _const_spec(w_rep.shape), _const_spec((1, d)), _const_spec((1, d)), _const_spec((1, d))],
        out_specs=_tok_spec(ts, d),
        out_shape=jax.ShapeDtypeStruct(u.shape, BF16),
        scratch_shapes=[pltpu.VMEM((SUBLANES, ts + 2 * CONV_HALO, d), F32)],
        compiler_params=_cparams("parallel", "arbitrary"),
        name="conv_dw",
    )(u, u, u, w_rep, b_dw.reshape(1, d), ln_g.reshape(1, d), ln_b.reshape(1, d))


def _mixer_conv(hx, hc, mx, mc, w1, b1, w_dw, b_dw, cln_g, cln_b, w2, b2, ln_g, ln_b, ctx_out):
    w1 = w1.astype(BF16)
    w2 = w2.astype(BF16)

    def branch(h, mod, name):
        u = _conv_in(h, mod, w1, b1)
        y = _conv_dw(u, w_dw, b_dw, cln_g, cln_b)
        return _out_proj([y], [], _plain_prologue, w2, b2, h, mod, ln_g, ln_b, name)

    hx = branch(hx, mx, "conv_out")
    if ctx_out:
        hc = branch(hc, mc, "conv_out_ctx")
    return hx, hc


GLA_NQK = GLA_HEADS * GLA_DK


def _rope_swap(x):
    quarter = GLA_DK // 4
    lane = lax.broadcasted_iota(jnp.int32, (1, GLA_DK), 1)
    first = (lane % (2 * quarter)) < quarter
    parts = []
    for hd in range(GLA_HEADS):
        xh = x[:, hd * GLA_DK:(hd + 1) * GLA_DK]
        up = pltpu.roll(xh, GLA_DK - quarter, axis=1)
        dn = pltpu.roll(xh, quarter, axis=1)
        parts.append(jnp.where(first, up, dn))
    return jnp.concatenate(parts, axis=1)


def _gla_in_kernel(*refs, rotary):
    if rotary:
        h_ref, mod_ref, w_ref, wa2_ref, ba_ref, cos_ref, sin_ref = refs[:7]
        outs = refs[7:]
    else:
        h_ref, mod_ref, w_ref, wa2_ref, ba_ref = refs[:5]
        outs = refs[5:]
    q_ref, k_ref, v_ref, g_ref, laf_ref, lab_ref = outs
    n, d = GLA_NQK, D_MODEL
    u = _modulate(h_ref[0], mod_ref, 1).astype(BF16)
    z = jnp.dot(u, w_ref[...], preferred_element_type=F32)
    q = z[:, :n]
    k = z[:, n:2 * n]
    o = 2 * n
    if rotary:
        cos = cos_ref[...]
        sin = sin_ref[...]
        q = q * cos + _rope_swap(q) * sin
        k = k * cos + _rope_swap(k) * sin
    q_ref[0] = q * (GLA_DK ** -0.5)
    k_ref[0] = k
    v_ref[0] = z[:, o:o + d].astype(BF16)
    g_ref[0] = z[:, o + d:o + 2 * d].astype(BF16)
    t = z[:, o + 2 * d:o + 2 * d + LANES].astype(BF16)
    for dr, la_ref in enumerate((laf_ref, lab_ref)):
        x = jnp.dot(t, wa2_ref[dr], preferred_element_type=F32) + ba_ref[dr:dr + 1, :]
        la_ref[0] = -_softplus(-x) * (1.0 / GLA_NORMALIZER)


def _gla_rope_tables(n):
    quarter = GLA_DK // 4
    t = np.arange(n)
    freqs = ROPE_BASE ** (-jnp.arange(quarter, dtype=F32) / quarter)
    ang_r = jnp.asarray(t // GRID_W, F32)[:, None] * freqs
    ang_c = jnp.asarray(t % GRID_W, F32)[:, None] * freqs
    cos = jnp.concatenate([jnp.cos(ang_r), jnp.cos(ang_r), jnp.cos(ang_c), jnp.cos(ang_c)], axis=-1)
    sin = jnp.concatenate([-jnp.sin(ang_r), jnp.sin(ang_r), -jnp.sin(ang_c), jnp.sin(ang_c)], axis=-1)
    return jnp.tile(cos, (1, GLA_HEADS)), jnp.tile(sin, (1, GLA_HEADS))


def _gla_in(h, mod, w_ext, wa2_ext, ba, rope):
    b, s, d = h.shape
    tm = _tile(s, PROJ_TILE)
    rotary = rope is not None
    in_specs = [_tok_spec(tm, d), _mod_spec(), _const_spec(w_ext.shape), _const_spec(wa2_ext.shape),
                _const_spec(ba.shape)]
    args = [h, mod, w_ext, wa2_ext, ba]
    if rotary:
        in_specs += [pl.BlockSpec((tm, GLA_NQK), lambda bi, t: (t, 0))] * 2
        args += list(rope)
    shp = lambda w, dt: jax.ShapeDtypeStruct((b, s, w), dt)
    return pl.pallas_call(
        functools.partial(_gla_in_kernel, rotary=rotary),
        grid=(b, s // tm),
        in_specs=in_specs,
        out_specs=[_tok_spec(tm, GLA_NQK), _tok_spec(tm, GLA_NQK), _tok_spec(tm, d), _tok_spec(tm, d),
                   _tok_spec(tm, GLA_NQK), _tok_spec(tm, GLA_NQK)],
        out_shape=[shp(GLA_NQK, F32), shp(GLA_NQK, F32), shp(d, BF16), shp(d, BF16),
                   shp(GLA_NQK, F32), shp(GLA_NQK, F32)],
        compiler_params=_cparams("parallel", "parallel"),
        name="gla_in_rope" if rotary else "gla_in",
    )(*args)


def _tri_masks(c):
    row = lax.broadcasted_iota(jnp.int32, (c, c), 0)
    col = lax.broadcasted_iota(jnp.int32, (c, c), 1)
    return row >= col, row <= col


def _gla_core_kernel(qf_ref, kf_ref, vf_ref, laf_ref, qb_ref, kb_ref, vb_ref, lab_ref, s0f_ref, s0b_ref,
                     of_ref, ob_ref, sf_ref, sb_ref, stf_ref, stb_ref, *, nblocks, sub):
    i = pl.program_id(1)
    c = GLA_CHUNK

    @pl.when(i == 0)
    def _():
        stf_ref[...] = s0f_ref[0]
        stb_ref[...] = s0b_ref[0]

    lower, upper = _tri_masks(c)
    dirs = ((qf_ref, kf_ref, vf_ref, laf_ref, of_ref, stf_ref, lower, c - 1, range(sub)),
            (qb_ref, kb_ref, vb_ref, lab_ref, ob_ref, stb_ref, upper, 0, range(sub - 1, -1, -1)))
    chains = []
    for q_ref, k_ref, v_ref, la_ref, o_ref, st_ref, mask, last, order in dirs:
        tri = mask.astype(BF16)
        steps = []
        for j in order:
            rows = slice(j * c, (j + 1) * c)
            hi, lo = _split(la_ref[0, rows, :])
            bcum = jnp.dot(tri, hi, preferred_element_type=F32) + jnp.dot(tri, lo, preferred_element_type=F32)
            blast = bcum[last:last + 1, :]
            k = k_ref[0, rows, :]
            steps.append(dict(rows=rows, dec=jnp.exp(blast), qd=(q_ref[0, rows, :] * jnp.exp(bcum)).astype(BF16),
                              kd=(k * jnp.exp(-bcum)).astype(BF16), kt=(k * jnp.exp(blast - bcum)).astype(BF16)))
        for hd in range(GLA_HEADS):
            ks = slice(hd * GLA_DK, (hd + 1) * GLA_DK)
            vs = slice(hd * GLA_DV, (hd + 1) * GLA_DV)
            chains.append(dict(
                mask=mask, o_ref=o_ref, st_ref=st_ref, hd=hd, vs=vs,
                steps=[dict(rows=t["rows"], dec=t["dec"][:, ks], qd=t["qd"][:, ks], kd=t["kd"][:, ks],
                            kt=t["kt"][:, ks], v=v_ref[0, t["rows"], vs]) for t in steps]))
    for ch in chains:
        for t in ch["steps"]:
            t["att"] = jnp.where(ch["mask"], _dot_nt(t["qd"], t["kd"]), 0.0)
    for ch in chains:
        for t in ch["steps"]:
            t["local"] = _dot(t["att"], t["v"])
            t["upd"] = _dot_tn(t["v"], t["kt"])
    state = [ch["st_ref"][ch["hd"]] for ch in chains]
    for j in range(sub):
        out = [ch["steps"][j]["local"] + _dot_nt(ch["steps"][j]["qd"], s) for ch, s in zip(chains, state)]
        for ch, o in zip(chains, out):
            ch["o_ref"][0, ch["steps"][j]["rows"], ch["vs"]] = o.astype(BF16)
        state = [ch["steps"][j]["dec"] * s + ch["steps"][j]["upd"] for ch, s in zip(chains, state)]
    for ch, s in zip(chains, state):
        ch["st_ref"][ch["hd"]] = s

    @pl.when(i == nblocks - 1)
    def _():
        sf_ref[0] = stf_ref[...]
        sb_ref[0] = stb_ref[...]


def _gla_core(q, k, v, la_f, la_b, s0f, s0b):
    b, s, _ = q.shape
    sub = GLA_CHUNKS_PER_STEP if s % (GLA_CHUNKS_PER_STEP * GLA_CHUNK) == 0 else 1
    rows = sub * GLA_CHUNK
    nblocks = s // rows
    fwd = lambda w: pl.BlockSpec((1, rows, w), lambda bi, i: (bi, i, 0))
    bwd = lambda w: pl.BlockSpec((1, rows, w), lambda bi, i: (bi, nblocks - 1 - i, 0))
    st_shape = (GLA_HEADS, GLA_DV, GLA_DK)
    st_spec = pl.BlockSpec((1,) + st_shape, lambda bi, i: (bi, 0, 0, 0))
    return pl.pallas_call(
        functools.partial(_gla_core_kernel, nblocks=nblocks, sub=sub),
        grid=(b, nblocks),
        in_specs=[fwd(GLA_NQK), fwd(GLA_NQK), fwd(D_MODEL), fwd(GLA_NQK),
                  bwd(GLA_NQK), bwd(GLA_NQK), bwd(D_MODEL), bwd(GLA_NQK), st_spec, st_spec],
        out_specs=[fwd(D_MODEL), bwd(D_MODEL), st_spec, st_spec],
        out_shape=[jax.ShapeDtypeStruct((b, s, D_MODEL), BF16)] * 2
        + [jax.ShapeDtypeStruct((b,) + st_shape, F32)] * 2,
        scratch_shapes=[pltpu.VMEM(st_shape, F32), pltpu.VMEM(st_shape, F32)],
        compiler_params=_cparams("parallel", "arbitrary"),
        name="gla_core",
    )(q, k, v, la_f, q, k, v, la_b, s0f, s0b)


def _gla_finish_prologue(of_ref, ob_ref, g_ref, ng_ref):
    o = of_ref[0].astype(F32) + ob_ref[0].astype(F32)
    g = g_ref[0].astype(F32)
    ng = ng_ref[...]
    parts = []
    for hd in range(GLA_HEADS):
        vs = slice(hd * GLA_DV, (hd + 1) * GLA_DV)
        oh = o[:, vs]
        gh = g[:, vs]
        oh = oh * lax.rsqrt(jnp.mean(oh * oh, axis=-1, keepdims=True) + LN_EPS) * ng
        parts.append((oh * (gh * _sigmoid(gh))).astype(BF16))
    return jnp.concatenate(parts, axis=1)


def _mixer_gla(hx, hc, mx, mc, w_in, w_a1, w_a2, b_a, norm_g, w_o, ln_g, ln_b, ctx_out):
    b, n, d = hx.shape
    nqk = GLA_NQK
    pad = jnp.zeros((d, LANES - 2 * GLA_GATE_RANK), F32)
    w_ext = jnp.concatenate([w_in, w_a1[0], w_a1[1], pad], axis=1).astype(BF16)
    wa2_ext = jnp.zeros((2, LANES, nqk), F32)
    wa2_ext = wa2_ext.at[0, :GLA_GATE_RANK].set(w_a2[0])
    wa2_ext = wa2_ext.at[1, GLA_GATE_RANK:2 * GLA_GATE_RANK].set(w_a2[1]).astype(BF16)
    w_o = w_o.astype(BF16)

    qc, kc, vc, gc, lcf, lcb = _gla_in(hc, mc, w_ext, wa2_ext, b_a, None)
    qx, kx, vx, gx, lxf, lxb = _gla_in(hx, mx, w_ext, wa2_ext, b_a, _gla_rope_tables(n))
    s0 = jnp.zeros((b, GLA_HEADS, GLA_DV, GLA_DK), F32)
    ocf, ocb, scf, scb = _gla_core(qc, kc, vc, lcf, lcb, s0, s0)
    oxf, oxb, _, _ = _gla_core(qx, kx, vx, lxf, lxb, scf, scb)
    ng = norm_g.reshape(1, GLA_DV)
    hx = _out_proj([oxf, oxb, gx], [ng], _gla_finish_prologue, w_o, None, hx, mx, ln_g, ln_b, "gla_out")
    if ctx_out:
        hc = _out_proj([ocf, ocb, gc], [ng], _gla_finish_prologue, w_o, None, hc, mc, ln_g, ln_b, "gla_out_ctx")
    return hx, hc


def _head_sum_matrices():
    g = np.zeros((D_MODEL, LANES), np.float32)
    g[np.arange(D_MODEL), np.arange(D_MODEL) // RW_HEAD] = 1.0
    return jnp.asarray(g, BF16), jnp.asarray(g.T.copy(), BF16)


def _rw_prep_kernel(hp_ref, h_ref, hn_ref, mod_ref, mu_ref, wr_ref, wk_ref, wv_ref, g1_ref, g2_ref,
                    w1_ref, w2_ref, a1_ref, a2_ref, w0_ref, a0_ref, kkw_ref, kaw_ref, rkw_ref, gs_ref, gt_ref,
                    r_ref, kk_ref, v_ref, g_ref, bonus_ref,
                    lw0_ref, k0_ref, b0_ref, lw1_ref, k1_ref, b1_ref, *, tm, nt):
    t = pl.program_id(1)
    u = _modulate(h_ref[0], mod_ref, 1)
    up = _modulate(hp_ref[0, 7:8, :], mod_ref, 1)
    un = _modulate(hn_ref[0, 0:1, :], mod_ref, 1)
    up = jnp.where(t > 0, up, 0.0)
    un = jnp.where(t < nt - 1, un, 0.0)
    rowid = lax.broadcasted_iota(jnp.int32, (tm, 1), 0)
    prev = jnp.where(rowid == 0, up, pltpu.roll(u, 1, axis=0))
    nxt = jnp.where(rowid == tm - 1, un, pltpu.roll(u, tm - 1, axis=0))
    xx = 0.5 * (prev + nxt) - u
    mix = lambda j: (u + xx * mu_ref[j:j + 1, :]).astype(BF16)
    gs = gs_ref[...]
    gt = gt_ref[...]

    r = jnp.dot(mix(0), wr_ref[...], preferred_element_type=F32)
    k = jnp.dot(mix(2), wk_ref[...], preferred_element_type=F32)
    v = jnp.dot(mix(3), wv_ref[...], preferred_element_type=F32)
    hw = jnp.tanh(jnp.dot(mix(1), w1_ref[...], preferred_element_type=F32)).astype(BF16)
    ha = jnp.dot(mix(4), a1_ref[...], preferred_element_type=F32).astype(BF16)
    gg = _sigmoid(jnp.dot(mix(5), g1_ref[...], preferred_element_type=F32)).astype(BF16)
    g_ref[0] = jnp.dot(gg, g2_ref[...], preferred_element_type=F32).astype(BF16)

    kk = k * kkw_ref[...]
    nrm = jnp.maximum(jnp.sqrt(_dot_sel(kk * kk, gs)), 1e-12)
    kk = kk * _dot_sel(1.0 / nrm, gt)
    ksum = jnp.zeros_like(k)
    for dr, (lw_ref, kd_ref, bd_ref) in enumerate(((lw0_ref, k0_ref, b0_ref), (lw1_ref, k1_ref, b1_ref))):
        xw = w0_ref[dr:dr + 1, :] + jnp.dot(hw, w2_ref[dr], preferred_element_type=F32)
        lw_ref[0] = -float(np.exp(-0.5)) * _sigmoid(xw)
        a = _sigmoid(a0_ref[dr:dr + 1, :] + jnp.dot(ha, a2_ref[dr], preferred_element_type=F32))
        kd = k * (1.0 + (a - 1.0) * kaw_ref[...])
        kd_ref[0] = kd
        bd_ref[0] = kk * a
        ksum = ksum + kd
    r_ref[0] = r
    kk_ref[0] = kk
    v_ref[0] = v.astype(BF16)
    bonus_ref[0] = (_dot_sel(_dot_sel(r * rkw_ref[...] * ksum, gs), gt) * v).astype(BF16)


def _rw_prep(h, mod, consts):
    b, s, d = h.shape
    tm = _tile(s, RW_PREP_TILE)
    nt = s // tm
    hb = tm // 8
    nhb = s // 8
    in_specs = [pl.BlockSpec((1, 8, d), lambda bi, t: (bi, jnp.maximum(t * hb - 1, 0), 0)),
                _tok_spec(tm, d),
                pl.BlockSpec((1, 8, d), lambda bi, t: (bi, jnp.minimum((t + 1) * hb, nhb - 1), 0)),
                _mod_spec()] + [_const_spec(a.shape) for a in consts]
    f32o = jax.ShapeDtypeStruct(h.shape, F32)
    bf16o = jax.ShapeDtypeStruct(h.shape, BF16)
    out_shape = [f32o, f32o, bf16o, bf16o, bf16o] + [f32o] * 6
    return pl.pallas_call(
        functools.partial(_rw_prep_kernel, tm=tm, nt=nt),
        grid=(b, nt),
        in_specs=in_specs,
        out_specs=[_tok_spec(tm, d)] * 11,
        out_shape=out_shape,
        compiler_params=_cparams("parallel", "arbitrary"),
        name="rw_prep",
    )(h, h, h, mod, *consts)


def _rw_prelude(refs, forward, rows):
    r_ref, kk_ref, v_ref, lw_ref, k_ref, b_ref = (ref.at[0, rows, :] for ref in refs)
    c = RW_CHUNK
    row = lax.broadcasted_iota(jnp.int32, (c, 4 * c), 0)
    col = lax.broadcasted_iota(jnp.int32, (c, 4 * c), 1) % c
    lower, upper = _tri_masks(c)
    if forward:
        strict, incl, tri, last = row > col, row >= col, lower, c - 1
    else:
        strict, incl, tri, last = row < col, row <= col, upper, 0
    tri = tri.astype(BF16)
    lw = lw_ref[...]
    hi, lo = _split(lw)
    cum = jnp.dot(tri, hi, preferred_element_type=F32) + jnp.dot(tri, lo, preferred_element_type=F32)
    tot = cum[last:last + 1, :]
    e_neg = jnp.exp(-cum)
    e_tail = jnp.exp(tot - cum)
    b_raw = b_ref[...]
    k_raw = k_ref[...]
    return dict(
        rows=rows, strict=strict, incl=incl, dec=jnp.exp(tot),
        a=(kk_ref[...] * jnp.exp(cum - lw)).astype(BF16), r=(r_ref[...] * jnp.exp(cum)).astype(BF16),
        bt=(b_raw * e_neg).astype(BF16), kt=(k_raw * e_neg).astype(BF16),
        bh=(b_raw * e_tail).astype(BF16), kh=(k_raw * e_tail).astype(BF16), v=v_ref[...])


def _rw_scan_block(pres_f, pres_b, stf_ref, stb_ref, yf_ref, yb_ref):
    c = RW_CHUNK
    rowp = lax.broadcasted_iota(jnp.int32, (c, 2 * c), 0)
    colp = lax.broadcasted_iota(jnp.int32, (c, 2 * c), 1) % c
    eye = (rowp == colp).astype(F32)
    blocks = [(rowp // s) == (colp // s) for s in (2 ** e for e in range(1, int(np.log2(c)) + 1))]
    row2 = lax.broadcasted_iota(jnp.int32, (LANES, LANES), 0)
    col2 = lax.broadcasted_iota(jnp.int32, (LANES, LANES), 1)
    same_head = (row2 < RW_HEAD) == (col2 < RW_HEAD)
    lane0 = lax.broadcasted_iota(jnp.int32, (1, LANES), 1) < RW_HEAD
    zero = jnp.zeros((c, LANES), BF16)

    def bdiag(x):
        x = x.astype(BF16)
        return jnp.concatenate([jnp.where(lane0, x, zero), jnp.where(lane0, zero, x)], axis=0)

    chains = [(pres, st_ref, y_ref, slice(p * LANES, (p + 1) * LANES), p)
              for pres, st_ref, y_ref in ((pres_f, stf_ref, yf_ref), (pres_b, stb_ref, yb_ref))
              for p in range(RW_PAIRS)]
    free = [(pre, {nm: pre[nm][:, sl] for nm in ("a", "r", "bt", "kt", "bh", "kh", "v")})
            for pres, _, _, sl, _ in chains for pre in pres]
    n = len(free)
    ar = [jnp.concatenate([t["a"], t["r"]], axis=0) for _, t in free]
    gram = [_dot_nt(ar[i], jnp.concatenate([bdiag(free[i][1]["bt"]), bdiag(free[i][1]["kt"])], axis=0))
            for i in range(n)]
    n1 = [jnp.where(free[i][0]["strict"][:, :2 * c], gram[i][:c, :2 * c], 0.0) for i in range(n)]
    a_ak = [jnp.where(free[i][0]["strict"][:, :2 * c], gram[i][:c, 2 * c:], 0.0) for i in range(n)]
    r_bk = [jnp.where(free[i][0]["incl"], gram[i][c:], 0.0) for i in range(n)]
    vbd = [bdiag(t["v"]) for _, t in free]
    w0 = [_dot(a_ak[i], vbd[i]) for i in range(n)]
    inv = [eye - jnp.where(blocks[0], n1[i], 0.0) for i in range(n)]
    for lvl in range(1, len(blocks)):
        sel = blocks[lvl] & ~blocks[lvl - 1]
        tmp = [_dot(jnp.where(sel, n1[i], 0.0), bdiag(inv[i])) for i in range(n)]
        inv = [inv[i] - _dot(inv[i], bdiag(tmp[i])) for i in range(n)]
    nsub = len(pres_f)
    zt = [st_ref[p] for _, st_ref, _, _, p in chains]
    for j in range(nsub):
        idx = [ci * nsub + j for ci in range(len(chains))]
        sz = [_dot_nt(ar[i], z) for i, z in zip(idx, zt)]
        u = [-_dot(inv[i], bdiag(s[:c] + w0[i])) for i, s in zip(idx, sz)]
        ys = [_dot(r_bk[i], jnp.concatenate([bdiag(uu), vbd[i]], axis=0)) for i, uu in zip(idx, u)]
        for (_, _, y_ref, sl, _), i, s, y in zip(chains, idx, sz, ys):
            y_ref[0, free[i][0]["rows"], sl] = (s[c:] + y).astype(BF16)
        upd = [_dot_tn(jnp.concatenate([uu.astype(BF16), free[i][1]["v"]], axis=0),
                       jnp.concatenate([free[i][1]["bh"], free[i][1]["kh"]], axis=0)) for i, uu in zip(idx, u)]
        zt = [free[i][0]["dec"][:, sl] * z + jnp.where(same_head, up, 0.0)
              for (_, _, _, sl, _), i, z, up in zip(chains, idx, zt, upd)]
    for (_, st_ref, _, _, p), z in zip(chains, zt):
        st_ref[p] = z


def _rw_scan_kernel(*refs, nblocks, sub):
    fwd_in, bwd_in = refs[0:6], refs[6:12]
    s0f_ref, s0b_ref, yf_ref, yb_ref, sf_ref, sb_ref, stf_ref, stb_ref = refs[12:]
    i = pl.program_id(1)
    c = RW_CHUNK

    @pl.when(i == 0)
    def _():
        stf_ref[...] = s0f_ref[0]
        stb_ref[...] = s0b_ref[0]

    pres_f = [_rw_prelude(fwd_in, True, slice(j * c, (j + 1) * c)) for j in range(sub)]
    pres_b = [_rw_prelude(bwd_in, False, slice(j * c, (j + 1) * c)) for j in range(sub - 1, -1, -1)]
    _rw_scan_block(pres_f, pres_b, stf_ref, stb_ref, yf_ref, yb_ref)

    @pl.when(i == nblocks - 1)
    def _():
        sf_ref[0] = stf_ref[...]
        sb_ref[0] = stb_ref[...]


def _rw_scan(r, kk, v, dirs, s0f, s0b):
    b, s, d = r.shape
    sub = RW_CHUNKS_PER_STEP if s % (RW_CHUNKS_PER_STEP * RW_CHUNK) == 0 else 1
    rows = sub * RW_CHUNK
    nblocks = s // rows
    fwd = pl.BlockSpec((1, rows, d), lambda bi, i: (bi, i, 0))
    bwd = pl.BlockSpec((1, rows, d), lambda bi, i: (bi, nblocks - 1 - i, 0))
    st_shape = (RW_PAIRS, LANES, LANES)
    st_spec = pl.BlockSpec((1,) + st_shape, lambda bi, i: (bi, 0, 0, 0))
    (lw0, k0, b0), (lw1, k1, b1) = dirs
    return pl.pallas_call(
        functools.partial(_rw_scan_kernel, nblocks=nblocks, sub=sub),
        grid=(b, nblocks),
        in_specs=[fwd] * 6 + [bwd] * 6 + [st_spec, st_spec],
        out_specs=[fwd, bwd, st_spec, st_spec],
        out_shape=[jax.ShapeDtypeStruct((b, s, d), BF16)] * 2 + [jax.ShapeDtypeStruct((b,) + st_shape, F32)] * 2,
        scratch_shapes=[pltpu.VMEM(st_shape, F32), pltpu.VMEM(st_shape, F32)],
        compiler_params=_cparams("parallel", "arbitrary"),
        name="rw_scan",
    )(r, kk, v, lw0, k0, b0, r, kk, v, lw1, k1, b1, s0f, s0b)


def _rw_finish_prologue(yf_ref, yb_ref, bonus_ref, g_ref, gs_ref, gt_ref, gng_ref, gnb_ref):
    y = yf_ref[0].astype(F32) + yb_ref[0].astype(F32)
    gs = gs_ref[...]
    gt = gt_ref[...]
    inv_n = 1.0 / RW_HEAD
    mean = _dot_sel(_dot_sel(y, gs) * inv_n, gt)
    yc = y - mean
    var = _dot_sel(yc * yc, gs) * inv_n
    rstd = _dot_sel(lax.rsqrt(var + RW_GN_EPS), gt)
    yn = yc * rstd * gng_ref[...] + gnb_ref[...]
    return ((yn + bonus_ref[0].astype(F32)) * g_ref[0].astype(F32)).astype(BF16)


def _mixer_rwkv(hx, hc, mx, mc, mu, w_rkv, w0, w1, w2, a0, a1, a2, g1, g2, k_k, k_a, r_k, gn_g, gn_b, w_o,
                ln_g, ln_b, ctx_out):
    b, n, d = hx.shape
    gs, gt = _head_sum_matrices()
    rank_w, rank_a = w1.shape[-1], a1.shape[-1]
    assert 2 * rank_w == LANES and 2 * rank_a == LANES

    def ext(m2):
        z = jnp.zeros((2, LANES, d), F32)
        z = z.at[0, :m2.shape[1]].set(m2[0])
        return z.at[1, m2.shape[1]:].set(m2[1]).astype(BF16)

    consts = [mu, w_rkv[0].astype(BF16), w_rkv[1].astype(BF16), w_rkv[2].astype(BF16),
              g1.astype(BF16), g2.astype(BF16),
              jnp.concatenate([w1[0], w1[1]], axis=1).astype(BF16), ext(w2),
              jnp.concatenate([a1[0], a1[1]], axis=1).astype(BF16), ext(a2),
              w0, a0, k_k.reshape(1, d), k_a.reshape(1, d), r_k.reshape(1, d), gs, gt]
    w_o = w_o.astype(BF16)

    def prep(h, mod):
        r, kk, v, g, bonus, lw0, k0, b0, lw1, k1, b1 = _rw_prep(h, mod, consts)
        return r, kk, v, g, bonus, ((lw0, k0, b0), (lw1, k1, b1))

    rc, kkc, vc, gc, bonc, dc = prep(hc, mc)
    rx, kkx, vx, gx, bonx, dx = prep(hx, mx)
    s0 = jnp.zeros((b, RW_PAIRS, LANES, LANES), F32)
    ycf, ycb, scf, scb = _rw_scan(rc, kkc, vc, dc, s0, s0)
    yxf, yxb, _, _ = _rw_scan(rx, kkx, vx, dx, scf, scb)
    fin_consts = [gs, gt, gn_g.reshape(1, d), gn_b.reshape(1, d)]
    hx = _out_proj([yxf, yxb, bonx, gx], fin_consts, _rw_finish_prologue, w_o, None, hx, mx, ln_g, ln_b, "rw_out",
                   tile=RW_PREP_TILE)
    if ctx_out:
        hc = _out_proj([ycf, ycb, bonc, gc], fin_consts, _rw_finish_prologue, w_o, None, hc, mc, ln_g, ln_b,
                       "rw_out_ctx", tile=RW_PREP_TILE)
    return hx, hc


def kernel(x, c, ctx, c_ctx, ada_w, ada_b, ln_g, ln_b, ffn_w13, ffn_w2, na_wqkv, na_wo, na_rpb, cv_w1, cv_b1, cv_wdw, cv_bdw, cv_ln_g, cv_ln_b, cv_w2, cv_b2, gla_win, gla_wa1, gla_wa2, gla_ba, gla_norm_g, gla_wo, rw_mu, rw_wrkv, rw_w0, rw_w1, rw_w2, rw_a0, rw_a1, rw_a2, rw_g1, rw_g2, rw_kk, rw_ka, rw_rk, rw_gn_g, rw_gn_b, rw_wo):
    b, n, d = x.shape
    depth = ada_w.shape[0]
    assert d == D_MODEL and depth == DEPTH and n % GRID_W == 0
    rows = -(-(b + 1) // 8) * 8
    cond = jnp.zeros((rows, d), F32).at[:b].set(c).at[b].set(c_ctx)
    mod_all = _ada(cond, ada_w, ada_b)
    w13 = ffn_w13.astype(BF16)
    w2 = ffn_w2.astype(BF16)
    hx, hc = x, ctx
    for i in range(depth):
        last = i == depth - 1
        mx = mod_all[i, :b].reshape(b, 3 * N_SUB, d)
        mc = jnp.broadcast_to(mod_all[i, b].reshape(1, 3 * N_SUB, d), (b, 3 * N_SUB, d))
        hx, hc = _ffn_pair(hx, hc, mx, mc, w13, w2, i, 0, ln_g[i, 0], ln_b[i, 0], 0)
        kind, j = i % 4, i // 4
        lg, lb = ln_g[i, 1], ln_b[i, 1]
        if kind == 0:
            hx, hc = _mixer_na(hx, hc, mx, mc, na_wqkv[j], na_wo[j], na_rpb[j], lg, lb, not last)
        elif kind == 1:
            hx, hc = _mixer_conv(hx, hc, mx, mc, cv_w1[j], cv_b1[j], cv_wdw[j], cv_bdw[j], cv_ln_g[j], cv_ln_b[j],
                                 cv_w2[j], cv_b2[j], lg, lb, not last)
        elif kind == 2:
            hx, hc = _mixer_gla(hx, hc, mx, mc, gla_win[j], gla_wa1[j], gla_wa2[j], gla_ba[j], gla_norm_g[j],
                                gla_wo[j], lg, lb, not last)
        else:
            hx, hc = _mixer_rwkv(hx, hc, mx, mc, rw_mu[j], rw_wrkv[j], rw_w0[j], rw_w1[j], rw_w2[j], rw_a0[j],
                                 rw_a1[j], rw_a2[j], rw_g1[j], rw_g2[j], rw_kk[j], rw_ka[j], rw_rk[j],
                                 rw_gn_g[j], rw_gn_b[j], rw_wo[j], lg, lb, not last)
        if last:
            hx = _ffn(hx, mx, w13, w2, i, 1, ln_g[i, 2], ln_b[i, 2], 2)
        else:
            hx, hc = _ffn_pair(hx, hc, mx, mc, w13, w2, i, 1, ln_g[i, 2], ln_b[i, 2], 2)
    return hx
```

```python
import functools

import numpy as np
import jax
import jax.numpy as jnp
from jax import lax
from jax.experimental import pallas as pl
from jax.experimental.pallas import tpu as pltpu

F32 = jnp.float32
BF16 = jnp.bfloat16

D_MODEL = 1024
DEPTH = 4
N_SUB = 3
GRID_W = 64
ALPHA = (2.0 * DEPTH) ** 0.25
LN_EPS = 1e-5
D_FF = 2816
NA_HEADS = 16
NA_HEAD_DIM = D_MODEL // NA_HEADS
NA_KH = 8
NA_KW = 16
CONV_WIDTH = 31
CONV_HALO = 16
GLA_HEADS = 4
GLA_DK = 128
GLA_DV = 256
GLA_GATE_RANK = 16
GLA_NORMALIZER = 16.0
GLA_CHUNK = 64
GLA_CHUNKS_PER_STEP = 2
ROPE_BASE = 10000.0
RW_HEAD = 64
RW_HEADS = D_MODEL // RW_HEAD
RW_PAIRS = RW_HEADS // 2
RW_CHUNK = 64
RW_CHUNKS_PER_STEP = 1
RW_GN_EPS = 64e-5
LANES = 128
SUBLANES = 8
NEG_BIG = -1e30
VMEM_LIMIT = 56 * 1024 * 1024
FFN_TILE = 512
PROJ_TILE = 512
CONV_TILE = 256
RW_PREP_TILE = 256


def _cparams(*sem):
    return pltpu.CompilerParams(dimension_semantics=sem, vmem_limit_bytes=VMEM_LIMIT)


def _dot(a, b):
    return jnp.dot(a.astype(BF16), b.astype(BF16), preferred_element_type=F32)


def _dot_nt(a, b):
    return lax.dot_general(a.astype(BF16), b.astype(BF16), (((1,), (1,)), ((), ())),
                           preferred_element_type=F32)


def _dot_tn(a, b):
    return lax.dot_general(a.astype(BF16), b.astype(BF16), (((0,), (0,)), ((), ())),
                           preferred_element_type=F32)


def _split(x):
    hi = x.astype(BF16)
    lo = (x - hi.astype(F32)).astype(BF16)
    return hi, lo


def _dot_sel(x, sel):
    hi, lo = _split(x)
    return (jnp.dot(hi, sel, preferred_element_type=F32)
            + jnp.dot(lo, sel, preferred_element_type=F32))


def _sel_dot(sel, x):
    hi, lo = _split(x)
    return (jnp.dot(sel, hi, preferred_element_type=F32)
            + jnp.dot(sel, lo, preferred_element_type=F32))


def _sigmoid(x):
    return jax.nn.sigmoid(x)


def _softplus(x):
    return jnp.maximum(x, 0.0) + jnp.log(1.0 + jnp.exp(-jnp.abs(x)))


def _layer_norm(x, g, b, eps=LN_EPS):
    mu = jnp.mean(x, axis=-1, keepdims=True)
    xc = x - mu
    var = jnp.mean(xc * xc, axis=-1, keepdims=True)
    return xc * lax.rsqrt(var + eps) * g + b


def _mod_rows(mod_ref, j):
    return (mod_ref[0, 3 * j:3 * j + 1, :], mod_ref[0, 3 * j + 1:3 * j + 2, :],
            mod_ref[0, 3 * j + 2:3 * j + 3, :])


def _modulate(h, mod_ref, j):
    shift, scale, _ = _mod_rows(mod_ref, j)
    return h * (1.0 + scale) + shift


def _post_norm(h, y, mod_ref, j, g_ref, b_ref):
    gate = mod_ref[0, 3 * j + 2:3 * j + 3, :]
    return _layer_norm(ALPHA * h + gate * y, g_ref[...], b_ref[...])


def _tok_spec(tm, d):
    return pl.BlockSpec((1, tm, d), lambda b, t: (b, t, 0))


def _mod_spec():
    return pl.BlockSpec((1, 3 * N_SUB, D_MODEL), lambda b, t: (b, 0, 0))


def _const_spec(shape):
    nd = len(shape)
    return pl.BlockSpec(shape, lambda b, t: (0,) * nd, pipeline_mode=pl.Buffered(1))


def _tile(s, pref):
    tm = min(pref, s)
    assert s % tm == 0
    return tm


def _ada_kernel(cond_ref, w_ref, b_ref, o_ref):
    c = cond_ref[...]
    o_ref[0] = _dot(c * _sigmoid(c), w_ref[0]) + b_ref[0]


def _ada(cond, ada_w, ada_b):
    r, d = cond.shape
    depth, _, n = ada_w.shape
    tn = 1024
    return pl.pallas_call(
        _ada_kernel,
        grid=(depth, n // tn),
        in_specs=[pl.BlockSpec((r, d), lambda l, j: (0, 0)),
                  pl.BlockSpec((1, d, tn), lambda l, j: (l, 0, j)),
                  pl.BlockSpec((1, 1, tn), lambda l, j: (l, 0, j))],
        out_specs=pl.BlockSpec((1, r, tn), lambda l, j: (l, 0, j)),
        out_shape=jax.ShapeDtypeStruct((depth, r, n), F32),
        compiler_params=_cparams("parallel", "parallel"),
        name="ada_mod",
    )(cond, ada_w, ada_b.reshape(depth, 1, n))


FFN_CHUNK = 256


def _ffn_tile(h_ref, mod_ref, w13_ref, w2_ref, g_ref, b_ref, o_ref, j):
    h = h_ref[0]
    u = _modulate(h, mod_ref, j).astype(BF16)
    acc = jnp.zeros(h.shape, F32)
    for lo in range(0, D_FF, FFN_CHUNK):
        hi = min(lo + FFN_CHUNK, D_FF)
        a = jnp.dot(u, w13_ref[:, lo:hi], preferred_element_type=F32)
        v = jnp.dot(u, w13_ref[:, D_FF + lo:D_FF + hi], preferred_element_type=F32)
        z = (a * _sigmoid(a) * v).astype(BF16)
        acc = acc + jnp.dot(z, w2_ref[lo:hi, :], preferred_element_type=F32)
    o_ref[0] = _post_norm(h, 0.5 * acc, mod_ref, j, g_ref, b_ref)


def _ffn_kernel(h_ref, mod_ref, w13_ref, w2_ref, g_ref, b_ref, o_ref, *, j):
    _ffn_tile(h_ref, mod_ref, w13_ref, w2_ref, g_ref, b_ref, o_ref, j)


def _ffn_pair_kernel(hx_ref, hc_ref, mx_ref, mc_ref, w13_ref, w2_ref, g_ref, b_ref, ox_ref, oc_ref, *, j, nx):
    t = pl.program_id(1)

    @pl.when(t < nx)
    def _():
        _ffn_tile(hx_ref, mx_ref, w13_ref, w2_ref, g_ref, b_ref, ox_ref, j)

    @pl.when(t == nx)
    def _():
        _ffn_tile(hc_ref, mc_ref, w13_ref, w2_ref, g_ref, b_ref, oc_ref, j)


def _ffn_weight_spec(shape, layer, half):
    return pl.BlockSpec((None, None) + shape[2:], lambda bi, t: (layer, half, 0, 0), pipeline_mode=pl.Buffered(1))


def _ffn_pair(hx, hc, mx, mc, w13_all, w2_all, layer, half, ln_g, ln_b, j):
    b, n, d = hx.shape
    l = hc.shape[1]
    tm = _tile(n, FFN_TILE)
    nx = n // tm
    x_spec = pl.BlockSpec((1, tm, d), lambda bi, t: (bi, jnp.minimum(t, nx - 1), 0))
    c_spec = pl.BlockSpec((1, l, d), lambda bi, t: (bi, 0, 0))
    return pl.pallas_call(
        functools.partial(_ffn_pair_kernel, j=j, nx=nx),
        grid=(b, nx + 1),
        in_specs=[x_spec, c_spec, _mod_spec(), _mod_spec(), _ffn_weight_spec(w13_all.shape, layer, half),
                  _ffn_weight_spec(w2_all.shape, layer, half), _const_spec((1, d)), _const_spec((1, d))],
        out_specs=[x_spec, c_spec],
        out_shape=[jax.ShapeDtypeStruct(hx.shape, F32), jax.ShapeDtypeStruct(hc.shape, F32)],
        compiler_params=_cparams("parallel", "arbitrary"),
        name="ffn_pair",
    )(hx, hc, mx, mc, w13_all, w2_all, ln_g.reshape(1, d), ln_b.reshape(1, d))


def _ffn(h, mod, w13_all, w2_all, layer, half, ln_g, ln_b, j):
    b, s, d = h.shape
    tm = _tile(s, FFN_TILE)
    pick = lambda shape: _ffn_weight_spec(shape, layer, half)
    return pl.pallas_call(
        functools.partial(_ffn_kernel, j=j),
        grid=(b, s // tm),
        in_specs=[_tok_spec(tm, d), _mod_spec(), pick(w13_all.shape), pick(w2_all.shape),
                  _const_spec((1, d)), _const_spec((1, d))],
        out_specs=_tok_spec(tm, d),
        out_shape=jax.ShapeDtypeStruct(h.shape, F32),
        compiler_params=_cparams("parallel", "parallel"),
        name="ffn_half",
    )(h, mod, w13_all, w2_all, ln_g.reshape(1, d), ln_b.reshape(1, d))


def _out_kernel(*refs, n_in, prologue, has_bias):
    ins = refs[:n_in]
    w_ref = refs[n_in]
    k = n_in + 1
    bias_ref = None
    if has_bias:
        bias_ref = refs[k]
        k += 1
    h_ref, mod_ref, g_ref, b_ref, o_ref = refs[k:k + 5]
    y = jnp.dot(prologue(*ins), w_ref[...], preferred_element_type=F32)
    if has_bias:
        y = y + bias_ref[...]
    o_ref[0] = _post_norm(h_ref[0], y, mod_ref, 1, g_ref, b_ref)


def _out_proj(tok_inputs, const_inputs, prologue, w, bias, h, mod, ln_g, ln_b, name, tile=PROJ_TILE):
    b, s, d = h.shape
    tm = _tile(s, tile)
    n_in = len(tok_inputs) + len(const_inputs)
    in_specs = [_tok_spec(tm, a.shape[-1]) for a in tok_inputs]
    in_specs += [_const_spec(a.shape) for a in const_inputs]
    in_specs.append(_const_spec(w.shape))
    args = list(tok_inputs) + list(const_inputs) + [w]
    if bias is not None:
        in_specs.append(_const_spec((1, d)))
        args.append(bias.reshape(1, d))
    in_specs += [_tok_spec(tm, d), _mod_spec(), _const_spec((1, d)), _const_spec((1, d))]
    args += [h, mod, ln_g.reshape(1, d), ln_b.reshape(1, d)]
    return pl.pallas_call(
        functools.partial(_out_kernel, n_in=n_in, prologue=prologue, has_bias=bias is not None),
        grid=(b, s // tm),
        in_specs=in_specs,
        out_specs=_tok_spec(tm, d),
        out_shape=jax.ShapeDtypeStruct(h.shape, F32),
        compiler_params=_cparams("parallel", "parallel"),
        name=name,
    )(*args)


def _plain_prologue(y_ref):
    return y_ref[0].astype(BF16)


def _na_qkv_kernel(h_ref, mod_ref, w_ref, q_ref, k_ref, v_ref):
    d = D_MODEL
    u = _modulate(h_ref[0], mod_ref, 1).astype(BF16)
    z = jnp.dot(u, w_ref[...], preferred_element_type=F32)
    q_ref[0] = (z[:, :d] * (NA_HEAD_DIM ** -0.5)).astype(BF16)
    k_ref[0] = z[:, d:2 * d].astype(BF16)
    v_ref[0] = z[:, 2 * d:].astype(BF16)


def _na_qkv(h, mod, w):
    b, s, d = h.shape
    tm = _tile(s, PROJ_TILE)
    out = jax.ShapeDtypeStruct(h.shape, BF16)
    return pl.pallas_call(
        _na_qkv_kernel,
        grid=(b, s // tm),
        in_specs=[_tok_spec(tm, d), _mod_spec(), _const_spec(w.shape)],
        out_specs=[_tok_spec(tm, d)] * 3,
        out_shape=[out] * 3,
        compiler_params=_cparams("parallel", "parallel"),
        name="na_qkv",
    )(h, mod, w)


def _na_row_start(r, rows, kh):
    return jnp.clip(r - kh // 2, 0, rows - kh)


def _na_attn_kernel(q_ref, k_ref, v_ref, kc_ref, vc_ref, bias_ref, o_ref, *, rows, kh):
    r = pl.program_id(1)
    start = pl.multiple_of(_na_row_start(r, rows, kh) * GRID_W, GRID_W)
    win = kh * GRID_W
    lane0 = lax.broadcasted_iota(jnp.int32, (1, LANES), 1) < NA_HEAD_DIM
    zero = jnp.zeros((GRID_W, LANES), BF16)
    pairs = range(NA_HEADS // 2)
    sls = [slice(p * LANES, (p + 1) * LANES) for p in pairs]
    qbd = []
    for sl in sls:
        q = q_ref[0, :, sl]
        qbd.append(jnp.concatenate([jnp.where(lane0, q, zero), jnp.where(lane0, zero, q)], axis=0))
    s = _na_row_start(r, rows, kh) - r + NA_KH - 1
    bias = [jnp.concatenate([bias_ref[s + a, p] for a in range(kh)], axis=0) for p in pairs]
    s_loc = [_dot_nt(k_ref[0, pl.ds(start, win), sls[p]], qbd[p]) + bias[p] for p in pairs]
    s_ctx = [_dot_nt(kc_ref[0, :, sls[p]], qbd[p]) for p in pairs]
    m = [jnp.maximum(jnp.max(s_loc[p], axis=0, keepdims=True), jnp.max(s_ctx[p], axis=0, keepdims=True))
         for p in pairs]
    p_loc = [jnp.exp(s_loc[p] - m[p]) for p in pairs]
    p_ctx = [jnp.exp(s_ctx[p] - m[p]) for p in pairs]
    inv = [1.0 / (jnp.sum(p_loc[p], axis=0, keepdims=True) + jnp.sum(p_ctx[p], axis=0, keepdims=True))
           for p in pairs]
    o = [_dot_tn(p_loc[p] * inv[p], v_ref[0, pl.ds(start, win), sls[p]])
         + _dot_tn(p_ctx[p] * inv[p], vc_ref[0, :, sls[p]]) for p in pairs]
    for p in pairs:
        o_ref[0, :, sls[p]] = jnp.where(lane0, o[p][:GRID_W], o[p][GRID_W:]).astype(BF16)


def _na_bias_table(rpb):
    w = GRID_W
    nrow = 2 * NA_KH - 1
    qc = np.arange(w)[:, None]
    kc = np.arange(w)[None, :]
    cstart = np.clip(qc - NA_KW // 2, 0, w - NA_KW)
    ok = (kc >= cstart) & (kc < cstart + NA_KW)
    lpad = w - NA_KW
    g = jnp.pad(rpb, ((0, 0), (0, 0), (lpad, lpad)))
    flat = jnp.broadcast_to(g[:, :, None, :], (NA_HEADS, nrow, w, 2 * w - 1)).reshape(NA_HEADS, nrow, -1)
    t = flat[:, :, w - 1:w - 1 + w * (2 * w - 2)].reshape(NA_HEADS, nrow, w, 2 * w - 2)[..., :w]
    tbl = jnp.where(ok[None, None], t, NEG_BIG)
    tbl = tbl.reshape(NA_HEADS // 2, 2, nrow, w, w)
    return jnp.transpose(tbl, (2, 0, 4, 1, 3)).reshape(nrow, NA_HEADS // 2, w, 2 * w)


def _na_attn(q, k, v, kc, vc, bias_tab):
    b, n, d = q.shape
    l = kc.shape[1]
    rows = n // GRID_W
    kh = min(NA_KH, rows)
    return pl.pallas_call(
        functools.partial(_na_attn_kernel, rows=rows, kh=kh),
        grid=(b, rows),
        in_specs=[pl.BlockSpec((1, GRID_W, d), lambda bi, r: (bi, r, 0)),
                  pl.BlockSpec((1, n, d), lambda bi, r: (bi, 0, 0)),
                  pl.BlockSpec((1, n, d), lambda bi, r: (bi, 0, 0)),
                  pl.BlockSpec((1, l, d), lambda bi, r: (bi, 0, 0)),
                  pl.BlockSpec((1, l, d), lambda bi, r: (bi, 0, 0)),
                  _const_spec(bias_tab.shape)],
        out_specs=pl.BlockSpec((1, GRID_W, d), lambda bi, r: (bi, r, 0)),
        out_shape=jax.ShapeDtypeStruct(q.shape, BF16),
        compiler_params=_cparams("parallel", "arbitrary"),
        name="na_attn",
    )(q, k, v, kc, vc, bias_tab)


def _ctx_attn_kernel(q_ref, k_ref, v_ref, o_ref):
    l = q_ref.shape[1]
    lane0 = lax.broadcasted_iota(jnp.int32, (1, LANES), 1) < NA_HEAD_DIM
    zero = jnp.zeros((l, LANES), BF16)
    pairs = range(NA_HEADS // 2)
    sls = [slice(p * LANES, (p + 1) * LANES) for p in pairs]
    qbd = []
    for sl in sls:
        q = q_ref[0, :, sl]
        qbd.append(jnp.concatenate([jnp.where(lane0, q, zero), jnp.where(lane0, zero, q)], axis=0))
    s = [_dot_nt(k_ref[0, :, sls[p]], qbd[p]) for p in pairs]
    e = [jnp.exp(s[p] - jnp.max(s[p], axis=0, keepdims=True)) for p in pairs]
    prob = [e[p] * (1.0 / jnp.sum(e[p], axis=0, keepdims=True)) for p in pairs]
    o = [_dot_tn(prob[p], v_ref[0, :, sls[p]]) for p in pairs]
    for p in pairs:
        o_ref[0, :, sls[p]] = jnp.where(lane0, o[p][:l], o[p][l:]).astype(BF16)


def _ctx_attn(q, k, v):
    b, l, d = q.shape
    spec = pl.BlockSpec((1, l, d), lambda bi: (bi, 0, 0))
    return pl.pallas_call(
        _ctx_attn_kernel,
        grid=(b,),
        in_specs=[spec] * 3,
        out_specs=spec,
        out_shape=jax.ShapeDtypeStruct(q.shape, BF16),
        compiler_params=_cparams("parallel"),
        name="ctx_attn",
    )(q, k, v)


def _mixer_na(hx, hc, mx, mc, w_qkv, w_o, rpb, ln_g, ln_b, ctx_out):
    w_qkv = w_qkv.astype(BF16)
    w_o = w_o.astype(BF16)
    qx, kx, vx = _na_qkv(hx, mx, w_qkv)
    qc, kc, vc = _na_qkv(hc, mc, w_qkv)
    ox = _na_attn(qx, kx, vx, kc, vc, _na_bias_table(rpb))
    hx = _out_proj([ox], [], _plain_prologue, w_o, None, hx, mx, ln_g, ln_b, "na_out")
    if ctx_out:
        oc = _ctx_attn(qc, kc, vc)
        hc = _out_proj([oc], [], _plain_prologue, w_o, None, hc, mc, ln_g, ln_b, "na_out_ctx")
    return hx, hc


CONV_ROWS = 16


def _conv_core_kernel(prev_ref, cur_ref, next_ref, mod_ref, w1_ref, b1_ref, w_ref, bdw_ref, g_ref, b_ref, o_ref,
                      win_ref, *, ts, nt):
    t = pl.program_id(1)
    halo = CONV_HALO
    d = cur_ref.shape[-1]
    rows = ts + 2 * halo
    hwin = jnp.concatenate([prev_ref[0], cur_ref[0], next_ref[0]], axis=0)
    z = jnp.dot(_modulate(hwin, mod_ref, 1).astype(BF16), w1_ref[...], preferred_element_type=F32) + b1_ref[...]
    rid = lax.broadcasted_iota(jnp.int32, (rows, 1), 0)
    inside = ((rid >= halo) | (t > 0)) & ((rid < halo + ts) | (t < nt - 1))
    win_ref[0] = jnp.where(inside, z[:, :d] * _sigmoid(z[:, d:]), 0.0)
    sub = lax.broadcasted_iota(jnp.int32, (rows // SUBLANES, SUBLANES, LANES), 1)
    for cc in range(d // LANES):
        cols = slice(cc * LANES, (cc + 1) * LANES)
        w = win_ref[0, :, cols].reshape(rows // SUBLANES, SUBLANES, LANES)
        for i in range(1, SUBLANES):
            rot = pltpu.roll(w, SUBLANES - i, axis=1)
            nxt = jnp.concatenate([rot[1:], rot[:1]], axis=0)
            win_ref[i, :, cols] = jnp.where(sub < SUBLANES - i, rot, nxt).reshape(rows, LANES)
    off = halo - CONV_WIDTH // 2
    slabs = CONV_ROWS // SUBLANES
    for g in range(ts // CONV_ROWS):
        acc = jnp.broadcast_to(bdw_ref[...], (slabs, SUBLANES, d))
        for k in range(CONV_WIDTH):
            i, base = (off + k) % SUBLANES, g * CONV_ROWS + (off + k) // SUBLANES * SUBLANES
            acc = acc + win_ref[i, base:base + CONV_ROWS, :].reshape(slabs, SUBLANES, d) * w_ref[k][None]
        y = _layer_norm(acc.reshape(CONV_ROWS, d), g_ref[...], b_ref[...])
        o_ref[0, g * CONV_ROWS:(g + 1) * CONV_ROWS, :] = (y * _sigmoid(y)).astype(BF16)


def _conv_core(h, mod, w1, b1, w_dw, b_dw, ln_g, ln_b):
    b, s, d = h.shape
    ts = _tile(s, CONV_TILE)
    nt = s // ts
    hb = ts // CONV_HALO
    nhb = s // CONV_HALO
    w_rep = jnp.broadcast_to(w_dw[:, None, :], (CONV_WIDTH, SUBLANES, d))
    return pl.pallas_call(
        functools.partial(_conv_core_kernel, ts=ts, nt=nt),
        grid=(b, nt),
        in_specs=[pl.BlockSpec((1, CONV_HALO, d), lambda bi, t: (bi, jnp.maximum(t * hb - 1, 0), 0)),
                  _tok_spec(ts, d),
                  pl.BlockSpec((1, CONV_HALO, d), lambda bi, t: (bi, jnp.minimum((t + 1) * hb, nhb - 1), 0)),
                  _mod_spec(), _const_spec(w1.shape), _const_spec((1, 2 * d)),
                  _const_spec(w_rep.shape), _const_spec((1, d)), _const_spec((1, d)), _const_spec((1, d))],
        out_specs=_tok_spec(ts, d),
        out_shape=jax.ShapeDtypeStruct(h.shape, BF16),
        scratch_shapes=[pltpu.VMEM((SUBLANES, ts + 2 * CONV_HALO, d), F32)],
        compiler_params=_cparams("parallel", "arbitrary"),
        name="conv_core",
    )(h, h, h, mod, w1, b1.reshape(1, 2 * d), w_rep, b_dw.reshape(1, d), ln_g.reshape(1, d), ln_b.reshape(1, d))


def _mixer_conv(hx, hc, mx, mc, w1, b1, w_dw, b_dw, cln_g, cln_b, w2, b2, ln_g, ln_b, ctx_out):
    w1 = w1.astype(BF16)
    w2 = w2.astype(BF16)

    def branch(h, mod, name):
        y = _conv_core(h, mod, w1, b1, w_dw, b_dw, cln_g, cln_b)
        return _out_proj([y], [], _plain_prologue, w2, b2, h, mod, ln_g, ln_b, name)

    hx = branch(hx, mx, "conv_out")
    if ctx_out:
        hc = branch(hc, mc, "conv_out_ctx")
    return hx, hc


GLA_NQK = GLA_HEADS * GLA_DK


def _rope_swap(x):
    quarter = GLA_DK // 4
    lane = lax.broadcasted_iota(jnp.int32, (1, GLA_DK), 1)
    first = (lane % (2 * quarter)) < quarter
    parts = []
    for hd in range(GLA_HEADS):
        xh = x[:, hd * GLA_DK:(hd + 1) * GLA_DK]
        up = pltpu.roll(xh, GLA_DK - quarter, axis=1)
        dn = pltpu.roll(xh, quarter, axis=1)
        parts.append(jnp.where(first, up, dn))
    return jnp.concatenate(parts, axis=1)


def _gla_in_kernel(*refs, rotary):
    if rotary:
        h_ref, mod_ref, w_ref, wa2_ref, ba_ref, cos_ref, sin_ref = refs[:7]
        outs = refs[7:]
    else:
        h_ref, mod_ref, w_ref, wa2_ref, ba_ref = refs[:5]
        outs = refs[5:]
    q_ref, k_ref, v_ref, g_ref, laf_ref, lab_ref = outs
    n, d = GLA_NQK, D_MODEL
    u = _modulate(h_ref[0], mod_ref, 1).astype(BF16)
    z = jnp.dot(u, w_ref[...], preferred_element_type=F32)
    q = z[:, :n]
    k = z[:, n:2 * n]
    o = 2 * n
    if rotary:
        cos = cos_ref[...]
        sin = sin_ref[...]
        q = q * cos + _rope_swap(q) * sin
        k = k * cos + _rope_swap(k) * sin
    q_ref[0] = q * (GLA_DK ** -0.5)
    k_ref[0] = k
    v_ref[0] = z[:, o:o + d].astype(BF16)
    g_ref[0] = z[:, o + d:o + 2 * d].astype(BF16)
    t = z[:, o + 2 * d:o + 2 * d + LANES].astype(BF16)
    for dr, la_ref in enumerate((laf_ref, lab_ref)):
        x = jnp.dot(t, wa2_ref[dr], preferred_element_type=F32) + ba_ref[dr:dr + 1, :]
        la_ref[0] = -_softplus(-x) * (1.0 / GLA_NORMALIZER)


def _gla_rope_tables(n):
    quarter = GLA_DK // 4
    t = np.arange(n)
    freqs = ROPE_BASE ** (-jnp.arange(quarter, dtype=F32) / quarter)
    ang_r = jnp.asarray(t // GRID_W, F32)[:, None] * freqs
    ang_c = jnp.asarray(t % GRID_W, F32)[:, None] * freqs
    cos = jnp.concatenate([jnp.cos(ang_r), jnp.cos(ang_r), jnp.cos(ang_c), jnp.cos(ang_c)], axis=-1)
    sin = jnp.concatenate([-jnp.sin(ang_r), jnp.sin(ang_r), -jnp.sin(ang_c), jnp.sin(ang_c)], axis=-1)
    return jnp.tile(cos, (1, GLA_HEADS)), jnp.tile(sin, (1, GLA_HEADS))


def _gla_in(h, mod, w_ext, wa2_ext, ba, rope):
    b, s, d = h.shape
    tm = _tile(s, PROJ_TILE)
    rotary = rope is not None
    in_specs = [_tok_spec(tm, d), _mod_spec(), _const_spec(w_ext.shape), _const_spec(wa2_ext.shape),
                _const_spec(ba.shape)]
    args = [h, mod, w_ext, wa2_ext, ba]
    if rotary:
        in_specs += [pl.BlockSpec((tm, GLA_NQK), lambda bi, t: (t, 0))] * 2
        args += list(rope)
    shp = lambda w, dt: jax.ShapeDtypeStruct((b, s, w), dt)
    return pl.pallas_call(
        functools.partial(_gla_in_kernel, rotary=rotary),
        grid=(b, s // tm),
        in_specs=in_specs,
        out_specs=[_tok_spec(tm, GLA_NQK), _tok_spec(tm, GLA_NQK), _tok_spec(tm, d), _tok_spec(tm, d),
                   _tok_spec(tm, GLA_NQK), _tok_spec(tm, GLA_NQK)],
        out_shape=[shp(GLA_NQK, F32), shp(GLA_NQK, F32), shp(d, BF16), shp(d, BF16),
                   shp(GLA_NQK, F32), shp(GLA_NQK, F32)],
        compiler_params=_cparams("parallel", "parallel"),
        name="gla_in_rope" if rotary else "gla_in",
    )(*args)


def _tri_masks(c):
    row = lax.broadcasted_iota(jnp.int32, (c, c), 0)
    col = lax.broadcasted_iota(jnp.int32, (c, c), 1)
    return row >= col, row <= col


def _gla_core_kernel(qf_ref, kf_ref, vf_ref, laf_ref, qb_ref, kb_ref, vb_ref, lab_ref, s0f_ref, s0b_ref,
                     of_ref, ob_ref, sf_ref, sb_ref, stf_ref, stb_ref, *, nblocks, sub):
    i = pl.program_id(1)
    c = GLA_CHUNK

    @pl.when(i == 0)
    def _():
        stf_ref[...] = s0f_ref[0]
        stb_ref[...] = s0b_ref[0]

    lower, upper = _tri_masks(c)
    dirs = ((qf_ref, kf_ref, vf_ref, laf_ref, of_ref, stf_ref, lower, c - 1, range(sub)),
            (qb_ref, kb_ref, vb_ref, lab_ref, ob_ref, stb_ref, upper, 0, range(sub - 1, -1, -1)))
    chains = []
    for q_ref, k_ref, v_ref, la_ref, o_ref, st_ref, mask, last, order in dirs:
        tri = mask.astype(BF16)
        steps = []
        for j in order:
            rows = slice(j * c, (j + 1) * c)
            hi, lo = _split(la_ref[0, rows, :])
            bcum = jnp.dot(tri, hi, preferred_element_type=F32) + jnp.dot(tri, lo, preferred_element_type=F32)
            blast = bcum[last:last + 1, :]
            k = k_ref[0, rows, :]
            steps.append(dict(rows=rows, dec=jnp.exp(blast), qd=(q_ref[0, rows, :] * jnp.exp(bcum)).astype(BF16),
                              kd=(k * jnp.exp(-bcum)).astype(BF16), kt=(k * jnp.exp(blast - bcum)).astype(BF16)))
        for hd in range(GLA_HEADS):
            ks = slice(hd * GLA_DK, (hd + 1) * GLA_DK)
            vs = slice(hd * GLA_DV, (hd + 1) * GLA_DV)
            chains.append(dict(
                mask=mask, o_ref=o_ref, st_ref=st_ref, hd=hd, vs=vs,
                steps=[dict(rows=t["rows"], dec=t["dec"][:, ks], qd=t["qd"][:, ks], kd=t["kd"][:, ks],
                            kt=t["kt"][:, ks], v=v_ref[0, t["rows"], vs]) for t in steps]))
    for ch in chains:
        for t in ch["steps"]:
            t["att"] = jnp.where(ch["mask"], _dot_nt(t["qd"], t["kd"]), 0.0)
    for ch in chains:
        for t in ch["steps"]:
            t["local"] = _dot(t["att"], t["v"])
            t["upd"] = _dot_tn(t["v"], t["kt"])
    state = [ch["st_ref"][ch["hd"]] for ch in chains]
    for j in range(sub):
        out = [ch["steps"][j]["local"] + _dot_nt(ch["steps"][j]["qd"], s) for ch, s in zip(chains, state)]
        for ch, o in zip(chains, out):
            ch["o_ref"][0, ch["steps"][j]["rows"], ch["vs"]] = o.astype(BF16)
        state = [ch["steps"][j]["dec"] * s + ch["steps"][j]["upd"] for ch, s in zip(chains, state)]
    for ch, s in zip(chains, state):
        ch["st_ref"][ch["hd"]] = s

    @pl.when(i == nblocks - 1)
    def _():
        sf_ref[0] = stf_ref[...]
        sb_ref[0] = stb_ref[...]


def _gla_core(q, k, v, la_f, la_b, s0f, s0b):
    b, s, _ = q.shape
    sub = GLA_CHUNKS_PER_STEP if s % (GLA_CHUNKS_PER_STEP * GLA_CHUNK) == 0 else 1
    rows = sub * GLA_CHUNK
    nblocks = s // rows
    fwd = lambda w: pl.BlockSpec((1, rows, w), lambda bi, i: (bi, i, 0))
    bwd = lambda w: pl.BlockSpec((1, rows, w), lambda bi, i: (bi, nblocks - 1 - i, 0))
    st_shape = (GLA_HEADS, GLA_DV, GLA_DK)
    st_spec = pl.BlockSpec((1,) + st_shape, lambda bi, i: (bi, 0, 0, 0))
    return pl.pallas_call(
        functools.partial(_gla_core_kernel, nblocks=nblocks, sub=sub),
        grid=(b, nblocks),
        in_specs=[fwd(GLA_NQK), fwd(GLA_NQK), fwd(D_MODEL), fwd(GLA_NQK),
                  bwd(GLA_NQK), bwd(GLA_NQK), bwd(D_MODEL), bwd(GLA_NQK), st_spec, st_spec],
        out_specs=[fwd(D_MODEL), bwd(D_MODEL), st_spec, st_spec],
        out_shape=[jax.ShapeDtypeStruct((b, s, D_MODEL), BF16)] * 2
        + [jax.ShapeDtypeStruct((b,) + st_shape, F32)] * 2,
        scratch_shapes=[pltpu.VMEM(st_shape, F32), pltpu.VMEM(st_shape, F32)],
        compiler_params=_cparams("parallel", "arbitrary"),
        name="gla_core",
    )(q, k, v, la_f, q, k, v, la_b, s0f, s0b)


def _gla_finish_prologue(of_ref, ob_ref, g_ref, ng_ref):
    o = of_ref[0].astype(F32) + ob_ref[0].astype(F32)
    g = g_ref[0].astype(F32)
    ng = ng_ref[...]
    parts = []
    for hd in range(GLA_HEADS):
        vs = slice(hd * GLA_DV, (hd + 1) * GLA_DV)
        oh = o[:, vs]
        gh = g[:, vs]
        oh = oh * lax.rsqrt(jnp.mean(oh * oh, axis=-1, keepdims=True) + LN_EPS) * ng
        parts.append((oh * (gh * _sigmoid(gh))).astype(BF16))
    return jnp.concatenate(parts, axis=1)


def _mixer_gla(hx, hc, mx, mc, w_in, w_a1, w_a2, b_a, norm_g, w_o, ln_g, ln_b, ctx_out):
    b, n, d = hx.shape
    nqk = GLA_NQK
    pad = jnp.zeros((d, LANES - 2 * GLA_GATE_RANK), F32)
    w_ext = jnp.concatenate([w_in, w_a1[0], w_a1[1], pad], axis=1).astype(BF16)
    wa2_ext = jnp.zeros((2, LANES, nqk), F32)
    wa2_ext = wa2_ext.at[0, :GLA_GATE_RANK].set(w_a2[0])
    wa2_ext = wa2_ext.at[1, GLA_GATE_RANK:2 * GLA_GATE_RANK].set(w_a2[1]).astype(BF16)
    w_o = w_o.astype(BF16)

    qc, kc, vc, gc, lcf, lcb = _gla_in(hc, mc, w_ext, wa2_ext, b_a, None)
    qx, kx, vx, gx, lxf, lxb = _gla_in(hx, mx, w_ext, wa2_ext, b_a, _gla_rope_tables(n))
    s0 = jnp.zeros((b, GLA_HEADS, GLA_DV, GLA_DK), F32)
    ocf, ocb, scf, scb = _gla_core(qc, kc, vc, lcf, lcb, s0, s0)
    oxf, oxb, _, _ = _gla_core(qx, kx, vx, lxf, lxb, scf, scb)
    ng = norm_g.reshape(1, GLA_DV)
    hx = _out_proj([oxf, oxb, gx], [ng], _gla_finish_prologue, w_o, None, hx, mx, ln_g, ln_b, "gla_out")
    if ctx_out:
        hc = _out_proj([ocf, ocb, gc], [ng], _gla_finish_prologue, w_o, None, hc, mc, ln_g, ln_b, "gla_out_ctx")
    return hx, hc


def _head_sum_matrices():
    g = np.zeros((D_MODEL, LANES), np.float32)
    g[np.arange(D_MODEL), np.arange(D_MODEL) // RW_HEAD] = 1.0
    return jnp.asarray(g, BF16), jnp.asarray(g.T.copy(), BF16)


def _rw_prep_kernel(hp_ref, h_ref, hn_ref, mod_ref, mu_ref, wr_ref, wk_ref, wv_ref, g1_ref, g2_ref,
                    w1_ref, w2_ref, a1_ref, a2_ref, w0_ref, a0_ref, kkw_ref, kaw_ref, rkw_ref, gs_ref, gt_ref,
                    r_ref, kk_ref, v_ref, g_ref, bonus_ref,
                    lw0_ref, k0_ref, b0_ref, lw1_ref, k1_ref, b1_ref, *, tm, nt):
    t = pl.program_id(1)
    u = _modulate(h_ref[0], mod_ref, 1)
    up = _modulate(hp_ref[0, 7:8, :], mod_ref, 1)
    un = _modulate(hn_ref[0, 0:1, :], mod_ref, 1)
    up = jnp.where(t > 0, up, 0.0)
    un = jnp.where(t < nt - 1, un, 0.0)
    rowid = lax.broadcasted_iota(jnp.int32, (tm, 1), 0)
    prev = jnp.where(rowid == 0, up, pltpu.roll(u, 1, axis=0))
    nxt = jnp.where(rowid == tm - 1, un, pltpu.roll(u, tm - 1, axis=0))
    xx = 0.5 * (prev + nxt) - u
    mix = lambda j: (u + xx * mu_ref[j:j + 1, :]).astype(BF16)
    gs = gs_ref[...]
    gt = gt_ref[...]

    r = jnp.dot(mix(0), wr_ref[...], preferred_element_type=F32)
    k = jnp.dot(mix(2), wk_ref[...], preferred_element_type=F32)
    v = jnp.dot(mix(3), wv_ref[...], preferred_element_type=F32)
    hw = jnp.tanh(jnp.dot(mix(1), w1_ref[...], preferred_element_type=F32)).astype(BF16)
    ha = jnp.dot(mix(4), a1_ref[...], preferred_element_type=F32).astype(BF16)
    gg = _sigmoid(jnp.dot(mix(5), g1_ref[...], preferred_element_type=F32)).astype(BF16)
    g_ref[0] = jnp.dot(gg, g2_ref[...], preferred_element_type=F32).astype(BF16)

    kk = k * kkw_ref[...]
    nrm = jnp.maximum(jnp.sqrt(_dot_sel(kk * kk, gs)), 1e-12)
    kk = kk * _dot_sel(1.0 / nrm, gt)
    ksum = jnp.zeros_like(k)
    for dr, (lw_ref, kd_ref, bd_ref) in enumerate(((lw0_ref, k0_ref, b0_ref), (lw1_ref, k1_ref, b1_ref))):
        xw = w0_ref[dr:dr + 1, :] + jnp.dot(hw, w2_ref[dr], preferred_element_type=F32)
        lw_ref[0] = -float(np.exp(-0.5)) * _sigmoid(xw)
        a = _sigmoid(a0_ref[dr:dr + 1, :] + jnp.dot(ha, a2_ref[dr], preferred_element_type=F32))
        kd = k * (1.0 + (a - 1.0) * kaw_ref[...])
        kd_ref[0] = kd
        bd_ref[0] = kk * a
        ksum = ksum + kd
    r_ref[0] = r
    kk_ref[0] = kk
    v_ref[0] = v.astype(BF16)
    bonus_ref[0] = (_dot_sel(_dot_sel(r * rkw_ref[...] * ksum, gs), gt) * v).astype(BF16)


def _rw_prep(h, mod, consts):
    b, s, d = h.shape
    tm = _tile(s, RW_PREP_TILE)
    nt = s // tm
    hb = tm // 8
    nhb = s // 8
    in_specs = [pl.BlockSpec((1, 8, d), lambda bi, t: (bi, jnp.maximum(t * hb - 1, 0), 0)),
                _tok_spec(tm, d),
                pl.BlockSpec((1, 8, d), lambda bi, t: (bi, jnp.minimum((t + 1) * hb, nhb - 1), 0)),
                _mod_spec()] + [_const_spec(a.shape) for a in consts]
    f32o = jax.ShapeDtypeStruct(h.shape, F32)
    bf16o = jax.ShapeDtypeStruct(h.shape, BF16)
    out_shape = [f32o, f32o, bf16o, bf16o, bf16o] + [f32o] * 6
    return pl.pallas_call(
        functools.partial(_rw_prep_kernel, tm=tm, nt=nt),
        grid=(b, nt),
        in_specs=in_specs,
        out_specs=[_tok_spec(tm, d)] * 11,
        out_shape=out_shape,
        compiler_params=_cparams("parallel", "arbitrary"),
        name="rw_prep",
    )(h, h, h, mod, *consts)


def _rw_prelude(refs, forward, rows):
    r_ref, kk_ref, v_ref, lw_ref, k_ref, b_ref = (ref.at[0, rows, :] for ref in refs)
    c = RW_CHUNK
    row = lax.broadcasted_iota(jnp.int32, (c, 4 * c), 0)
    col = lax.broadcasted_iota(jnp.int32, (c, 4 * c), 1) % c
    lower, upper = _tri_masks(c)
    if forward:
        strict, incl, tri, last = row > col, row >= col, lower, c - 1
    else:
        strict, incl, tri, last = row < col, row <= col, upper, 0
    tri = tri.astype(BF16)
    lw = lw_ref[...]
    hi, lo = _split(lw)
    cum = jnp.dot(tri, hi, preferred_element_type=F32) + jnp.dot(tri, lo, preferred_element_type=F32)
    tot = cum[last:last + 1, :]
    e_neg = jnp.exp(-cum)
    e_tail = jnp.exp(tot - cum)
    b_raw = b_ref[...]
    k_raw = k_ref[...]
    return dict(
        rows=rows, strict=strict, incl=incl, dec=jnp.exp(tot),
        a=(kk_ref[...] * jnp.exp(cum - lw)).astype(BF16), r=(r_ref[...] * jnp.exp(cum)).astype(BF16),
        bt=(b_raw * e_neg).astype(BF16), kt=(k_raw * e_neg).astype(BF16),
        bh=(b_raw * e_tail).astype(BF16), kh=(k_raw * e_tail).astype(BF16), v=v_ref[...])


def _rw_scan_block(pres_f, pres_b, stf_ref, stb_ref, yf_ref, yb_ref):
    c = RW_CHUNK
    rowp = lax.broadcasted_iota(jnp.int32, (c, 2 * c), 0)
    colp = lax.broadcasted_iota(jnp.int32, (c, 2 * c), 1) % c
    eye = (rowp == colp).astype(F32)
    blocks = [(rowp // s) == (colp // s) for s in (2 ** e for e in range(1, int(np.log2(c)) + 1))]
    row2 = lax.broadcasted_iota(jnp.int32, (LANES, LANES), 0)
    col2 = lax.broadcasted_iota(jnp.int32, (LANES, LANES), 1)
    same_head = (row2 < RW_HEAD) == (col2 < RW_HEAD)
    lane0 = lax.broadcasted_iota(jnp.int32, (1, LANES), 1) < RW_HEAD
    zero = jnp.zeros((c, LANES), BF16)

    def bdiag(x):
        x = x.astype(BF16)
        return jnp.concatenate([jnp.where(lane0, x, zero), jnp.where(lane0, zero, x)], axis=0)

    chains = [(pres, st_ref, y_ref, slice(p * LANES, (p + 1) * LANES), p)
              for pres, st_ref, y_ref in ((pres_f, stf_ref, yf_ref), (pres_b, stb_ref, yb_ref))
              for p in range(RW_PAIRS)]
    free = [(pre, {nm: pre[nm][:, sl] for nm in ("a", "r", "bt", "kt", "bh", "kh", "v")})
            for pres, _, _, sl, _ in chains for pre in pres]
    n = len(free)
    ar = [jnp.concatenate([t["a"], t["r"]], axis=0) for _, t in free]
    gram = [_dot_nt(ar[i], jnp.concatenate([bdiag(free[i][1]["bt"]), bdiag(free[i][1]["kt"])], axis=0))
            for i in range(n)]
    n1 = [jnp.where(free[i][0]["strict"][:, :2 * c], gram[i][:c, :2 * c], 0.0) for i in range(n)]
    a_ak = [jnp.where(free[i][0]["strict"][:, :2 * c], gram[i][:c, 2 * c:], 0.0) for i in range(n)]
    r_bk = [jnp.where(free[i][0]["incl"], gram[i][c:], 0.0) for i in range(n)]
    vbd = [bdiag(t["v"]) for _, t in free]
    w0 = [_dot(a_ak[i], vbd[i]) for i in range(n)]
    inv = [eye - jnp.where(blocks[0], n1[i], 0.0) for i in range(n)]
    for lvl in range(1, len(blocks)):
        sel = blocks[lvl] & ~blocks[lvl - 1]
        tmp = [_dot(jnp.where(sel, n1[i], 0.0), bdiag(inv[i])) for i in range(n)]
        inv = [inv[i] - _dot(inv[i], bdiag(tmp[i])) for i in range(n)]
    nsub = len(pres_f)
    zt = [st_ref[p] for _, st_ref, _, _, p in chains]
    for j in range(nsub):
        idx = [ci * nsub + j for ci in range(len(chains))]
        sz = [_dot_nt(ar[i], z) for i, z in zip(idx, zt)]
        u = [-_dot(inv[i], bdiag(s[:c] + w0[i])) for i, s in zip(idx, sz)]
        ys = [_dot(r_bk[i], jnp.concatenate([bdiag(uu), vbd[i]], axis=0)) for i, uu in zip(idx, u)]
        for (_, _, y_ref, sl, _), i, s, y in zip(chains, idx, sz, ys):
            y_ref[0, free[i][0]["rows"], sl] = (s[c:] + y).astype(BF16)
        upd = [_dot_tn(jnp.concatenate([uu.astype(BF16), free[i][1]["v"]], axis=0),
                       jnp.concatenate([free[i][1]["bh"], free[i][1]["kh"]], axis=0)) for i, uu in zip(idx, u)]
        zt = [free[i][0]["dec"][:, sl] * z + jnp.where(same_head, up, 0.0)
              for (_, _, _, sl, _), i, z, up in zip(chains, idx, zt, upd)]
    for (_, st_ref, _, _, p), z in zip(chains, zt):
        st_ref[p] = z


def _rw_scan_kernel(*refs, nblocks, sub):
    fwd_in, bwd_in = refs[0:6], refs[6:12]
    s0f_ref, s0b_ref, yf_ref, yb_ref, sf_ref, sb_ref, stf_ref, stb_ref = refs[12:]
    i = pl.program_id(1)
    c = RW_CHUNK

    @pl.when(i == 0)
    def _():
        stf_ref[...] = s0f_ref[0]
        stb_ref[...] = s0b_ref[0]

    pres_f = [_rw_prelude(fwd_in, True, slice(j * c, (j + 1) * c)) for j in range(sub)]
    pres_b = [_rw_prelude(bwd_in, False, slice(j * c, (j + 1) * c)) for j in range(sub - 1, -1, -1)]
    _rw_scan_block(pres_f, pres_b, stf_ref, stb_ref, yf_ref, yb_ref)

    @pl.when(i == nblocks - 1)
    def _():
        sf_ref[0] = stf_ref[...]
        sb_ref[0] = stb_ref[...]


def _rw_scan(r, kk, v, dirs, s0f, s0b):
    b, s, d = r.shape
    sub = RW_CHUNKS_PER_STEP if s % (RW_CHUNKS_PER_STEP * RW_CHUNK) == 0 else 1
    rows = sub * RW_CHUNK
    nblocks = s // rows
    fwd = pl.BlockSpec((1, rows, d), lambda bi, i: (bi, i, 0))
    bwd = pl.BlockSpec((1, rows, d), lambda bi, i: (bi, nblocks - 1 - i, 0))
    st_shape = (RW_PAIRS, LANES, LANES)
    st_spec = pl.BlockSpec((1,) + st_shape, lambda bi, i: (bi, 0, 0, 0))
    (lw0, k0, b0), (lw1, k1, b1) = dirs
    return pl.pallas_call(
        functools.partial(_rw_scan_kernel, nblocks=nblocks, sub=sub),
        grid=(b, nblocks),
        in_specs=[fwd] * 6 + [bwd] * 6 + [st_spec, st_spec],
        out_specs=[fwd, bwd, st_spec, st_spec],
        out_shape=[jax.ShapeDtypeStruct((b, s, d), BF16)] * 2 + [jax.ShapeDtypeStruct((b,) + st_shape, F32)] * 2,
        scratch_shapes=[pltpu.VMEM(st_shape, F32), pltpu.VMEM(st_shape, F32)],
        compiler_params=_cparams("parallel", "arbitrary"),
        name="rw_scan",
    )(r, kk, v, lw0, k0, b0, r, kk, v, lw1, k1, b1, s0f, s0b)


def _rw_finish_prologue(yf_ref, yb_ref, bonus_ref, g_ref, gs_ref, gt_ref, gng_ref, gnb_ref):
    y = yf_ref[0].astype(F32) + yb_ref[0].astype(F32)
    gs = gs_ref[...]
    gt = gt_ref[...]
    inv_n = 1.0 / RW_HEAD
    mean = _dot_sel(_dot_sel(y, gs) * inv_n, gt)
    yc = y - mean
    var = _dot_sel(yc * yc, gs) * inv_n
    rstd = _dot_sel(lax.rsqrt(var + RW_GN_EPS), gt)
    yn = yc * rstd * gng_ref[...] + gnb_ref[...]
    return ((yn + bonus_ref[0].astype(F32)) * g_ref[0].astype(F32)).astype(BF16)


def _mixer_rwkv(hx, hc, mx, mc, mu, w_rkv, w0, w1, w2, a0, a1, a2, g1, g2, k_k, k_a, r_k, gn_g, gn_b, w_o,
                ln_g, ln_b, ctx_out):
    b, n, d = hx.shape
    gs, gt = _head_sum_matrices()
    rank_w, rank_a = w1.shape[-1], a1.shape[-1]
    assert 2 * rank_w == LANES and 2 * rank_a == LANES

    def ext(m2):
        z = jnp.zeros((2, LANES, d), F32)
        z = z.at[0, :m2.shape[1]].set(m2[0])
        return z.at[1, m2.shape[1]:].set(m2[1]).astype(BF16)

    consts = [mu, w_rkv[0].astype(BF16), w_rkv[1].astype(BF16), w_rkv[2].astype(BF16),
              g1.astype(BF16), g2.astype(BF16),
              jnp.concatenate([w1[0], w1[1]], axis=1).astype(BF16), ext(w2),
              jnp.concatenate([a1[0], a1[1]], axis=1).astype(BF16), ext(a2),
              w0, a0, k_k.reshape(1, d), k_a.reshape(1, d), r_k.reshape(1, d), gs, gt]
    w_o = w_o.astype(BF16)

    def prep(h, mod):
        r, kk, v, g, bonus, lw0, k0, b0, lw1, k1, b1 = _rw_prep(h, mod, consts)
        return r, kk, v, g, bonus, ((lw0, k0, b0), (lw1, k1, b1))

    rc, kkc, vc, gc, bonc, dc = prep(hc, mc)
    rx, kkx, vx, gx, bonx, dx = prep(hx, mx)
    s0 = jnp.zeros((b, RW_PAIRS, LANES, LANES), F32)
    ycf, ycb, scf, scb = _rw_scan(rc, kkc, vc, dc, s0, s0)
    yxf, yxb, _, _ = _rw_scan(rx, kkx, vx, dx, scf, scb)
    fin_consts = [gs, gt, gn_g.reshape(1, d), gn_b.reshape(1, d)]
    hx = _out_proj([yxf, yxb, bonx, gx], fin_consts, _rw_finish_prologue, w_o, None, hx, mx, ln_g, ln_b, "rw_out",
                   tile=RW_PREP_TILE)
    if ctx_out:
        hc = _out_proj([ycf, ycb, bonc, gc], fin_consts, _rw_finish_prologue, w_o, None, hc, mc, ln_g, ln_b,
                       "rw_out_ctx", tile=RW_PREP_TILE)
    return hx, hc


def kernel(x, c, ctx, c_ctx, ada_w, ada_b, ln_g, ln_b, ffn_w13, ffn_w2, na_wqkv, na_wo, na_rpb, cv_w1, cv_b1, cv_wdw, cv_bdw, cv_ln_g, cv_ln_b, cv_w2, cv_b2, gla_win, gla_wa1, gla_wa2, gla_ba, gla_norm_g, gla_wo, rw_mu, rw_wrkv, rw_w0, rw_w1, rw_w2, rw_a0, rw_a1, rw_a2, rw_g1, rw_g2, rw_kk, rw_ka, rw_rk, rw_gn_g, rw_gn_b, rw_wo):
    b, n, d = x.shape
    depth = ada_w.shape[0]
    assert d == D_MODEL and depth == DEPTH and n % GRID_W == 0
    rows = -(-(b + 1) // 8) * 8
    cond = jnp.zeros((rows, d), F32).at[:b].set(c).at[b].set(c_ctx)
    mod_all = _ada(cond, ada_w, ada_b)
    w13 = ffn_w13.astype(BF16)
    w2 = ffn_w2.astype(BF16)
    hx, hc = x, ctx
    for i in range(depth):
        last = i == depth - 1
        mx = mod_all[i, :b].reshape(b, 3 * N_SUB, d)
        mc = jnp.broadcast_to(mod_all[i, b].reshape(1, 3 * N_SUB, d), (b, 3 * N_SUB, d))
        hx, hc = _ffn_pair(hx, hc, mx, mc, w13, w2, i, 0, ln_g[i, 0], ln_b[i, 0], 0)
        kind, j = i % 4, i // 4
        lg, lb = ln_g[i, 1], ln_b[i, 1]
        if kind == 0:
            hx, hc = _mixer_na(hx, hc, mx, mc, na_wqkv[j], na_wo[j], na_rpb[j], lg, lb, not last)
        elif kind == 1:
            hx, hc = _mixer_conv(hx, hc, mx, mc, cv_w1[j], cv_b1[j], cv_wdw[j], cv_bdw[j], cv_ln_g[j], cv_ln_b[j],
                                 cv_w2[j], cv_b2[j], lg, lb, not last)
        elif kind == 2:
            hx, hc = _mixer_gla(hx, hc, mx, mc, gla_win[j], gla_wa1[j], gla_wa2[j], gla_ba[j], gla_norm_g[j],
                                gla_wo[j], lg, lb, not last)
        else:
            hx, hc = _mixer_rwkv(hx, hc, mx, mc, rw_mu[j], rw_wrkv[j], rw_w0[j], rw_w1[j], rw_w2[j], rw_a0[j],
                                 rw_a1[j], rw_a2[j], rw_g1[j], rw_g2[j], rw_kk[j], rw_ka[j], rw_rk[j],
                                 rw_gn_g[j], rw_gn_b[j], rw_wo[j], lg, lb, not last)
        if last:
            hx = _ffn(hx, mx, w13, w2, i, 1, ln_g[i, 2], ln_b[i, 2], 2)
        else:
            hx, hc = _ffn_pair(hx, hc, mx, mc, w13, w2, i, 1, ln_g[i, 2], ln_b[i, 2], 2)
    return hx
```

```python
import functools

import numpy as np
import jax
import jax.numpy as jnp
from jax import lax
from jax.experimental import pallas as pl
from jax.experimental.pallas import tpu as pltpu

F32 = jnp.float32
BF16 = jnp.bfloat16

D_MODEL = 1024
DEPTH = 4
N_SUB = 3
GRID_W = 64
ALPHA = (2.0 * DEPTH) ** 0.25
LN_EPS = 1e-5
D_FF = 2816
NA_HEADS = 16
NA_HEAD_DIM = D_MODEL // NA_HEADS
NA_KH = 8
NA_KW = 16
CONV_WIDTH = 31
CONV_HALO = 16
GLA_HEADS = 4
GLA_DK = 128
GLA_DV = 256
GLA_GATE_RANK = 16
GLA_NORMALIZER = 16.0
GLA_CHUNK = 64
GLA_CHUNKS_PER_STEP = 4
ROPE_BASE = 10000.0
RW_HEAD = 64
RW_HEADS = D_MODEL // RW_HEAD
RW_PAIRS = RW_HEADS // 2
RW_CHUNK = 64
RW_CHUNKS_PER_STEP = 1
RW_GN_EPS = 64e-5
LANES = 128
SUBLANES = 8
NEG_BIG = -1e30
VMEM_LIMIT = 56 * 1024 * 1024
FFN_TILE = 512
PROJ_TILE = 512
CONV_TILE = 512
RW_PREP_TILE = 256


def _cparams(*sem):
    return pltpu.CompilerParams(dimension_semantics=sem, vmem_limit_bytes=VMEM_LIMIT)


def _dot(a, b):
    return jnp.dot(a.astype(BF16), b.astype(BF16), preferred_element_type=F32)


def _dot_nt(a, b):
    return lax.dot_general(a.astype(BF16), b.astype(BF16), (((1,), (1,)), ((), ())),
                           preferred_element_type=F32)


def _dot_tn(a, b):
    return lax.dot_general(a.astype(BF16), b.astype(BF16), (((0,), (0,)), ((), ())),
                           preferred_element_type=F32)


def _split(x):
    hi = x.astype(BF16)
    lo = (x - hi.astype(F32)).astype(BF16)
    return hi, lo


def _dot_sel(x, sel):
    hi, lo = _split(x)
    return (jnp.dot(hi, sel, preferred_element_type=F32)
            + jnp.dot(lo, sel, preferred_element_type=F32))


def _sigmoid(x):
    return jax.nn.sigmoid(x)


def _softplus(x):
    return jnp.maximum(x, 0.0) + jnp.log(1.0 + jnp.exp(-jnp.abs(x)))


def _layer_norm(x, g, b, eps=LN_EPS):
    mu = jnp.mean(x, axis=-1, keepdims=True)
    xc = x - mu
    var = jnp.mean(xc * xc, axis=-1, keepdims=True)
    return xc * lax.rsqrt(var + eps) * g + b


def _mod_rows(mod_ref, j):
    return (mod_ref[0, 3 * j:3 * j + 1, :], mod_ref[0, 3 * j + 1:3 * j + 2, :],
            mod_ref[0, 3 * j + 2:3 * j + 3, :])


def _modulate(h, mod_ref, j):
    shift, scale, _ = _mod_rows(mod_ref, j)
    return h * (1.0 + scale) + shift


def _post_norm(h, y, mod_ref, j, g_ref, b_ref):
    gate = mod_ref[0, 3 * j + 2:3 * j + 3, :]
    return _layer_norm(ALPHA * h + gate * y, g_ref[...], b_ref[...])


def _tok_spec(tm, d):
    return pl.BlockSpec((1, tm, d), lambda b, t: (b, t, 0))


def _mod_spec():
    return pl.BlockSpec((1, 3 * N_SUB, D_MODEL), lambda b, t: (b, 0, 0))


def _const_spec(shape):
    nd = len(shape)
    return pl.BlockSpec(shape, lambda b, t: (0,) * nd, pipeline_mode=pl.Buffered(1))


def _tile(s, pref):
    tm = min(pref, s)
    assert s % tm == 0
    return tm


def _ada_kernel(cond_ref, w_ref, b_ref, o_ref):
    c = cond_ref[...]
    o_ref[0] = _dot(c * _sigmoid(c), w_ref[0]) + b_ref[0]


def _ada(cond, ada_w, ada_b):
    r, d = cond.shape
    depth, _, n = ada_w.shape
    tn = 1024
    return pl.pallas_call(
        _ada_kernel,
        grid=(depth, n // tn),
        in_specs=[pl.BlockSpec((r, d), lambda l, j: (0, 0)),
                  pl.BlockSpec((1, d, tn), lambda l, j: (l, 0, j)),
                  pl.BlockSpec((1, 1, tn), lambda l, j: (l, 0, j))],
        out_specs=pl.BlockSpec((1, r, tn), lambda l, j: (l, 0, j)),
        out_shape=jax.ShapeDtypeStruct((depth, r, n), F32),
        compiler_params=_cparams("parallel", "parallel"),
        name="ada_mod",
    )(cond, ada_w, ada_b.reshape(depth, 1, n))


FFN_CHUNK = 256


def _ffn_tile(h_ref, mod_ref, w13_ref, w2_ref, g_ref, b_ref, o_ref, j):
    h = h_ref[0]
    u = _modulate(h, mod_ref, j).astype(BF16)
    acc = jnp.zeros(h.shape, F32)
    for lo in range(0, D_FF, FFN_CHUNK):
        hi = min(lo + FFN_CHUNK, D_FF)
        a = jnp.dot(u, w13_ref[:, lo:hi], preferred_element_type=F32)
        v = jnp.dot(u, w13_ref[:, D_FF + lo:D_FF + hi], preferred_element_type=F32)
        z = (a * _sigmoid(a) * v).astype(BF16)
        acc = acc + jnp.dot(z, w2_ref[lo:hi, :], preferred_element_type=F32)
    o_ref[0] = _post_norm(h, 0.5 * acc, mod_ref, j, g_ref, b_ref)


def _ffn_kernel(h_ref, mod_ref, w13_ref, w2_ref, g_ref, b_ref, o_ref, *, j):
    _ffn_tile(h_ref, mod_ref, w13_ref, w2_ref, g_ref, b_ref, o_ref, j)


def _ffn_pair_kernel(hx_ref, hc_ref, mx_ref, mc_ref, w13_ref, w2_ref, g_ref, b_ref, ox_ref, oc_ref, *, j, nx):
    t = pl.program_id(1)

    @pl.when(t < nx)
    def _():
        _ffn_tile(hx_ref, mx_ref, w13_ref, w2_ref, g_ref, b_ref, ox_ref, j)

    @pl.when(t == nx)
    def _():
        _ffn_tile(hc_ref, mc_ref, w13_ref, w2_ref, g_ref, b_ref, oc_ref, j)


def _ffn_weight_spec(shape, layer, half):
    return pl.BlockSpec((None, None) + shape[2:], lambda bi, t: (layer, half, 0, 0), pipeline_mode=pl.Buffered(1))


def _ffn_pair(hx, hc, mx, mc, w13_all, w2_all, layer, half, ln_g, ln_b, j):
    b, n, d = hx.shape
    l = hc.shape[1]
    tm = _tile(n, FFN_TILE)
    nx = n // tm
    x_spec = pl.BlockSpec((1, tm, d), lambda bi, t: (bi, jnp.minimum(t, nx - 1), 0))
    c_spec = pl.BlockSpec((1, l, d), lambda bi, t: (bi, 0, 0))
    return pl.pallas_call(
        functools.partial(_ffn_pair_kernel, j=j, nx=nx),
        grid=(b, nx + 1),
        in_specs=[x_spec, c_spec, _mod_spec(), _mod_spec(), _ffn_weight_spec(w13_all.shape, layer, half),
                  _ffn_weight_spec(w2_all.shape, layer, half), _const_spec((1, d)), _const_spec((1, d))],
        out_specs=[x_spec, c_spec],
        out_shape=[jax.ShapeDtypeStruct(hx.shape, F32), jax.ShapeDtypeStruct(hc.shape, F32)],
        compiler_params=_cparams("parallel", "arbitrary"),
        name="ffn_pair",
    )(hx, hc, mx, mc, w13_all, w2_all, ln_g.reshape(1, d), ln_b.reshape(1, d))


def _ffn(h, mod, w13_all, w2_all, layer, half, ln_g, ln_b, j):
    b, s, d = h.shape
    tm = _tile(s, FFN_TILE)
    pick = lambda shape: _ffn_weight_spec(shape, layer, half)
    return pl.pallas_call(
        functools.partial(_ffn_kernel, j=j),
        grid=(b, s // tm),
        in_specs=[_tok_spec(tm, d), _mod_spec(), pick(w13_all.shape), pick(w2_all.shape),
                  _const_spec((1, d)), _const_spec((1, d))],
        out_specs=_tok_spec(tm, d),
        out_shape=jax.ShapeDtypeStruct(h.shape, F32),
        compiler_params=_cparams("parallel", "parallel"),
        name="ffn_half",
    )(h, mod, w13_all, w2_all, ln_g.reshape(1, d), ln_b.reshape(1, d))


def _out_kernel(*refs, n_in, prologue, has_bias):
    ins = refs[:n_in]
    w_ref = refs[n_in]
    k = n_in + 1
    bias_ref = None
    if has_bias:
        bias_ref = refs[k]
        k += 1
    h_ref, mod_ref, g_ref, b_ref, o_ref = refs[k:k + 5]
    y = jnp.dot(prologue(*ins), w_ref[...], preferred_element_type=F32)
    if has_bias:
        y = y + bias_ref[...]
    o_ref[0] = _post_norm(h_ref[0], y, mod_ref, 1, g_ref, b_ref)


def _out_proj(tok_inputs, const_inputs, prologue, w, bias, h, mod, ln_g, ln_b, name, tile=PROJ_TILE):
    b, s, d = h.shape
    tm = _tile(s, tile)
    n_in = len(tok_inputs) + len(const_inputs)
    in_specs = [_tok_spec(tm, a.shape[-1]) for a in tok_inputs]
    in_specs += [_const_spec(a.shape) for a in const_inputs]
    in_specs.append(_const_spec(w.shape))
    args = list(tok_inputs) + list(const_inputs) + [w]
    if bias is not None:
        in_specs.append(_const_spec((1, d)))
        args.append(bias.reshape(1, d))
    in_specs += [_tok_spec(tm, d), _mod_spec(), _const_spec((1, d)), _const_spec((1, d))]
    args += [h, mod, ln_g.reshape(1, d), ln_b.reshape(1, d)]
    return pl.pallas_call(
        functools.partial(_out_kernel, n_in=n_in, prologue=prologue, has_bias=bias is not None),
        grid=(b, s // tm),
        in_specs=in_specs,
        out_specs=_tok_spec(tm, d),
        out_shape=jax.ShapeDtypeStruct(h.shape, F32),
        compiler_params=_cparams("parallel", "parallel"),
        name=name,
    )(*args)


def _plain_prologue(y_ref):
    return y_ref[0].astype(BF16)


def _na_qkv_kernel(h_ref, mod_ref, w_ref, q_ref, k_ref, v_ref):
    d = D_MODEL
    u = _modulate(h_ref[0], mod_ref, 1).astype(BF16)
    z = jnp.dot(u, w_ref[...], preferred_element_type=F32)
    q_ref[0] = (z[:, :d] * (NA_HEAD_DIM ** -0.5)).astype(BF16)
    k_ref[0] = z[:, d:2 * d].astype(BF16)
    v_ref[0] = z[:, 2 * d:].astype(BF16)


def _na_qkv(h, mod, w):
    b, s, d = h.shape
    tm = _tile(s, PROJ_TILE)
    out = jax.ShapeDtypeStruct(h.shape, BF16)
    return pl.pallas_call(
        _na_qkv_kernel,
        grid=(b, s // tm),
        in_specs=[_tok_spec(tm, d), _mod_spec(), _const_spec(w.shape)],
        out_specs=[_tok_spec(tm, d)] * 3,
        out_shape=[out] * 3,
        compiler_params=_cparams("parallel", "parallel"),
        name="na_qkv",
    )(h, mod, w)


def _na_row_start(r, rows, kh):
    return jnp.clip(r - kh // 2, 0, rows - kh)


def _na_attn_kernel(q_ref, k_ref, v_ref, kc_ref, vc_ref, bias_ref, o_ref, *, rows, kh):
    r = pl.program_id(1)
    start = pl.multiple_of(_na_row_start(r, rows, kh) * GRID_W, GRID_W)
    win = kh * GRID_W
    lane0 = lax.broadcasted_iota(jnp.int32, (1, LANES), 1) < NA_HEAD_DIM
    zero = jnp.zeros((GRID_W, LANES), BF16)
    pairs = range(NA_HEADS // 2)
    sls = [slice(p * LANES, (p + 1) * LANES) for p in pairs]
    qbd = []
    for sl in sls:
        q = q_ref[0, :, sl]
        qbd.append(jnp.concatenate([jnp.where(lane0, q, zero), jnp.where(lane0, zero, q)], axis=0))
    s = _na_row_start(r, rows, kh) - r + NA_KH - 1
    bias = [jnp.concatenate([bias_ref[s + a, p] for a in range(kh)], axis=0) for p in pairs]
    s_loc = [_dot_nt(k_ref[0, pl.ds(start, win), sls[p]], qbd[p]) + bias[p] for p in pairs]
    s_ctx = [_dot_nt(kc_ref[0, :, sls[p]], qbd[p]) for p in pairs]
    m = [jnp.maximum(jnp.max(s_loc[p], axis=0, keepdims=True), jnp.max(s_ctx[p], axis=0, keepdims=True))
         for p in pairs]
    p_loc = [jnp.exp(s_loc[p] - m[p]) for p in pairs]
    p_ctx = [jnp.exp(s_ctx[p] - m[p]) for p in pairs]
    inv = [1.0 / (jnp.sum(p_loc[p], axis=0, keepdims=True) + jnp.sum(p_ctx[p], axis=0, keepdims=True))
           for p in pairs]
    o = [_dot_tn(p_loc[p] * inv[p], v_ref[0, pl.ds(start, win), sls[p]])
         + _dot_tn(p_ctx[p] * inv[p], vc_ref[0, :, sls[p]]) for p in pairs]
    for p in pairs:
        o_ref[0, :, sls[p]] = jnp.where(lane0, o[p][:GRID_W], o[p][GRID_W:]).astype(BF16)


def _na_bias_table(rpb):
    w = GRID_W
    nrow = 2 * NA_KH - 1
    qc = np.arange(w)[:, None]
    kc = np.arange(w)[None, :]
    cstart = np.clip(qc - NA_KW // 2, 0, w - NA_KW)
    ok = (kc >= cstart) & (kc < cstart + NA_KW)
    lpad = w - NA_KW
    g = jnp.pad(rpb, ((0, 0), (0, 0), (lpad, lpad)))
    flat = jnp.broadcast_to(g[:, :, None, :], (NA_HEADS, nrow, w, 2 * w - 1)).reshape(NA_HEADS, nrow, -1)
    t = flat[:, :, w - 1:w - 1 + w * (2 * w - 2)].reshape(NA_HEADS, nrow, w, 2 * w - 2)[..., :w]
    tbl = jnp.where(ok[None, None], t, NEG_BIG)
    tbl = tbl.reshape(NA_HEADS // 2, 2, nrow, w, w)
    return jnp.transpose(tbl, (2, 0, 4, 1, 3)).reshape(nrow, NA_HEADS // 2, w, 2 * w)


def _na_attn(q, k, v, kc, vc, bias_tab):
    b, n, d = q.shape
    l = kc.shape[1]
    rows = n // GRID_W
    kh = min(NA_KH, rows)
    return pl.pallas_call(
        functools.partial(_na_attn_kernel, rows=rows, kh=kh),
        grid=(b, rows),
        in_specs=[pl.BlockSpec((1, GRID_W, d), lambda bi, r: (bi, r, 0)),
                  pl.BlockSpec((1, n, d), lambda bi, r: (bi, 0, 0)),
                  pl.BlockSpec((1, n, d), lambda bi, r: (bi, 0, 0)),
                  pl.BlockSpec((1, l, d), lambda bi, r: (bi, 0, 0)),
                  pl.BlockSpec((1, l, d), lambda bi, r: (bi, 0, 0)),
                  _const_spec(bias_tab.shape)],
        out_specs=pl.BlockSpec((1, GRID_W, d), lambda bi, r: (bi, r, 0)),
        out_shape=jax.ShapeDtypeStruct(q.shape, BF16),
        compiler_params=_cparams("parallel", "arbitrary"),
        name="na_attn",
    )(q, k, v, kc, vc, bias_tab)


def _ctx_attn_kernel(q_ref, k_ref, v_ref, o_ref):
    l = q_ref.shape[1]
    lane0 = lax.broadcasted_iota(jnp.int32, (1, LANES), 1) < NA_HEAD_DIM
    zero = jnp.zeros((l, LANES), BF16)
    pairs = range(NA_HEADS // 2)
    sls = [slice(p * LANES, (p + 1) * LANES) for p in pairs]
    qbd = []
    for sl in sls:
        q = q_ref[0, :, sl]
        qbd.append(jnp.concatenate([jnp.where(lane0, q, zero), jnp.where(lane0, zero, q)], axis=0))
    s = [_dot_nt(k_ref[0, :, sls[p]], qbd[p]) for p in pairs]
    e = [jnp.exp(s[p] - jnp.max(s[p], axis=0, keepdims=True)) for p in pairs]
    prob = [e[p] * (1.0 / jnp.sum(e[p], axis=0, keepdims=True)) for p in pairs]
    o = [_dot_tn(prob[p], v_ref[0, :, sls[p]]) for p in pairs]
    for p in pairs:
        o_ref[0, :, sls[p]] = jnp.where(lane0, o[p][:l], o[p][l:]).astype(BF16)


def _ctx_attn(q, k, v):
    b, l, d = q.shape
    spec = pl.BlockSpec((1, l, d), lambda bi: (bi, 0, 0))
    return pl.pallas_call(
        _ctx_attn_kernel,
        grid=(b,),
        in_specs=[spec] * 3,
        out_specs=spec,
        out_shape=jax.ShapeDtypeStruct(q.shape, BF16),
        compiler_params=_cparams("parallel"),
        name="ctx_attn",
    )(q, k, v)


def _mixer_na(hx, hc, mx, mc, w_qkv, w_o, rpb, ln_g, ln_b, ctx_out):
    w_qkv = w_qkv.astype(BF16)
    w_o = w_o.astype(BF16)
    qx, kx, vx = _na_qkv(hx, mx, w_qkv)
    qc, kc, vc = _na_qkv(hc, mc, w_qkv)
    ox = _na_attn(qx, kx, vx, kc, vc, _na_bias_table(rpb))
    hx = _out_proj([ox], [], _plain_prologue, w_o, None, hx, mx, ln_g, ln_b, "na_out")
    if ctx_out:
        oc = _ctx_attn(qc, kc, vc)
        hc = _out_proj([oc], [], _plain_prologue, w_o, None, hc, mc, ln_g, ln_b, "na_out_ctx")
    return hx, hc


CONV_ROWS = 16


def _conv_core_kernel(prev_ref, cur_ref, next_ref, mod_ref, w1_ref, b1_ref, w_ref, bdw_ref, g_ref, b_ref, o_ref,
                      win_ref, *, ts, nt):
    t = pl.program_id(1)
    halo = CONV_HALO
    d = cur_ref.shape[-1]
    rows = ts + 2 * halo
    hwin = jnp.concatenate([prev_ref[0], cur_ref[0], next_ref[0]], axis=0)
    z = jnp.dot(_modulate(hwin, mod_ref, 1).astype(BF16), w1_ref[...], preferred_element_type=F32) + b1_ref[...]
    rid = lax.broadcasted_iota(jnp.int32, (rows, 1), 0)
    inside = ((rid >= halo) | (t > 0)) & ((rid < halo + ts) | (t < nt - 1))
    win_ref[0] = jnp.where(inside, z[:, :d] * _sigmoid(z[:, d:]), 0.0)
    sub = lax.broadcasted_iota(jnp.int32, (rows // SUBLANES, SUBLANES, LANES), 1)
    for cc in range(d // LANES):
        cols = slice(cc * LANES, (cc + 1) * LANES)
        w = win_ref[0, :, cols].reshape(rows // SUBLANES, SUBLANES, LANES)
        for i in range(1, SUBLANES):
            rot = pltpu.roll(w, SUBLANES - i, axis=1)
            nxt = jnp.concatenate([rot[1:], rot[:1]], axis=0)
            win_ref[i, :, cols] = jnp.where(sub < SUBLANES - i, rot, nxt).reshape(rows, LANES)
    off = halo - CONV_WIDTH // 2
    slabs = CONV_ROWS // SUBLANES
    for g in range(ts // CONV_ROWS):
        acc = jnp.broadcast_to(bdw_ref[...], (slabs, SUBLANES, d))
        for k in range(CONV_WIDTH):
            i, base = (off + k) % SUBLANES, g * CONV_ROWS + (off + k) // SUBLANES * SUBLANES
            acc = acc + win_ref[i, base:base + CONV_ROWS, :].reshape(slabs, SUBLANES, d) * w_ref[k][None]
        y = _layer_norm(acc.reshape(CONV_ROWS, d), g_ref[...], b_ref[...])
        o_ref[0, g * CONV_ROWS:(g + 1) * CONV_ROWS, :] = (y * _sigmoid(y)).astype(BF16)


def _conv_core(h, mod, w1, b1, w_dw, b_dw, ln_g, ln_b):
    b, s, d = h.shape
    ts = _tile(s, CONV_TILE)
    nt = s // ts
    hb = ts // CONV_HALO
    nhb = s // CONV_HALO
    w_rep = jnp.broadcast_to(w_dw[:, None, :], (CONV_WIDTH, SUBLANES, d))
    return pl.pallas_call(
        functools.partial(_conv_core_kernel, ts=ts, nt=nt),
        grid=(b, nt),
        in_specs=[pl.BlockSpec((1, CONV_HALO, d), lambda bi, t: (bi, jnp.maximum(t * hb - 1, 0), 0)),
                  _tok_spec(ts, d),
                  pl.BlockSpec((1, CONV_HALO, d), lambda bi, t: (bi, jnp.minimum((t + 1) * hb, nhb - 1), 0)),
                  _mod_spec(), _const_spec(w1.shape), _const_spec((1, 2 * d)),
                  _const_spec(w_rep.shape), _const_spec((1, d)), _const_spec((1, d)), _const_spec((1, d))],
        out_specs=_tok_spec(ts, d),
        out_shape=jax.ShapeDtypeStruct(h.shape, BF16),
        scratch_shapes=[pltpu.VMEM((SUBLANES, ts + 2 * CONV_HALO, d), F32)],
        compiler_params=_cparams("parallel", "arbitrary"),
        name="conv_core",
    )(h, h, h, mod, w1, b1.reshape(1, 2 * d), w_rep, b_dw.reshape(1, d), ln_g.reshape(1, d), ln_b.reshape(1, d))


def _mixer_conv(hx, hc, mx, mc, w1, b1, w_dw, b_dw, cln_g, cln_b, w2, b2, ln_g, ln_b, ctx_out):
    w1 = w1.astype(BF16)
    w2 = w2.astype(BF16)

    def branch(h, mod, name):
        y = _conv_core(h, mod, w1, b1, w_dw, b_dw, cln_g, cln_b)
        return _out_proj([y], [], _plain_prologue, w2, b2, h, mod, ln_g, ln_b, name)

    hx = branch(hx, mx, "conv_out")
    if ctx_out:
        hc = branch(hc, mc, "conv_out_ctx")
    return hx, hc


GLA_NQK = GLA_HEADS * GLA_DK


def _rope_swap(x):
    quarter = GLA_DK // 4
    lane = lax.broadcasted_iota(jnp.int32, (1, GLA_DK), 1)
    first = (lane % (2 * quarter)) < quarter
    parts = []
    for hd in range(GLA_HEADS):
        xh = x[:, hd * GLA_DK:(hd + 1) * GLA_DK]
        up = pltpu.roll(xh, GLA_DK - quarter, axis=1)
        dn = pltpu.roll(xh, quarter, axis=1)
        parts.append(jnp.where(first, up, dn))
    return jnp.concatenate(parts, axis=1)


def _gla_in_kernel(*refs, rotary):
    if rotary:
        h_ref, mod_ref, w_ref, wa2_ref, ba_ref, cos_ref, sin_ref = refs[:7]
        outs = refs[7:]
    else:
        h_ref, mod_ref, w_ref, wa2_ref, ba_ref = refs[:5]
        outs = refs[5:]
    q_ref, k_ref, v_ref, g_ref, laf_ref, lab_ref = outs
    n, d = GLA_NQK, D_MODEL
    u = _modulate(h_ref[0], mod_ref, 1).astype(BF16)
    z = jnp.dot(u, w_ref[...], preferred_element_type=F32)
    q = z[:, :n]
    k = z[:, n:2 * n]
    o = 2 * n
    if rotary:
        cos = cos_ref[...]
        sin = sin_ref[...]
        q = q * cos + _rope_swap(q) * sin
        k = k * cos + _rope_swap(k) * sin
    q_ref[0] = q * (GLA_DK ** -0.5)
    k_ref[0] = k
    v_ref[0] = z[:, o:o + d].astype(BF16)
    g_ref[0] = z[:, o + d:o + 2 * d].astype(BF16)
    t = z[:, o + 2 * d:o + 2 * d + LANES].astype(BF16)
    for dr, la_ref in enumerate((laf_ref, lab_ref)):
        x = jnp.dot(t, wa2_ref[dr], preferred_element_type=F32) + ba_ref[dr:dr + 1, :]
        la_ref[0] = -_softplus(-x) * (1.0 / GLA_NORMALIZER)


def _gla_rope_tables(n):
    quarter = GLA_DK // 4
    t = np.arange(n)
    freqs = ROPE_BASE ** (-jnp.arange(quarter, dtype=F32) / quarter)
    ang_r = jnp.asarray(t // GRID_W, F32)[:, None] * freqs
    ang_c = jnp.asarray(t % GRID_W, F32)[:, None] * freqs
    cos = jnp.concatenate([jnp.cos(ang_r), jnp.cos(ang_r), jnp.cos(ang_c), jnp.cos(ang_c)], axis=-1)
    sin = jnp.concatenate([-jnp.sin(ang_r), jnp.sin(ang_r), -jnp.sin(ang_c), jnp.sin(ang_c)], axis=-1)
    return jnp.tile(cos, (1, GLA_HEADS)), jnp.tile(sin, (1, GLA_HEADS))


def _gla_in(h, mod, w_ext, wa2_ext, ba, rope):
    b, s, d = h.shape
    tm = _tile(s, PROJ_TILE)
    rotary = rope is not None
    in_specs = [_tok_spec(tm, d), _mod_spec(), _const_spec(w_ext.shape), _const_spec(wa2_ext.shape),
                _const_spec(ba.shape)]
    args = [h, mod, w_ext, wa2_ext, ba]
    if rotary:
        in_specs += [pl.BlockSpec((tm, GLA_NQK), lambda bi, t: (t, 0))] * 2
        args += list(rope)
    shp = lambda w, dt: jax.ShapeDtypeStruct((b, s, w), dt)
    return pl.pallas_call(
        functools.partial(_gla_in_kernel, rotary=rotary),
        grid=(b, s // tm),
        in_specs=in_specs,
        out_specs=[_tok_spec(tm, GLA_NQK), _tok_spec(tm, GLA_NQK), _tok_spec(tm, d), _tok_spec(tm, d),
                   _tok_spec(tm, GLA_NQK), _tok_spec(tm, GLA_NQK)],
        out_shape=[shp(GLA_NQK, F32), shp(GLA_NQK, F32), shp(d, BF16), shp(d, BF16),
                   shp(GLA_NQK, F32), shp(GLA_NQK, F32)],
        compiler_params=_cparams("parallel", "parallel"),
        name="gla_in_rope" if rotary else "gla_in",
    )(*args)


def _tri_masks(c):
    row = lax.broadcasted_iota(jnp.int32, (c, c), 0)
    col = lax.broadcasted_iota(jnp.int32, (c, c), 1)
    return row >= col, row <= col


def _gla_core_kernel(qf_ref, kf_ref, vf_ref, laf_ref, qb_ref, kb_ref, vb_ref, lab_ref, s0f_ref, s0b_ref,
                     of_ref, ob_ref, sf_ref, sb_ref, stf_ref, stb_ref, *, nblocks, sub):
    i = pl.program_id(1)
    c = GLA_CHUNK

    @pl.when(i == 0)
    def _():
        stf_ref[...] = s0f_ref[0]
        stb_ref[...] = s0b_ref[0]

    lower, upper = _tri_masks(c)
    dirs = ((qf_ref, kf_ref, vf_ref, laf_ref, of_ref, stf_ref, lower, c - 1, range(sub)),
            (qb_ref, kb_ref, vb_ref, lab_ref, ob_ref, stb_ref, upper, 0, range(sub - 1, -1, -1)))
    chains = []
    for q_ref, k_ref, v_ref, la_ref, o_ref, st_ref, mask, last, order in dirs:
        tri = mask.astype(BF16)
        steps = []
        for j in order:
            rows = slice(j * c, (j + 1) * c)
            hi, lo = _split(la_ref[0, rows, :])
            bcum = jnp.dot(tri, hi, preferred_element_type=F32) + jnp.dot(tri, lo, preferred_element_type=F32)
            blast = bcum[last:last + 1, :]
            k = k_ref[0, rows, :]
            steps.append(dict(rows=rows, dec=jnp.exp(blast), qd=(q_ref[0, rows, :] * jnp.exp(bcum)).astype(BF16),
                              kd=(k * jnp.exp(-bcum)).astype(BF16), kt=(k * jnp.exp(blast - bcum)).astype(BF16)))
        for hd in range(GLA_HEADS):
            ks = slice(hd * GLA_DK, (hd + 1) * GLA_DK)
            vs = slice(hd * GLA_DV, (hd + 1) * GLA_DV)
            chains.append(dict(
                mask=mask, o_ref=o_ref, st_ref=st_ref, hd=hd, vs=vs,
                steps=[dict(rows=t["rows"], dec=t["dec"][:, ks], qd=t["qd"][:, ks], kd=t["kd"][:, ks],
                            kt=t["kt"][:, ks], v=v_ref[0, t["rows"], vs]) for t in steps]))
    for ch in chains:
        for t in ch["steps"]:
            t["att"] = jnp.where(ch["mask"], _dot_nt(t["qd"], t["kd"]), 0.0)
    for ch in chains:
        for t in ch["steps"]:
            t["local"] = _dot(t["att"], t["v"])
            t["upd"] = _dot_tn(t["v"], t["kt"])
    state = [ch["st_ref"][ch["hd"]] for ch in chains]
    for j in range(sub):
        out = [ch["steps"][j]["local"] + _dot_nt(ch["steps"][j]["qd"], s) for ch, s in zip(chains, state)]
        for ch, o in zip(chains, out):
            ch["o_ref"][0, ch["steps"][j]["rows"], ch["vs"]] = o.astype(BF16)
        state = [ch["steps"][j]["dec"] * s + ch["steps"][j]["upd"] for ch, s in zip(chains, state)]
    for ch, s in zip(chains, state):
        ch["st_ref"][ch["hd"]] = s

    @pl.when(i == nblocks - 1)
    def _():
        sf_ref[0] = stf_ref[...]
        sb_ref[0] = stb_ref[...]


def _gla_core(q, k, v, la_f, la_b, s0f, s0b):
    b, s, _ = q.shape
    sub = GLA_CHUNKS_PER_STEP if s % (GLA_CHUNKS_PER_STEP * GLA_CHUNK) == 0 else 1
    rows = sub * GLA_CHUNK
    nblocks = s // rows
    fwd = lambda w: pl.BlockSpec((1, rows, w), lambda bi, i: (bi, i, 0))
    bwd = lambda w: pl.BlockSpec((1, rows, w), lambda bi, i: (bi, nblocks - 1 - i, 0))
    st_shape = (GLA_HEADS, GLA_DV, GLA_DK)
    st_spec = pl.BlockSpec((1,) + st_shape, lambda bi, i: (bi, 0, 0, 0))
    return pl.pallas_call(
        functools.partial(_gla_core_kernel, nblocks=nblocks, sub=sub),
        grid=(b, nblocks),
        in_specs=[fwd(GLA_NQK), fwd(GLA_NQK), fwd(D_MODEL), fwd(GLA_NQK),
                  bwd(GLA_NQK), bwd(GLA_NQK), bwd(D_MODEL), bwd(GLA_NQK), st_spec, st_spec],
        out_specs=[fwd(D_MODEL), bwd(D_MODEL), st_spec, st_spec],
        out_shape=[jax.ShapeDtypeStruct((b, s, D_MODEL), BF16)] * 2
        + [jax.ShapeDtypeStruct((b,) + st_shape, F32)] * 2,
        scratch_shapes=[pltpu.VMEM(st_shape, F32), pltpu.VMEM(st_shape, F32)],
        compiler_params=_cparams("parallel", "arbitrary"),
        name="gla_core",
    )(q, k, v, la_f, q, k, v, la_b, s0f, s0b)


def _gla_finish_prologue(of_ref, ob_ref, g_ref, ng_ref):
    o = of_ref[0].astype(F32) + ob_ref[0].astype(F32)
    g = g_ref[0].astype(F32)
    ng = ng_ref[...]
    parts = []
    for hd in range(GLA_HEADS):
        vs = slice(hd * GLA_DV, (hd + 1) * GLA_DV)
        oh = o[:, vs]
        gh = g[:, vs]
        oh = oh * lax.rsqrt(jnp.mean(oh * oh, axis=-1, keepdims=True) + LN_EPS) * ng
        parts.append((oh * (gh * _sigmoid(gh))).astype(BF16))
    return jnp.concatenate(parts, axis=1)


def _mixer_gla(hx, hc, mx, mc, w_in, w_a1, w_a2, b_a, norm_g, w_o, ln_g, ln_b, ctx_out):
    b, n, d = hx.shape
    nqk = GLA_NQK
    pad = jnp.zeros((d, LANES - 2 * GLA_GATE_RANK), F32)
    w_ext = jnp.concatenate([w_in, w_a1[0], w_a1[1], pad], axis=1).astype(BF16)
    wa2_ext = jnp.zeros((2, LANES, nqk), F32)
    wa2_ext = wa2_ext.at[0, :GLA_GATE_RANK].set(w_a2[0])
    wa2_ext = wa2_ext.at[1, GLA_GATE_RANK:2 * GLA_GATE_RANK].set(w_a2[1]).astype(BF16)
    w_o = w_o.astype(BF16)

    qc, kc, vc, gc, lcf, lcb = _gla_in(hc, mc, w_ext, wa2_ext, b_a, None)
    qx, kx, vx, gx, lxf, lxb = _gla_in(hx, mx, w_ext, wa2_ext, b_a, _gla_rope_tables(n))
    s0 = jnp.zeros((b, GLA_HEADS, GLA_DV, GLA_DK), F32)
    ocf, ocb, scf, scb = _gla_core(qc, kc, vc, lcf, lcb, s0, s0)
    oxf, oxb, _, _ = _gla_core(qx, kx, vx, lxf, lxb, scf, scb)
    ng = norm_g.reshape(1, GLA_DV)
    hx = _out_proj([oxf, oxb, gx], [ng], _gla_finish_prologue, w_o, None, hx, mx, ln_g, ln_b, "gla_out")
    if ctx_out:
        hc = _out_proj([ocf, ocb, gc], [ng], _gla_finish_prologue, w_o, None, hc, mc, ln_g, ln_b, "gla_out_ctx")
    return hx, hc


def _head_sum_matrices():
    g = np.zeros((D_MODEL, LANES), np.float32)
    g[np.arange(D_MODEL), np.arange(D_MODEL) // RW_HEAD] = 1.0
    return jnp.asarray(g, BF16), jnp.asarray(g.T.copy(), BF16)


def _rw_prep_kernel(hp_ref, h_ref, hn_ref, mod_ref, mu_ref, wr_ref, wk_ref, wv_ref, g1_ref, g2_ref,
                    w1_ref, w2_ref, a1_ref, a2_ref, w0_ref, a0_ref, kkw_ref, kaw_ref, rkw_ref, gs_ref, gt_ref,
                    r_ref, kk_ref, v_ref, g_ref, bonus_ref,
                    lw0_ref, k0_ref, b0_ref, lw1_ref, k1_ref, b1_ref, *, tm, nt):
    t = pl.program_id(1)
    u = _modulate(h_ref[0], mod_ref, 1)
    up = _modulate(hp_ref[0, 7:8, :], mod_ref, 1)
    un = _modulate(hn_ref[0, 0:1, :], mod_ref, 1)
    up = jnp.where(t > 0, up, 0.0)
    un = jnp.where(t < nt - 1, un, 0.0)
    rowid = lax.broadcasted_iota(jnp.int32, (tm, 1), 0)
    prev = jnp.where(rowid == 0, up, pltpu.roll(u, 1, axis=0))
    nxt = jnp.where(rowid == tm - 1, un, pltpu.roll(u, tm - 1, axis=0))
    xx = 0.5 * (prev + nxt) - u
    mix = lambda j: (u + xx * mu_ref[j:j + 1, :]).astype(BF16)
    gs = gs_ref[...]
    gt = gt_ref[...]

    r = jnp.dot(mix(0), wr_ref[...], preferred_element_type=F32)
    k = jnp.dot(mix(2), wk_ref[...], preferred_element_type=F32)
    v = jnp.dot(mix(3), wv_ref[...], preferred_element_type=F32)
    hw = jnp.tanh(jnp.dot(mix(1), w1_ref[...], preferred_element_type=F32)).astype(BF16)
    ha = jnp.dot(mix(4), a1_ref[...], preferred_element_type=F32).astype(BF16)
    gg = _sigmoid(jnp.dot(mix(5), g1_ref[...], preferred_element_type=F32)).astype(BF16)
    g_ref[0] = jnp.dot(gg, g2_ref[...], preferred_element_type=F32).astype(BF16)

    kk = k * kkw_ref[...]
    nrm = jnp.maximum(jnp.sqrt(_dot_sel(kk * kk, gs)), 1e-12)
    kk = kk * _dot_sel(1.0 / nrm, gt)
    ksum = jnp.zeros_like(k)
    for dr, (lw_ref, kd_ref, bd_ref) in enumerate(((lw0_ref, k0_ref, b0_ref), (lw1_ref, k1_ref, b1_ref))):
        xw = w0_ref[dr:dr + 1, :] + jnp.dot(hw, w2_ref[dr], preferred_element_type=F32)
        lw_ref[0] = -float(np.exp(-0.5)) * _sigmoid(xw)
        a = _sigmoid(a0_ref[dr:dr + 1, :] + jnp.dot(ha, a2_ref[dr], preferred_element_type=F32))
        kd = k * (1.0 + (a - 1.0) * kaw_ref[...])
        kd_ref[0] = kd
        bd_ref[0] = kk * a
        ksum = ksum + kd
    r_ref[0] = r
    kk_ref[0] = kk
    v_ref[0] = v.astype(BF16)
    bonus_ref[0] = (_dot_sel(_dot_sel(r * rkw_ref[...] * ksum, gs), gt) * v).astype(BF16)


def _rw_prep(h, mod, consts):
    b, s, d = h.shape
    tm = _tile(s, RW_PREP_TILE)
    nt = s // tm
    hb = tm // 8
    nhb = s // 8
    in_specs = [pl.BlockSpec((1, 8, d), lambda bi, t: (bi, jnp.maximum(t * hb - 1, 0), 0)),
                _tok_spec(tm, d),
                pl.BlockSpec((1, 8, d), lambda bi, t: (bi, jnp.minimum((t + 1) * hb, nhb - 1), 0)),
                _mod_spec()] + [_const_spec(a.shape) for a in consts]
    f32o = jax.ShapeDtypeStruct(h.shape, F32)
    bf16o = jax.ShapeDtypeStruct(h.shape, BF16)
    out_shape = [f32o, f32o, bf16o, bf16o, bf16o] + [f32o] * 6
    return pl.pallas_call(
        functools.partial(_rw_prep_kernel, tm=tm, nt=nt),
        grid=(b, nt),
        in_specs=in_specs,
        out_specs=[_tok_spec(tm, d)] * 11,
        out_shape=out_shape,
        compiler_params=_cparams("parallel", "arbitrary"),
        name="rw_prep",
    )(h, h, h, mod, *consts)


def _rw_prelude(refs, forward, rows):
    r_ref, kk_ref, v_ref, lw_ref, k_ref, b_ref = (ref.at[0, rows, :] for ref in refs)
    c = RW_CHUNK
    row = lax.broadcasted_iota(jnp.int32, (c, 4 * c), 0)
    col = lax.broadcasted_iota(jnp.int32, (c, 4 * c), 1) % c
    lower, upper = _tri_masks(c)
    if forward:
        strict, incl, tri, last = row > col, row >= col, lower, c - 1
    else:
        strict, incl, tri, last = row < col, row <= col, upper, 0
    tri = tri.astype(BF16)
    lw = lw_ref[...]
    hi, lo = _split(lw)
    cum = jnp.dot(tri, hi, preferred_element_type=F32) + jnp.dot(tri, lo, preferred_element_type=F32)
    tot = cum[last:last + 1, :]
    e_neg = jnp.exp(-cum)
    e_tail = jnp.exp(tot - cum)
    b_raw = b_ref[...]
    k_raw = k_ref[...]
    return dict(
        rows=rows, strict=strict, incl=incl, dec=jnp.exp(tot),
        a=(kk_ref[...] * jnp.exp(cum - lw)).astype(BF16), r=(r_ref[...] * jnp.exp(cum)).astype(BF16),
        bt=(b_raw * e_neg).astype(BF16), kt=(k_raw * e_neg).astype(BF16),
        bh=(b_raw * e_tail).astype(BF16), kh=(k_raw * e_tail).astype(BF16), v=v_ref[...])


def _rw_scan_block(pres_f, pres_b, stf_ref, stb_ref, yf_ref, yb_ref):
    c = RW_CHUNK
    rowp = lax.broadcasted_iota(jnp.int32, (c, 2 * c), 0)
    colp = lax.broadcasted_iota(jnp.int32, (c, 2 * c), 1) % c
    eye = (rowp == colp).astype(F32)
    blocks = [(rowp // s) == (colp // s) for s in (2 ** e for e in range(1, int(np.log2(c)) + 1))]
    row2 = lax.broadcasted_iota(jnp.int32, (LANES, LANES), 0)
    col2 = lax.broadcasted_iota(jnp.int32, (LANES, LANES), 1)
    same_head = (row2 < RW_HEAD) == (col2 < RW_HEAD)
    lane0 = lax.broadcasted_iota(jnp.int32, (1, LANES), 1) < RW_HEAD
    zero = jnp.zeros((c, LANES), BF16)

    def bdiag(x):
        x = x.astype(BF16)
        return jnp.concatenate([jnp.where(lane0, x, zero), jnp.where(lane0, zero, x)], axis=0)

    chains = [(pres, st_ref, y_ref, slice(p * LANES, (p + 1) * LANES), p)
              for pres, st_ref, y_ref in ((pres_f, stf_ref, yf_ref), (pres_b, stb_ref, yb_ref))
              for p in range(RW_PAIRS)]
    free = [(pre, {nm: pre[nm][:, sl] for nm in ("a", "r", "bt", "kt", "bh", "kh", "v")})
            for pres, _, _, sl, _ in chains for pre in pres]
    n = len(free)
    ar = [jnp.concatenate([t["a"], t["r"]], axis=0) for _, t in free]
    gram = [_dot_nt(ar[i], jnp.concatenate([bdiag(free[i][1]["bt"]), bdiag(free[i][1]["kt"])], axis=0))
            for i in range(n)]
    n1 = [jnp.where(free[i][0]["strict"][:, :2 * c], gram[i][:c, :2 * c], 0.0) for i in range(n)]
    a_ak = [jnp.where(free[i][0]["strict"][:, :2 * c], gram[i][:c, 2 * c:], 0.0) for i in range(n)]
    r_bk = [jnp.where(free[i][0]["incl"], gram[i][c:], 0.0) for i in range(n)]
    vbd = [bdiag(t["v"]) for _, t in free]
    w0 = [_dot(a_ak[i], vbd[i]) for i in range(n)]
    inv = [eye - jnp.where(blocks[0], n1[i], 0.0) for i in range(n)]
    for lvl in range(1, len(blocks)):
        sel = blocks[lvl] & ~blocks[lvl - 1]
        tmp = [_dot(jnp.where(sel, n1[i], 0.0), bdiag(inv[i])) for i in range(n)]
        inv = [inv[i] - _dot(inv[i], bdiag(tmp[i])) for i in range(n)]
    nsub = len(pres_f)
    zt = [st_ref[p] for _, st_ref, _, _, p in chains]
    for j in range(nsub):
        idx = [ci * nsub + j for ci in range(len(chains))]
        sz = [_dot_nt(ar[i], z) for i, z in zip(idx, zt)]
        u = [-_dot(inv[i], bdiag(s[:c] + w0[i])) for i, s in zip(idx, sz)]
        ys = [_dot(r_bk[i], jnp.concatenate([bdiag(uu), vbd[i]], axis=0)) for i, uu in zip(idx, u)]
        for (_, _, y_ref, sl, _), i, s, y in zip(chains, idx, sz, ys):
            y_ref[0, free[i][0]["rows"], sl] = (s[c:] + y).astype(BF16)
        upd = [_dot_tn(jnp.concatenate([uu.astype(BF16), free[i][1]["v"]], axis=0),
                       jnp.concatenate([free[i][1]["bh"], free[i][1]["kh"]], axis=0)) for i, uu in zip(idx, u)]
        zt = [free[i][0]["dec"][:, sl] * z + jnp.where(same_head, up, 0.0)
              for (_, _, _, sl, _), i, z, up in zip(chains, idx, zt, upd)]
    for (_, st_ref, _, _, p), z in zip(chains, zt):
        st_ref[p] = z


def _rw_scan_kernel(*refs, nblocks, sub):
    fwd_in, bwd_in = refs[0:6], refs[6:12]
    s0f_ref, s0b_ref, yf_ref, yb_ref, sf_ref, sb_ref, stf_ref, stb_ref = refs[12:]
    i = pl.program_id(1)
    c = RW_CHUNK

    @pl.when(i == 0)
    def _():
        stf_ref[...] = s0f_ref[0]
        stb_ref[...] = s0b_ref[0]

    pres_f = [_rw_prelude(fwd_in, True, slice(j * c, (j + 1) * c)) for j in range(sub)]
    pres_b = [_rw_prelude(bwd_in, False, slice(j * c, (j + 1) * c)) for j in range(sub - 1, -1, -1)]
    _rw_scan_block(pres_f, pres_b, stf_ref, stb_ref, yf_ref, yb_ref)

    @pl.when(i == nblocks - 1)
    def _():
        sf_ref[0] = stf_ref[...]
        sb_ref[0] = stb_ref[...]


def _rw_scan(r, kk, v, dirs, s0f, s0b):
    b, s, d = r.shape
    sub = RW_CHUNKS_PER_STEP if s % (RW_CHUNKS_PER_STEP * RW_CHUNK) == 0 else 1
    rows = sub * RW_CHUNK
    nblocks = s // rows
    fwd = pl.BlockSpec((1, rows, d), lambda bi, i: (bi, i, 0))
    bwd = pl.BlockSpec((1, rows, d), lambda bi, i: (bi, nblocks - 1 - i, 0))
    st_shape = (RW_PAIRS, LANES, LANES)
    st_spec = pl.BlockSpec((1,) + st_shape, lambda bi, i: (bi, 0, 0, 0))
    (lw0, k0, b0), (lw1, k1, b1) = dirs
    return pl.pallas_call(
        functools.partial(_rw_scan_kernel, nblocks=nblocks, sub=sub),
        grid=(b, nblocks),
        in_specs=[fwd] * 6 + [bwd] * 6 + [st_spec, st_spec],
        out_specs=[fwd, bwd, st_spec, st_spec],
        out_shape=[jax.ShapeDtypeStruct((b, s, d), BF16)] * 2 + [jax.ShapeDtypeStruct((b,) + st_shape, F32)] * 2,
        scratch_shapes=[pltpu.VMEM(st_shape, F32), pltpu.VMEM(st_shape, F32)],
        compiler_params=_cparams("parallel", "arbitrary"),
        name="rw_scan",
    )(r, kk, v, lw0, k0, b0, r, kk, v, lw1, k1, b1, s0f, s0b)


def _rw_finish_prologue(yf_ref, yb_ref, bonus_ref, g_ref, gs_ref, gt_ref, gng_ref, gnb_ref):
    y = yf_ref[0].astype(F32) + yb_ref[0].astype(F32)
    gs = gs_ref[...]
    gt = gt_ref[...]
    inv_n = 1.0 / RW_HEAD
    mean = _dot_sel(_dot_sel(y, gs) * inv_n, gt)
    yc = y - mean
    var = _dot_sel(yc * yc, gs) * inv_n
    rstd = _dot_sel(lax.rsqrt(var + RW_GN_EPS), gt)
    yn = yc * rstd * gng_ref[...] + gnb_ref[...]
    return ((yn + bonus_ref[0].astype(F32)) * g_ref[0].astype(F32)).astype(BF16)


def _mixer_rwkv(hx, hc, mx, mc, mu, w_rkv, w0, w1, w2, a0, a1, a2, g1, g2, k_k, k_a, r_k, gn_g, gn_b, w_o,
                ln_g, ln_b, ctx_out):
    b, n, d = hx.shape
    gs, gt = _head_sum_matrices()
    rank_w, rank_a = w1.shape[-1], a1.shape[-1]
    assert 2 * rank_w == LANES and 2 * rank_a == LANES

    def ext(m2):
        z = jnp.zeros((2, LANES, d), F32)
        z = z.at[0, :m2.shape[1]].set(m2[0])
        return z.at[1, m2.shape[1]:].set(m2[1]).astype(BF16)

    consts = [mu, w_rkv[0].astype(BF16), w_rkv[1].astype(BF16), w_rkv[2].astype(BF16),
              g1.astype(BF16), g2.astype(BF16),
              jnp.concatenate([w1[0], w1[1]], axis=1).astype(BF16), ext(w2),
              jnp.concatenate([a1[0], a1[1]], axis=1).astype(BF16), ext(a2),
              w0, a0, k_k.reshape(1, d), k_a.reshape(1, d), r_k.reshape(1, d), gs, gt]
    w_o = w_o.astype(BF16)

    def prep(h, mod):
        r, kk, v, g, bonus, lw0, k0, b0, lw1, k1, b1 = _rw_prep(h, mod, consts)
        return r, kk, v, g, bonus, ((lw0, k0, b0), (lw1, k1, b1))

    rc, kkc, vc, gc, bonc, dc = prep(hc, mc)
    rx, kkx, vx, gx, bonx, dx = prep(hx, mx)
    s0 = jnp.zeros((b, RW_PAIRS, LANES, LANES), F32)
    ycf, ycb, scf, scb = _rw_scan(rc, kkc, vc, dc, s0, s0)
    yxf, yxb, _, _ = _rw_scan(rx, kkx, vx, dx, scf, scb)
    fin_consts = [gs, gt, gn_g.reshape(1, d), gn_b.reshape(1, d)]
    hx = _out_proj([yxf, yxb, bonx, gx], fin_consts, _rw_finish_prologue, w_o, None, hx, mx, ln_g, ln_b, "rw_out",
                   tile=RW_PREP_TILE)
    if ctx_out:
        hc = _out_proj([ycf, ycb, bonc, gc], fin_consts, _rw_finish_prologue, w_o, None, hc, mc, ln_g, ln_b,
                       "rw_out_ctx", tile=RW_PREP_TILE)
    return hx, hc


def kernel(x, c, ctx, c_ctx, ada_w, ada_b, ln_g, ln_b, ffn_w13, ffn_w2, na_wqkv, na_wo, na_rpb, cv_w1, cv_b1, cv_wdw, cv_bdw, cv_ln_g, cv_ln_b, cv_w2, cv_b2, gla_win, gla_wa1, gla_wa2, gla_ba, gla_norm_g, gla_wo, rw_mu, rw_wrkv, rw_w0, rw_w1, rw_w2, rw_a0, rw_a1, rw_a2, rw_g1, rw_g2, rw_kk, rw_ka, rw_rk, rw_gn_g, rw_gn_b, rw_wo):
    b, n, d = x.shape
    depth = ada_w.shape[0]
    assert d == D_MODEL and depth == DEPTH and n % GRID_W == 0
    rows = -(-(b + 1) // 8) * 8
    cond = jnp.zeros((rows, d), F32).at[:b].set(c).at[b].set(c_ctx)
    mod_all = _ada(cond, ada_w, ada_b)
    w13 = ffn_w13.astype(BF16)
    w2 = ffn_w2.astype(BF16)
    hx, hc = x, ctx
    for i in range(depth):
        last = i == depth - 1
        mx = mod_all[i, :b].reshape(b, 3 * N_SUB, d)
        mc = jnp.broadcast_to(mod_all[i, b].reshape(1, 3 * N_SUB, d), (b, 3 * N_SUB, d))
        hx, hc = _ffn_pair(hx, hc, mx, mc, w13, w2, i, 0, ln_g[i, 0], ln_b[i, 0], 0)
        kind, j = i % 4, i // 4
        lg, lb = ln_g[i, 1], ln_b[i, 1]
        if kind == 0:
            hx, hc = _mixer_na(hx, hc, mx, mc, na_wqkv[j], na_wo[j], na_rpb[j], lg, lb, not last)
        elif kind == 1:
            hx, hc = _mixer_conv(hx, hc, mx, mc, cv_w1[j], cv_b1[j], cv_wdw[j], cv_bdw[j], cv_ln_g[j], cv_ln_b[j],
                                 cv_w2[j], cv_b2[j], lg, lb, not last)
        elif kind == 2:
            hx, hc = _mixer_gla(hx, hc, mx, mc, gla_win[j], gla_wa1[j], gla_wa2[j], gla_ba[j], gla_norm_g[j],
                                gla_wo[j], lg, lb, not last)
        else:
            hx, hc = _mixer_rwkv(hx, hc, mx, mc, rw_mu[j], rw_wrkv[j], rw_w0[j], rw_w1[j], rw_w2[j], rw_a0[j],
                                 rw_a1[j], rw_a2[j], rw_g1[j], rw_g2[j], rw_kk[j], rw_ka[j], rw_rk[j],
                                 rw_gn_g[j], rw_gn_b[j], rw_wo[j], lg, lb, not last)
        if last:
            hx = _ffn(hx, mx, w13, w2, i, 1, ln_g[i, 2], ln_b[i, 2], 2)
        else:
            hx, hc = _ffn_pair(hx, hc, mx, mc, w13, w2, i, 1, ln_g[i, 2], ln_b[i, 2], 2)
    return hx
```
